```python
import math
import jax, jax.numpy as jnp
from jax import lax
import numpy as np

D_MODEL = 1024
BATCH = 2
SEQ = 8192
DEPTH = 1

NSA_HEAD_DIM = 64
NSA_HEADS = D_MODEL // NSA_HEAD_DIM
NSA_KV_HEADS = 4
CMP_BLK = 32
CMP_STRIDE = 16
CMP_HIDDEN = 256
SEL_BLK = 64
N_SEL = 16
WINDOW = 512
NSA_Q_BLK = 64

GDN_HEAD_DIM = 128
GDN_HEADS = D_MODEL // GDN_HEAD_DIM
GDN_CONV = 4
GDN_CHUNK = 64

FFN_DIM = 2816
FFN_CONV = 3

NORM_EPS = 1e-6
MASK_VALUE = -1e30
FORCED_SCORE = 1e6

NSA_Q_COLS = NSA_HEADS * NSA_HEAD_DIM
NSA_KV_COLS = 6 * NSA_KV_HEADS * NSA_HEAD_DIM
NSA_GATE_COLS = 3 * NSA_HEADS
GDN_QKV_COLS = 3 * GDN_HEADS * GDN_HEAD_DIM
GDN_Z_COLS = GDN_HEADS * GDN_HEAD_DIM
MERGE_COLS = 2 * D_MODEL
IN_SPLITS = (NSA_Q_COLS, NSA_KV_COLS, NSA_GATE_COLS, GDN_QKV_COLS, GDN_HEADS, GDN_HEADS, GDN_Z_COLS, MERGE_COLS)
IN_COLS = NSA_Q_COLS + NSA_KV_COLS + NSA_GATE_COLS + GDN_QKV_COLS + 2 * GDN_HEADS + GDN_Z_COLS + MERGE_COLS

kernel_name = 'hybrid_nsa_gdn_convffn_adaln'


def rms_norm(x, w):
    xf = x.astype(jnp.float32)
    y = xf * lax.rsqrt(jnp.mean(xf * xf, axis=-1, keepdims=True) + NORM_EPS)
    return (y * w.astype(jnp.float32)).astype(x.dtype)


def l2_normalize(x):
    xf = x.astype(jnp.float32)
    return xf * lax.rsqrt(jnp.sum(xf * xf, axis=-1, keepdims=True) + NORM_EPS)


def masked_softmax(s, mask):
    s = jnp.where(mask, s.astype(jnp.float32), MASK_VALUE)
    return jnp.where(mask, jax.nn.softmax(s, axis=-1), 0.0)


def causal_depthwise_conv(x, w):
    width, ch = w.shape
    return lax.conv_general_dilated(
        x, w.astype(x.dtype)[:, None, :], window_strides=(1,),
        padding=((width - 1, 0),), dimension_numbers=('NWC', 'WIO', 'NWC'),
        feature_group_count=ch)


def nsa_attention(q, k_c, v_c, k_s, v_s, k_w, v_w, gate_logits, pos_k, pos_v,
                  ck_w1, ck_b1, ck_w2, ck_b2, cv_w1, cv_b1, cv_w2, cv_b2):
    bsz, seq, n_heads, dk = q.shape
    n_grp = k_c.shape[2]
    rep = n_heads // n_grp
    n_cmp = (seq - CMP_BLK) // CMP_STRIDE + 1
    n_blk = seq // SEL_BLK
    n_sel = min(N_SEL, n_blk)
    scale = dk ** -0.5

    c_start = np.arange(n_cmp) * CMP_STRIDE
    cmp_idx = c_start[:, None] + np.arange(CMP_BLK)[None, :]
    s_start = np.arange(n_blk) * SEL_BLK
    overlap = np.clip(np.minimum(c_start[:, None] + CMP_BLK, s_start[None, :] + SEL_BLK)
                      - np.maximum(c_start[:, None], s_start[None, :]), 0, None) / CMP_BLK
    overlap = jnp.asarray(overlap, jnp.float32)
    c_end = jnp.asarray(c_start + CMP_BLK - 1)

    def compress(t, pos, w1, b1, w2, b2):
        blk = t[:, cmp_idx] + pos[None, None, :, None, :]
        blk = blk.transpose(0, 3, 1, 2, 4).reshape(bsz, n_grp, n_cmp, CMP_BLK * dk)
        return jax.nn.gelu(blk @ w1 + b1) @ w2 + b2

    kc = compress(k_c, pos_k, ck_w1, ck_b1, ck_w2, ck_b2)
    vc = compress(v_c, pos_v, cv_w1, cv_b1, cv_w2, cv_b2)
    ks_blk = k_s.reshape(bsz, n_blk, SEL_BLK, n_grp, dk).transpose(0, 3, 1, 2, 4)
    vs_blk = v_s.reshape(bsz, n_blk, SEL_BLK, n_grp, dk).transpose(0, 3, 1, 2, 4)
    pad = ((0, 0), (0, 0), (WINDOW, 0), (0, 0))
    kw_pad = jnp.pad(k_w.transpose(0, 2, 1, 3), pad)
    vw_pad = jnp.pad(v_w.transpose(0, 2, 1, 3), pad)
    qg = q.reshape(bsz, seq, n_grp, rep, dk).transpose(0, 2, 3, 1, 4)
    gates = jax.nn.sigmoid(gate_logits.astype(jnp.float32)).astype(q.dtype)
    gates = gates.reshape(bsz, seq, n_grp, rep, 3).transpose(0, 2, 3, 1, 4)
    b_ix = jnp.arange(bsz)[:, None, None, None]
    g_ix = jnp.arange(n_grp)[None, :, None, None]
    j_blk = jnp.arange(n_blk)
    w_off = jnp.arange(WINDOW + NSA_Q_BLK)

    def block(qb):
        qs = qb * NSA_Q_BLK
        t = qs + jnp.arange(NSA_Q_BLK)
        qi = lax.dynamic_slice_in_dim(qg, qs, NSA_Q_BLK, axis=3)
        gi = lax.dynamic_slice_in_dim(gates, qs, NSA_Q_BLK, axis=3)
        s_c = jnp.einsum('bgrqd,bgnd->bgrqn', qi, kc).astype(jnp.float32) * scale
        p_c = masked_softmax(s_c, c_end[None, :] <= t[:, None])
        o_c = jnp.einsum('bgrqn,bgnd->bgrqd', p_c.astype(vc.dtype), vc)
        imp = jnp.einsum('bgrqn,nj->bgqj', p_c, overlap)
        cur = (t // SEL_BLK)[:, None]
        forced = (j_blk == 0) | (j_blk == cur) | (j_blk == cur - 1)
        valid = j_blk * SEL_BLK <= t[:, None]
        score = jnp.where(forced, FORCED_SCORE, jnp.where(valid, imp, -1.0))
        top_v, top_i = lax.top_k(score, n_sel)
        ks = ks_blk[b_ix, g_ix, top_i]
        vs = vs_blk[b_ix, g_ix, top_i]
        kpos = top_i[..., None] * SEL_BLK + jnp.arange(SEL_BLK)
        m_s = (top_v >= 0.0)[..., None] & (kpos <= t[:, None, None])
        m_s = m_s.reshape(bsz, n_grp, 1, NSA_Q_BLK, n_sel * SEL_BLK)
        s_s = jnp.einsum('bgrqd,bgqnkd->bgrqnk', qi, ks).astype(jnp.float32) * scale
        p_s = masked_softmax(s_s.reshape(bsz, n_grp, rep, NSA_Q_BLK, n_sel * SEL_BLK), m_s)
        p_s = p_s.reshape(bsz, n_grp, rep, NSA_Q_BLK, n_sel, SEL_BLK)
        o_s = jnp.einsum('bgrqnk,bgqnkd->bgrqd', p_s.astype(vs.dtype), vs)
        kw = lax.dynamic_slice_in_dim(kw_pad, qs, WINDOW + NSA_Q_BLK, axis=2)
        vw = lax.dynamic_slice_in_dim(vw_pad, qs, WINDOW + NSA_Q_BLK, axis=2)
        wpos = (qs - WINDOW + w_off)[None, :]
        m_w = (wpos <= t[:, None]) & (wpos > t[:, None] - WINDOW) & (wpos >= 0)
        s_w = jnp.einsum('bgrqd,bgkd->bgrqk', qi, kw).astype(jnp.float32) * scale
        p_w = masked_softmax(s_w, m_w)
        o_w = jnp.einsum('bgrqk,bgkd->bgrqd', p_w.astype(vw.dtype), vw)
        return gi[..., 0:1] * o_c + gi[..., 1:2] * o_s + gi[..., 2:3] * o_w

    out = lax.map(block, jnp.arange(seq // NSA_Q_BLK))
    return out.transpose(1, 0, 4, 2, 3, 5).reshape(bsz, seq, n_heads * dk)


def chunk_gated_delta_rule(q, k, v, g, beta):
    bsz, seq, n_heads, dk = q.shape
    dv = v.shape[-1]
    cs = GDN_CHUNK
    n_chunk = seq // cs
    f32 = jnp.float32

    def chunks(t):
        t = t.astype(f32).reshape((bsz, n_chunk, cs, n_heads) + t.shape[3:])
        return jnp.swapaxes(t, 2, 3)

    q = chunks(q) * (dk ** -0.5)
    k = chunks(k)
    v = chunks(v)
    g = chunks(g)
    beta = chunks(beta)
    gc = jnp.cumsum(g, axis=-1)
    incl = jnp.tril(jnp.ones((cs, cs), bool))
    strict = jnp.tril(jnp.ones((cs, cs), bool), -1)
    decay = jnp.exp(jnp.where(incl, gc[..., :, None] - gc[..., None, :], -jnp.inf))
    kk = jnp.einsum('bnhid,bnhjd->bnhij', k, k)
    a_mat = jnp.eye(cs, dtype=f32) + jnp.where(strict, beta[..., :, None] * kk * decay, 0.0)
    rhs = jnp.concatenate([v * beta[..., None], k * (beta * jnp.exp(gc))[..., None]], axis=-1)
    sol = lax.linalg.triangular_solve(a_mat, rhs, left_side=True, lower=True, unit_diagonal=True)
    u, w = sol[..., :dv], sol[..., dv:]
    qk = jnp.einsum('bnhid,bnhjd->bnhij', q, k) * decay
    q_dec = q * jnp.exp(gc)[..., None]
    k_dec = k * jnp.exp(gc[..., -1:] - gc)[..., None]
    g_tot = jnp.exp(gc[..., -1])

    def step(state, xs):
        u_i, w_i, qk_i, qd_i, kd_i, gt_i = xs
        v_new = u_i - jnp.einsum('bhck,bhkv->bhcv', w_i, state)
        o_i = jnp.einsum('bhck,bhkv->bhcv', qd_i, state) + jnp.einsum('bhij,bhjv->bhiv', qk_i, v_new)
        state = state * gt_i[..., None, None] + jnp.einsum('bhck,bhcv->bhkv', kd_i, v_new)
        return state, o_i

    xs = tuple(jnp.moveaxis(t, 1, 0) for t in (u, w, qk, q_dec, k_dec, g_tot))
    state0 = jnp.zeros((bsz, n_heads, dk, dv), f32)
    _, o = lax.scan(step, state0, xs)
    return o.transpose(1, 0, 3, 2, 4).reshape(bsz, seq, n_heads, dv)


def gated_deltanet(qkv, a, b, z, conv_w, a_log, dt_bias, norm_w):
    bsz, seq, _ = qkv.shape
    qkv = jax.nn.silu(causal_depthwise_conv(qkv, conv_w))
    q, k, v = jnp.split(qkv, 3, axis=-1)
    shp = (bsz, seq, GDN_HEADS, GDN_HEAD_DIM)
    q = l2_normalize(q.reshape(shp))
    k = l2_normalize(k.reshape(shp))
    v = v.reshape(shp)
    g = -jnp.exp(a_log.astype(jnp.float32)) * jax.nn.softplus(a.astype(jnp.float32) + dt_bias.astype(jnp.float32))
    beta = jax.nn.sigmoid(b.astype(jnp.float32))
    o = chunk_gated_delta_rule(q, k, v, g, beta)
    o = rms_norm(o, norm_w) * jax.nn.silu(z.reshape(shp).astype(jnp.float32))
    return o.reshape(bsz, seq, GDN_HEADS * GDN_HEAD_DIM).astype(qkv.dtype)


def setup_inputs(seed: int = 0) -> dict:
    key = jax.random.key(seed)
    ks = iter(jax.random.split(key, 40))

    def nrm(shape, scale):
        return jax.random.normal(next(ks), shape, jnp.float32) * scale

    L, D, dk = DEPTH, D_MODEL, NSA_HEAD_DIM
    dt = jnp.exp(jax.random.uniform(next(ks), (L, GDN_HEADS), jnp.float32, math.log(1e-3), math.log(1e-1)))
    return {
        'x': nrm((BATCH, SEQ, D), 1.0),
        'c': nrm((BATCH, D), 1.0),
        'w_ada': nrm((L, D, 6 * D), 0.5 * D ** -0.5),
        'b_ada': nrm((L, 6 * D), 0.01),
        'norm_mix': 1.0 + nrm((L, D), 0.02),
        'w_in': nrm((L, D, IN_COLS), D ** -0.5),
        'nsa_pos_k': nrm((L, CMP_BLK, dk), 0.5),
        'nsa_pos_v': nrm((L, CMP_BLK, dk), 0.5),
        'nsa_ck_w1': nrm((L, CMP_BLK * dk, CMP_HIDDEN), (CMP_BLK * dk) ** -0.5),
        'nsa_ck_b1': nrm((L, CMP_HIDDEN), 0.01),
        'nsa_ck_w2': nrm((L, CMP_HIDDEN, dk), CMP_HIDDEN ** -0.5),
        'nsa_ck_b2': nrm((L, dk), 0.01),
        'nsa_cv_w1': nrm((L, CMP_BLK * dk, CMP_HIDDEN), (CMP_BLK * dk) ** -0.5),
        'nsa_cv_b1': nrm((L, CMP_HIDDEN), 0.01),
        'nsa_cv_w2': nrm((L, CMP_HIDDEN, dk), CMP_HIDDEN ** -0.5),
        'nsa_cv_b2': nrm((L, dk), 0.01),
        'gdn_conv': nrm((L, GDN_CONV, GDN_QKV_COLS), GDN_CONV ** -0.5),
        'gdn_a_log': jnp.log(jax.random.uniform(next(ks), (L, GDN_HEADS), jnp.float32, 1.0, 16.0)),
        'gdn_dt_bias': dt + jnp.log(-jnp.expm1(-dt)),
        'gdn_norm': 1.0 + nrm((L, GDN_HEAD_DIM), 0.02),
        'w_proj_nsa': nrm((L, NSA_Q_COLS, D), NSA_Q_COLS ** -0.5),
        'w_proj_gdn': nrm((L, GDN_Z_COLS, D), GDN_Z_COLS ** -0.5),
        'w_out': nrm((L, D, D), D ** -0.5),
        'norm_ffn': 1.0 + nrm((L, D), 0.02),
        'ffn_up': nrm((L, D, 2 * FFN_DIM), D ** -0.5),
        'ffn_conv': nrm((L, FFN_CONV, 2 * FFN_DIM), FFN_CONV ** -0.5),
        'ffn_conv_b': nrm((L, 2 * FFN_DIM), 0.01),
        'ffn_down': nrm((L, FFN_DIM, D), FFN_DIM ** -0.5),
        'norm_final': 1.0 + nrm((D,), 0.02),
    }


def reference(x, c, w_ada, b_ada, norm_mix, w_in, nsa_pos_k, nsa_pos_v,
              nsa_ck_w1, nsa_ck_b1, nsa_ck_w2, nsa_ck_b2,
              nsa_cv_w1, nsa_cv_b1, nsa_cv_w2, nsa_cv_b2,
              gdn_conv, gdn_a_log, gdn_dt_bias, gdn_norm,
              w_proj_nsa, w_proj_gdn, w_out, norm_ffn,
              ffn_up, ffn_conv, ffn_conv_b, ffn_down, norm_final):
    bsz, seq, _ = x.shape
    split_at = tuple(int(v) for v in np.cumsum(IN_SPLITS)[:-1])
    for l in range(DEPTH):
        mod = (c @ w_ada[l] + b_ada[l])[:, None, :]
        sh_m, sc_m, gt_m, sh_f, sc_f, gt_f = jnp.split(mod, 6, axis=-1)
        h = rms_norm(x, norm_mix[l]) * (1.0 + sc_m) + sh_m
        q_a, kv_a, gl_a, qkv_b, a_b, b_b, z_b, merge = jnp.split(h @ w_in[l], split_at, axis=-1)
        q_a = q_a.reshape(bsz, seq, NSA_HEADS, NSA_HEAD_DIM)
        kv_a = kv_a.reshape(bsz, seq, 6, NSA_KV_HEADS, NSA_HEAD_DIM)
        y_a = nsa_attention(q_a, kv_a[:, :, 0], kv_a[:, :, 1], kv_a[:, :, 2], kv_a[:, :, 3],
                            kv_a[:, :, 4], kv_a[:, :, 5], gl_a.reshape(bsz, seq, NSA_HEADS, 3),
                            nsa_pos_k[l], nsa_pos_v[l],
                            nsa_ck_w1[l], nsa_ck_b1[l], nsa_ck_w2[l], nsa_ck_b2[l],
                            nsa_cv_w1[l], nsa_cv_b1[l], nsa_cv_w2[l], nsa_cv_b2[l])
        y_b = gated_deltanet(qkv_b, a_b, b_b, z_b, gdn_conv[l], gdn_a_log[l], gdn_dt_bias[l], gdn_norm[l])
        g_a, g_b = jnp.split(jax.nn.sigmoid(merge.astype(jnp.float32)).astype(x.dtype), 2, axis=-1)
        mixed = g_a * (y_a @ w_proj_nsa[l]) + g_b * (y_b @ w_proj_gdn[l])
        x = x + gt_m * (mixed @ w_out[l])
        h = rms_norm(x, norm_ffn[l]) * (1.0 + sc_f) + sh_f
        u = causal_depthwise_conv(h @ ffn_up[l], ffn_conv[l]) + ffn_conv_b[l]
        u_gate, u_val = jnp.split(u, 2, axis=-1)
        x = x + gt_f * ((jax.nn.silu(u_gate) * u_val) @ ffn_down[l])
    return rms_norm(x, norm_final)
```

```python
import functools

import numpy as np
import jax
import jax.numpy as jnp
from jax import lax
from jax.experimental import pallas as pl
from jax.experimental.pallas import tpu as pltpu

F32 = jnp.float32
BF16 = jnp.bfloat16
HIGHEST = lax.Precision.HIGHEST

D_MODEL = 1024
NSA_HEAD_DIM = 64
NSA_HEADS = 16
NSA_KV_HEADS = 4
NSA_REP = NSA_HEADS // NSA_KV_HEADS
CMP_BLK = 32
CMP_STRIDE = 16
CMP_HIDDEN = 256
SEL_BLK = 64
N_SEL = 16
WINDOW = 512
GDN_HEAD_DIM = 128
GDN_HEADS = 8
GDN_CONV = 4
GDN_CHUNK = 64
FFN_DIM = 2816
FFN_CONV = 3
NORM_EPS = 1e-6
MASK_VALUE = -1e30
FORCED_SCORE = 1e6

LANES = 128
SUBLANES = 8
VMEM_LIMIT = 56 * 1024 * 1024

NSA_TQ = 128
NSA_TK = 512
ROW_TILE = 512
FFN_TF = 1408

SH_M, SC_M, GT_M, SH_F, SC_F, GT_F = range(6)
SMALL_GATE, SMALL_A, SMALL_B = 0, 48, 56


def _params(sem):
    return pltpu.CompilerParams(dimension_semantics=sem, vmem_limit_bytes=VMEM_LIMIT)


def _split3(a):
    hi = a.astype(BF16)
    r1 = a - hi.astype(F32)
    mid = r1.astype(BF16)
    lo = (r1 - mid.astype(F32)).astype(BF16)
    return hi, mid, lo


def _dot3(a, b, dims=None):
    a_hi, a_lo = _split2(a)
    b_hi, b_lo = _split2(b)
    if dims is None:
        dims = (((a.ndim - 1,), (0,)), ((), ()))
    dg = functools.partial(lax.dot_general, dimension_numbers=dims,
                           preferred_element_type=F32)
    return dg(a_hi, b_lo) + dg(a_lo, b_hi) + dg(a_hi, b_hi)


def _split2(a):
    hi = a.astype(BF16)
    lo = (a - hi.astype(F32)).astype(BF16)
    return hi, lo


def _dot_exact_lhs(a_bf16, b, dims=None):
    if dims is None:
        dims = (((a_bf16.ndim - 1,), (0,)), ((), ()))
    out = None
    for part in _split3(b):
        term = lax.dot_general(a_bf16, part, dimension_numbers=dims, preferred_element_type=F32)
        out = term if out is None else out + term
    return out


NT_DIMS = (((1,), (1,)), ((), ()))
TN_DIMS = (((0,), (0,)), ((), ()))


def _mod_kernel(c_ref, w_ref, b_ref, o_ref):
    o_ref[...] = jnp.dot(c_ref[...], w_ref[...], preferred_element_type=F32,
                         precision=HIGHEST) + b_ref[...]


def _adaln_mod(c, w_ada, b_ada):
    bsz, d = c.shape
    n = w_ada.shape[1]
    c8 = jnp.zeros((SUBLANES, d), F32).at[:bsz].set(c)
    out = pl.pallas_call(
        _mod_kernel,
        grid=(n // d,),
        in_specs=[pl.BlockSpec((SUBLANES, d), lambda j: (0, 0)),
                  pl.BlockSpec((d, d), lambda j: (0, j)),
                  pl.BlockSpec((1, d), lambda j: (0, j))],
        out_specs=pl.BlockSpec((SUBLANES, d), lambda j: (0, j)),
        out_shape=jax.ShapeDtypeStruct((SUBLANES, n), F32),
        compiler_params=_params(("arbitrary",)), name="adaln_mod",
    )(c8, w_ada, b_ada.reshape(1, n))
    return out[:bsz].reshape(bsz, n // d, d)


def _modulated_norm(x, norm_w, mod, sc_row, sh_row):
    ms = jnp.mean(x * x, axis=-1, keepdims=True)
    y = x * lax.rsqrt(ms + NORM_EPS) * norm_w
    return y * (1.0 + mod[sc_row:sc_row + 1, :]) + mod[sh_row:sh_row + 1, :]


def _nmm_kernel(x_ref, nw_ref, mod_ref, w_ref, o_ref, h_ref):
    @pl.when(pl.program_id(1) == 0)
    def _():
        h = _modulated_norm(x_ref[...], nw_ref[...], mod_ref[0], SC_M, SH_M)
        h_ref[...] = h.astype(BF16)

    o_ref[...] = jnp.dot(h_ref[...], w_ref[...],
                         preferred_element_type=F32).astype(o_ref.dtype)


def _norm_mod_matmul(x2, norm_w, mod, w, tn, out_dtype, seq):
    t, d = x2.shape
    n = w.shape[1]
    tiles_per_batch = seq // ROW_TILE
    return pl.pallas_call(
        _nmm_kernel,
        grid=(t // ROW_TILE, n // tn),
        in_specs=[pl.BlockSpec((ROW_TILE, d), lambda i, j: (i, 0)),
                  pl.BlockSpec((1, d), lambda i, j: (0, 0)),
                  pl.BlockSpec((1, 6, d), lambda i, j: (i // tiles_per_batch, 0, 0)),
                  pl.BlockSpec((d, tn), lambda i, j: (0, j))],
        out_specs=pl.BlockSpec((ROW_TILE, tn), lambda i, j: (i, j)),
        out_shape=jax.ShapeDtypeStruct((t, n), out_dtype),
        scratch_shapes=[pltpu.VMEM((ROW_TILE, d), BF16)],
        compiler_params=_params(("parallel", "arbitrary")), name="norm_mod_proj",
    )(x2, norm_w.reshape(1, d), mod, w)


def _compress_kernel(t_ref, pos_ref, w1_ref, b1_ref, w2_ref, b2_ref, o_ref):
    half = CMP_STRIDE * NSA_HEAD_DIM
    t = t_ref[0, 0, 0]
    top = jnp.dot(t, w1_ref[0, :half, :], preferred_element_type=F32)
    bot = jnp.dot(t, w1_ref[0, half:, :], preferred_element_type=F32)
    n_rows = t.shape[0]
    bot_next = pltpu.roll(bot, n_rows - 1, 0)
    pos_term = jnp.dot(pos_ref[0], w1_ref[0], preferred_element_type=F32)[0:1]
    hid = jax.nn.gelu(top + bot_next + pos_term + b1_ref[0])
    o_ref[0, 0, 0] = jnp.dot(hid.astype(BF16), w2_ref[0],
                             preferred_element_type=F32) + b2_ref[0]


def _compress(kv_half, pos, w1, b1, w2, b2):
    bsz, _, n_grp, n_half, width = kv_half.shape
    return pl.pallas_call(
        _compress_kernel,
        grid=(2, bsz, n_grp),
        in_specs=[pl.BlockSpec((1, 1, 1, n_half, width), lambda k, b, g: (b, k, g, 0, 0)),
                  pl.BlockSpec((1, 2 * SUBLANES, width * 2), lambda k, b, g: (k, 0, 0)),
                  pl.BlockSpec((1, width * 2, CMP_HIDDEN), lambda k, b, g: (k, 0, 0)),
                  pl.BlockSpec((1, 1, CMP_HIDDEN), lambda k, b, g: (k, 0, 0)),
                  pl.BlockSpec((1, CMP_HIDDEN, NSA_HEAD_DIM), lambda k, b, g: (k, 0, 0)),
                  pl.BlockSpec((1, 1, NSA_HEAD_DIM), lambda k, b, g: (k, 0, 0))],
        out_specs=pl.BlockSpec((1, 1, 1, n_half, NSA_HEAD_DIM), lambda k, b, g: (k, b, g, 0, 0)),
        out_shape=jax.ShapeDtypeStruct((2, bsz, n_grp, n_half, NSA_HEAD_DIM), F32),
        compiler_params=_params(("parallel", "parallel", "parallel")), name="nsa_compress",
    )(kv_half, pos, w1, b1, w2, b2)


def _nsa_kernel(qT_ref, ks_ref, kw_ref, vsT_ref, vwT_ref, kc_ref, vcT_ref, ov_ref,
                gl_ref, o_ref, sel_ref):
    tq, tk = NSA_TQ, NSA_TK
    rows = NSA_REP * tq
    dk = NSA_HEAD_DIM
    qt = pl.program_id(2)
    qs = qt * tq

    q4 = qT_ref[0]
    qT = jnp.concatenate([q4[r * dk:(r + 1) * dk, :] for r in range(NSA_REP)], axis=1)
    zeros = jnp.zeros_like(qT)
    qT_s = jnp.concatenate([qT, zeros], axis=0)
    qT_w = jnp.concatenate([zeros, qT], axis=0)
    t_q = qs + lax.broadcasted_iota(jnp.int32, (1, tq), 1)
    t_row = jnp.concatenate([t_q] * NSA_REP, axis=1)

    n_cmp = kc_ref.shape[2]
    s_c = jnp.dot(kc_ref[0, 0], qT, preferred_element_type=F32)
    c_end = lax.broadcasted_iota(jnp.int32, (n_cmp, 1), 0) * CMP_STRIDE + (CMP_BLK - 1)
    m_c = c_end <= t_row
    s_c = jnp.where(m_c, s_c, MASK_VALUE)
    mx_c = jnp.max(s_c, axis=0, keepdims=True)
    p_c = jnp.where(m_c, jnp.exp(s_c - mx_c), 0.0)
    l_c = jnp.sum(p_c, axis=0, keepdims=True)
    p_c = p_c * jnp.where(l_c > 0.0, 1.0 / l_c, 0.0)
    o_c = jnp.dot(vcT_ref[0, 0], p_c.astype(BF16), preferred_element_type=F32)

    p_sum = p_c[:, 0:tq]
    for r in range(1, NSA_REP):
        p_sum = p_sum + p_c[:, r * tq:(r + 1) * tq]
    ov = ov_ref[...]
    imp = _dot_exact_lhs(ov, p_sum)
    n_blk = ov.shape[0]
    j_blk = lax.broadcasted_iota(jnp.int32, (n_blk, 1), 0)
    cur = t_q // SEL_BLK
    forced = (j_blk == 0) | (j_blk == cur) | (j_blk == cur - 1)
    valid = j_blk * SEL_BLK <= t_q
    score = jnp.where(forced, FORCED_SCORE, jnp.where(valid, imp, -1.0))
    j_f = lax.broadcasted_iota(jnp.int32, (n_blk, tq), 0).astype(F32)
    sel_bias = jnp.full((n_blk, tq), MASK_VALUE, F32)
    for _ in range(N_SEL):
        mx = jnp.max(score, axis=0, keepdims=True)
        first = jnp.min(jnp.where(score == mx, j_f, float(n_blk)), axis=0, keepdims=True)
        pick = j_f == first
        sel_bias = jnp.where(pick & (mx >= 0.0), 0.0, sel_bias)
        score = jnp.where(pick, -jnp.inf, score)
    sel_ref[...] = sel_bias

    blk_per_tile = tk // SEL_BLK
    n_kt = (qs + tq + tk - 1) // tk

    def sel_step(kt, carry):
        m_i, l_i, acc = carry
        s = jnp.dot(ks_ref[0, 0, kt], qT_s, preferred_element_type=F32)
        pieces = []
        for jj in range(blk_per_tile):
            row = sel_ref[pl.ds(kt * blk_per_tile + jj, 1), :]
            pieces.append(jnp.broadcast_to(row, (SEL_BLK, tq)))
        bias = jnp.concatenate(pieces, axis=0)
        kpos = kt * tk + lax.broadcasted_iota(jnp.int32, (tk, 1), 0)
        bias = jnp.where(kpos <= t_q, bias, MASK_VALUE)
        s = s + jnp.concatenate([bias] * NSA_REP, axis=1)
        m_new = jnp.maximum(m_i, jnp.max(s, axis=0, keepdims=True))
        alpha = jnp.exp(m_i - m_new)
        p = jnp.exp(s - m_new)
        l_new = alpha * l_i + jnp.sum(p, axis=0, keepdims=True)
        pv = jnp.dot(vsT_ref[0, 0, kt], p.astype(BF16), preferred_element_type=F32)
        return m_new, l_new, alpha * acc + pv

    init = (jnp.full((1, rows), MASK_VALUE, F32), jnp.zeros((1, rows), F32),
            jnp.zeros((dk, rows), F32))
    _, l_s, acc_s = lax.fori_loop(0, n_kt, sel_step, init)
    o_s = acc_s * (1.0 / l_s)

    n_wt = WINDOW // tq + 1
    s_parts, ti_list = [], []
    for i in range(n_wt):
        ti = jnp.maximum(qt - (n_wt - 1) + i, 0)
        ti_list.append(ti)
        s_parts.append(jnp.dot(kw_ref[0, 0, ti], qT_w, preferred_element_type=F32))
    s_w = jnp.concatenate(s_parts, axis=0)
    wpos = qs - WINDOW + lax.broadcasted_iota(jnp.int32, (n_wt * tq, 1), 0)
    m_w = (wpos <= t_row) & (wpos > t_row - WINDOW) & (wpos >= 0)
    s_w = jnp.where(m_w, s_w, MASK_VALUE)
    mx_w = jnp.max(s_w, axis=0, keepdims=True)
    p_w = jnp.where(m_w, jnp.exp(s_w - mx_w), 0.0)
    l_w = jnp.sum(p_w, axis=0, keepdims=True)
    p_wb = p_w.astype(BF16)
    o_w = None
    for i in range(n_wt):
        term = jnp.dot(vwT_ref[0, 0, ti_list[i]], p_wb[i * tq:(i + 1) * tq, :],
                       preferred_element_type=F32)
        o_w = term if o_w is None else o_w + term
    o_w = o_w * (1.0 / l_w)

    gates = jax.nn.sigmoid(gl_ref[0, 0])
    outs = []
    for r in range(NSA_REP):
        sl = slice(r * tq, (r + 1) * tq)
        outs.append(gates[3 * r:3 * r + 1] * o_c[:, sl]
                    + gates[3 * r + 1:3 * r + 2] * o_s[:, sl]
                    + gates[3 * r + 2:3 * r + 3] * o_w[:, sl])
    o_ref[0] = jnp.concatenate(outs, axis=0).astype(o_ref.dtype)


def _nsa_attention(qT, ksw, vsT, vwT, kc, vcT, glT, overlapT):
    bsz, hd, seq = qT.shape
    n_grp = NSA_KV_HEADS
    tq, tk = NSA_TQ, NSA_TK
    ks5 = ksw.reshape(bsz, n_grp, seq // tk, tk, LANES)
    kw5 = ksw.reshape(bsz, n_grp, seq // tq, tq, LANES)
    n_cmp = kc.shape[2]
    n_blk = seq // SEL_BLK
    gdim = glT.shape[2]
    return pl.pallas_call(
        _nsa_kernel,
        grid=(bsz, n_grp, seq // tq),
        in_specs=[
            pl.BlockSpec((1, NSA_REP * NSA_HEAD_DIM, tq), lambda b, g, t: (b, g, t)),
            pl.BlockSpec((1, 1, seq // tk, tk, LANES), lambda b, g, t: (b, g, 0, 0, 0)),
            pl.BlockSpec((1, 1, seq // tq, tq, LANES), lambda b, g, t: (b, g, 0, 0, 0)),
            pl.BlockSpec((1, 1, seq // tk, NSA_HEAD_DIM, tk), lambda b, g, t: (b, g, 0, 0, 0)),
            pl.BlockSpec((1, 1, seq // tq, NSA_HEAD_DIM, tq), lambda b, g, t: (b, g, 0, 0, 0)),
            pl.BlockSpec((1, 1, n_cmp, NSA_HEAD_DIM), lambda b, g, t: (b, g, 0, 0)),
            pl.BlockSpec((1, 1, NSA_HEAD_DIM, n_cmp), lambda b, g, t: (b, g, 0, 0)),
            pl.BlockSpec((n_blk, n_cmp), lambda b, g, t: (0, 0)),
            pl.BlockSpec((1, 1, gdim, tq), lambda b, g, t: (b, g, 0, t)),
        ],
        out_specs=pl.BlockSpec((1, NSA_REP * NSA_HEAD_DIM, tq), lambda b, g, t: (b, g, t)),
        out_shape=jax.ShapeDtypeStruct((bsz, hd, seq), BF16),
        scratch_shapes=[pltpu.VMEM((n_blk, tq), F32)],
        compiler_params=_params(("parallel", "parallel", "arbitrary")), name="nsa_attention",
    )(qT, ks5, kw5, vsT, vwT, kc, vcT, overlapT, glT)


def _gdn_kernel(x_ref, halo_ref, z_ref, small_ref, cw_ref, alog_ref, dtb_ref, nw_ref,
                o_ref, state_ref, xc_ref):
    cs = GDN_CHUNK
    hd = GDN_HEAD_DIM
    c = pl.program_id(1)

    @pl.when(c == 0)
    def _():
        state_ref[...] = jnp.zeros_like(state_ref)

    halo = jnp.where(c > 0, halo_ref[...], 0.0)
    xc_ref[0:SUBLANES, :] = halo
    xc_ref[SUBLANES:, :] = x_ref[...]
    cw = cw_ref[...]
    y = cw[GDN_CONV - 1:GDN_CONV, :] * x_ref[...]
    for i in range(GDN_CONV - 1):
        off = SUBLANES - (GDN_CONV - 1) + i
        y = y + cw[i:i + 1, :] * xc_ref[off:off + cs, :]
    y = y * jax.nn.sigmoid(y)

    small = small_ref[...]
    sp_in = small + dtb_ref[...]
    softplus = jnp.maximum(sp_in, 0.0) + jnp.log1p(jnp.exp(-jnp.abs(sp_in)))
    g_all = -jnp.exp(alog_ref[...]) * softplus
    beta_all = jax.nn.sigmoid(small)
    ii = lax.broadcasted_iota(jnp.int32, (cs, cs), 0)
    jj = lax.broadcasted_iota(jnp.int32, (cs, cs), 1)
    incl = ii >= jj
    strict = ii > jj
    tril = jnp.where(incl, 1.0, 0.0).astype(BF16)
    eye = jnp.where(ii == jj, 1.0, 0.0).astype(F32)
    gc_all = _dot_exact_lhs(tril, g_all)
    e128 = jnp.where(lax.broadcasted_iota(jnp.int32, (LANES, LANES), 0)
                     == lax.broadcasted_iota(jnp.int32, (LANES, LANES), 1), 1.0, 0.0).astype(BF16)
    gcT_all = _dot_exact_lhs(e128, gc_all, NT_DIMS)

    nw = nw_ref[...]
    z_all = z_ref[...]
    n_qk = GDN_HEADS * hd
    outs = []
    for h in range(GDN_HEADS):
        q = y[:, h * hd:(h + 1) * hd]
        k = y[:, n_qk + h * hd:n_qk + (h + 1) * hd]
        v = y[:, 2 * n_qk + h * hd:2 * n_qk + (h + 1) * hd]
        q = q * lax.rsqrt(jnp.sum(q * q, axis=-1, keepdims=True) + NORM_EPS) * (hd ** -0.5)
        k = k * lax.rsqrt(jnp.sum(k * k, axis=-1, keepdims=True) + NORM_EPS)
        gc = gc_all[:, SMALL_A + h:SMALL_A + h + 1]
        gc_row = gcT_all[SMALL_A + h:SMALL_A + h + 1, :]
        beta = beta_all[:, SMALL_B + h:SMALL_B + h + 1]
        decay = jnp.exp(jnp.where(incl, gc - gc_row, MASK_VALUE))
        kk = _dot3(k, k, NT_DIMS)
        neg_l = jnp.where(strict, -(beta * kk * decay), 0.0)
        inv = eye + neg_l
        pw = neg_l
        for _ in range(5):
            pw = _dot3(pw, pw)
            inv = inv + _dot3(inv, pw)
        e_gc = jnp.exp(gc)
        rhs = jnp.concatenate([v * beta, k * (beta * e_gc)], axis=1)
        sol = _dot3(inv, rhs)
        u, w = sol[:, :hd], sol[:, hd:]
        qk = _dot3(q, k, NT_DIMS) * decay
        q_dec = q * e_gc
        gc_last = gc[cs - 1:cs, :]
        k_dec = k * jnp.exp(gc_last - gc)
        state = state_ref[h]
        v_new = u - _dot3(w, state)
        o = _dot3(q_dec, state) + _dot3(qk, v_new)
        state_ref[h] = state * jnp.exp(gc_last) + _dot3(k_dec, v_new, TN_DIMS)
        o = o * lax.rsqrt(jnp.mean(o * o, axis=-1, keepdims=True) + NORM_EPS) * nw
        z = z_all[:, h * hd:(h + 1) * hd]
        outs.append(o * (z * jax.nn.sigmoid(z)))
    o_ref[...] = jnp.concatenate(outs, axis=1).astype(o_ref.dtype)


def _gated_deltanet(pc, small, conv_w, alog_pad, dtb_pad, norm_w, bsz, seq):
    cs = GDN_CHUNK
    n_chunk = seq // cs
    qkv_cols = 3 * GDN_HEADS * GDN_HEAD_DIM
    z_cols = GDN_HEADS * GDN_HEAD_DIM
    halo_per_chunk = cs // SUBLANES
    return pl.pallas_call(
        _gdn_kernel,
        grid=(bsz, n_chunk),
        in_specs=[
            pl.BlockSpec((cs, qkv_cols), lambda b, c: (b * n_chunk + c, 0)),
            pl.BlockSpec((SUBLANES, qkv_cols),
                         lambda b, c: (jnp.maximum((b * n_chunk + c) * halo_per_chunk - 1, 0), 0)),
            pl.BlockSpec((cs, z_cols), lambda b, c: (b * n_chunk + c, qkv_cols // z_cols)),
            pl.BlockSpec((cs, LANES), lambda b, c: (b * n_chunk + c, 0)),
            pl.BlockSpec((GDN_CONV, qkv_cols), lambda b, c: (0, 0)),
            pl.BlockSpec((1, LANES), lambda b, c: (0, 0)),
            pl.BlockSpec((1, LANES), lambda b, c: (0, 0)),
            pl.BlockSpec((1, GDN_HEAD_DIM), lambda b, c: (0, 0)),
        ],
        out_specs=pl.BlockSpec((cs, z_cols), lambda b, c: (b * n_chunk + c, 0)),
        out_shape=jax.ShapeDtypeStruct((bsz * seq, z_cols), BF16),
        scratch_shapes=[pltpu.VMEM((GDN_HEADS, GDN_HEAD_DIM, GDN_HEAD_DIM), F32),
                        pltpu.VMEM((cs + SUBLANES, qkv_cols), F32)],
        compiler_params=_params(("parallel", "arbitrary")), name="gated_deltanet",
    )(pc, pc, pc, small, conv_w, alog_pad, dtb_pad, norm_w.reshape(1, GDN_HEAD_DIM))


def _merge_kernel(ya_ref, yb_ref, ma_ref, mb_ref, x_ref, mod_ref, wa_ref, wb_ref, wo_ref,
                  nf_ref, x1_ref, h2_ref):
    pa = jnp.dot(ya_ref[...], wa_ref[...], preferred_element_type=F32)
    pb = jnp.dot(yb_ref[...], wb_ref[...], preferred_element_type=F32)
    mixed = jax.nn.sigmoid(ma_ref[...]) * pa + jax.nn.sigmoid(mb_ref[...]) * pb
    mod = mod_ref[0]
    x1 = x_ref[...] + mod[GT_M:GT_M + 1, :] * jnp.dot(
        mixed.astype(BF16), wo_ref[...], preferred_element_type=F32)
    x1_ref[...] = x1
    h2_ref[...] = _modulated_norm(x1, nf_ref[...], mod, SC_F, SH_F).astype(BF16)


def _merge_project(ya, yb, pc, x2, mod, w_pa, w_pb, w_o, norm_ffn, seq):
    t, d = x2.shape
    tiles_per_batch = seq // ROW_TILE
    merge_blk = (3 * GDN_HEADS * GDN_HEAD_DIM + GDN_HEADS * GDN_HEAD_DIM) // d
    row = lambda i: (i, 0)
    const = lambda i: (0, 0)
    return pl.pallas_call(
        _merge_kernel,
        grid=(t // ROW_TILE,),
        in_specs=[pl.BlockSpec((ROW_TILE, d), row),
                  pl.BlockSpec((ROW_TILE, d), row),
                  pl.BlockSpec((ROW_TILE, d), lambda i: (i, merge_blk)),
                  pl.BlockSpec((ROW_TILE, d), lambda i: (i, merge_blk + 1)),
                  pl.BlockSpec((ROW_TILE, d), row),
                  pl.BlockSpec((1, 6, d), lambda i: (i // tiles_per_batch, 0, 0)),
                  pl.BlockSpec((d, d), const),
                  pl.BlockSpec((d, d), const),
                  pl.BlockSpec((d, d), const),
                  pl.BlockSpec((1, d), const)],
        out_specs=[pl.BlockSpec((ROW_TILE, d), row), pl.BlockSpec((ROW_TILE, d), row)],
        out_shape=[jax.ShapeDtypeStruct((t, d), F32), jax.ShapeDtypeStruct((t, d), BF16)],
        compiler_params=_params(("parallel",)), name="merge_project",
    )(ya, yb, pc, pc, x2, mod, w_pa, w_pb, w_o, norm_ffn.reshape(1, d))


def _ffn_kernel(h_ref, halo_ref, x1_ref, mod_ref, wg_ref, wv_ref, cwg_ref, cwv_ref,
                cbg_ref, cbv_ref, wd_ref, nfin_ref, o_ref, acc_ref, ug_ref, uv_ref,
                *, tiles_per_batch):
    i = pl.program_id(0)
    f = pl.program_id(1)
    tm = h_ref.shape[0]
    first_of_batch = (i % tiles_per_batch) == 0

    @pl.when(f == 0)
    def _():
        acc_ref[...] = jnp.zeros_like(acc_ref)

    h = h_ref[...]
    halo = halo_ref[...]

    def conv_half(w_ref, cw_ref, cb_ref, u_ref):
        u_ref[SUBLANES:, :] = jnp.dot(h, w_ref[...], preferred_element_type=F32)
        up_halo = jnp.dot(halo, w_ref[...], preferred_element_type=F32)[SUBLANES:]
        u_ref[0:SUBLANES, :] = jnp.where(first_of_batch, 0.0, up_halo)
        cw = cw_ref[...]
        out = cb_ref[...] + cw[FFN_CONV - 1:FFN_CONV, :] * u_ref[SUBLANES:, :]
        for j in range(FFN_CONV - 1):
            off = SUBLANES - (FFN_CONV - 1) + j
            out = out + cw[j:j + 1, :] * u_ref[off:off + tm, :]
        return out

    gate = conv_half(wg_ref, cwg_ref, cbg_ref, ug_ref)
    val = conv_half(wv_ref, cwv_ref, cbv_ref, uv_ref)
    act = gate * jax.nn.sigmoid(gate) * val
    acc_ref[...] += jnp.dot(act.astype(BF16), wd_ref[...], preferred_element_type=F32)

    @pl.when(f == pl.num_programs(1) - 1)
    def _():
        mod = mod_ref[0]
        x2 = x1_ref[...] + mod[GT_F:GT_F + 1, :] * acc_ref[...]
        ms = jnp.mean(x2 * x2, axis=-1, keepdims=True)
        o_ref[...] = x2 * lax.rsqrt(ms + NORM_EPS) * nfin_ref[...]


def _conv_ffn(h2, x1, mod, w_up, conv_w, conv_b, w_down, norm_final, seq):
    t, d = x1.shape
    tm, tf = ROW_TILE, FFN_TF
    n_f = FFN_DIM // tf
    tiles_per_batch = seq // tm
    halo_per_tile = tm // SUBLANES
    row = lambda i, f: (i, 0)
    return pl.pallas_call(
        functools.partial(_ffn_kernel, tiles_per_batch=tiles_per_batch),
        grid=(t // tm, n_f),
        in_specs=[pl.BlockSpec((tm, d), row),
                  pl.BlockSpec((SUBLANES * 2, d),
                               lambda i, f: (jnp.maximum(i * (halo_per_tile // 2) - 1, 0), 0)),
                  pl.BlockSpec((tm, d), row),
                  pl.BlockSpec((1, 6, d), lambda i, f: (i // tiles_per_batch, 0, 0)),
                  pl.BlockSpec((d, tf), lambda i, f: (0, f)),
                  pl.BlockSpec((d, tf), lambda i, f: (0, n_f + f)),
                  pl.BlockSpec((FFN_CONV, tf), lambda i, f: (0, f)),
                  pl.BlockSpec((FFN_CONV, tf), lambda i, f: (0, n_f + f)),
                  pl.BlockSpec((1, tf), lambda i, f: (0, f)),
                  pl.BlockSpec((1, tf), lambda i, f: (0, n_f + f)),
                  pl.BlockSpec((tf, d), lambda i, f: (f, 0)),
                  pl.BlockSpec((1, d), lambda i, f: (0, 0))],
        out_specs=pl.BlockSpec((tm, d), row),
        out_shape=jax.ShapeDtypeStruct((t, d), F32),
        scratch_shapes=[pltpu.VMEM((tm, d), F32),
                        pltpu.VMEM((tm + SUBLANES, tf), F32),
                        pltpu.VMEM((tm + SUBLANES, tf), F32)],
        compiler_params=_params(("parallel", "arbitrary")), name="conv_ffn",
    )(h2, h2, x1, mod, w_up, w_up, conv_w, conv_w, conv_b, conv_b, w_down,
      norm_final.reshape(1, d))


def _overlap_t(seq):
    n_cmp = (seq - CMP_BLK) // CMP_STRIDE + 1
    n_blk = seq // SEL_BLK
    c_start = np.arange(n_cmp) * CMP_STRIDE
    s_start = np.arange(n_blk) * SEL_BLK
    ov = np.clip(np.minimum(c_start[:, None] + CMP_BLK, s_start[None, :] + SEL_BLK)
                 - np.maximum(c_start[:, None], s_start[None, :]), 0, None) / CMP_BLK
    n_half = seq // CMP_STRIDE
    out = np.zeros((n_blk, n_half), np.float32)
    out[:, :n_cmp] = ov.T
    return jnp.asarray(out, BF16)


def kernel(x, c, w_ada, b_ada, norm_mix, w_in, nsa_pos_k, nsa_pos_v, nsa_ck_w1, nsa_ck_b1, nsa_ck_w2, nsa_ck_b2, nsa_cv_w1, nsa_cv_b1, nsa_cv_w2, nsa_cv_b2, gdn_conv, gdn_a_log, gdn_dt_bias, gdn_norm, w_proj_nsa, w_proj_gdn, w_out, norm_ffn, ffn_up, ffn_conv, ffn_conv_b, ffn_down, norm_final):
    bsz, seq, d = x.shape
    depth = w_ada.shape[0]
    n_grp, dk = NSA_KV_HEADS, NSA_HEAD_DIM
    q_cols = NSA_HEADS * dk
    kv_cols = 6 * n_grp * dk
    gate_cols = 3 * NSA_HEADS
    gqkv_cols = 3 * GDN_HEADS * GDN_HEAD_DIM
    z_cols = GDN_HEADS * GDN_HEAD_DIM
    o_kv = q_cols
    o_gl = o_kv + kv_cols
    o_gq = o_gl + gate_cols
    o_a = o_gq + gqkv_cols
    o_b = o_a + GDN_HEADS
    o_z = o_b + GDN_HEADS
    o_mg = o_z + z_cols
    overlap_t = _overlap_t(seq)
    x2 = x.reshape(bsz * seq, d)
    assert depth == 1, "kernel supports the problem's DEPTH == 1"
    for l in range(depth):
        mod = _adaln_mod(c, w_ada[l], b_ada[l])
        w = w_in[l]
        w_att = jnp.concatenate([w[:, :o_kv] * (dk ** -0.5), w[:, o_kv:o_gl]], axis=1).astype(BF16)
        w_small = jnp.zeros((d, LANES), F32)
        w_small = w_small.at[:, SMALL_GATE:SMALL_GATE + gate_cols].set(w[:, o_gl:o_gq])
        w_small = w_small.at[:, SMALL_A:SMALL_A + 2 * GDN_HEADS].set(w[:, o_a:o_z]).astype(BF16)
        w_rest = jnp.concatenate([w[:, o_gq:o_a], w[:, o_z:]], axis=1).astype(BF16)

        p_att = _norm_mod_matmul(x2, norm_mix[l], mod, w_att, 1280, BF16, seq)
        p_small = _norm_mod_matmul(x2, norm_mix[l], mod, w_small, LANES, F32, seq)
        pc = _norm_mod_matmul(x2, norm_mix[l], mod, w_rest, 2048, F32, seq)

        qT = p_att[:, :q_cols].reshape(bsz, seq, q_cols).transpose(0, 2, 1)
        kv = p_att[:, q_cols:].reshape(bsz, seq, 6, n_grp, dk).transpose(0, 2, 3, 1, 4)
        kv_half = kv.reshape(bsz, 6, n_grp, seq // CMP_STRIDE, CMP_STRIDE * dk)
        ksw = jnp.concatenate([kv[:, 2], kv[:, 4]], axis=-1)
        tq, tk = NSA_TQ, NSA_TK
        vsT = kv[:, 3].reshape(bsz, n_grp, seq // tk, tk, dk).transpose(0, 1, 2, 4, 3)
        vwT = kv[:, 5].reshape(bsz, n_grp, seq // tq, tq, dk).transpose(0, 1, 2, 4, 3)
        glT = p_small[:, SMALL_GATE:SMALL_GATE + gate_cols].reshape(bsz, seq, n_grp, 3 * NSA_REP)
        glT = jnp.pad(glT.transpose(0, 2, 3, 1), ((0, 0), (0, 0), (0, 4), (0, 0)))

        pos = jnp.stack([nsa_pos_k[l], nsa_pos_v[l]]).reshape(2, 1, CMP_BLK * dk)
        pos = jnp.broadcast_to(pos, (2, 2 * SUBLANES, CMP_BLK * dk)).astype(BF16)
        cw1 = jnp.stack([nsa_ck_w1[l], nsa_cv_w1[l]]).astype(BF16)
        cb1 = jnp.stack([nsa_ck_b1[l], nsa_cv_b1[l]]).reshape(2, 1, CMP_HIDDEN)
        cw2 = jnp.stack([nsa_ck_w2[l], nsa_cv_w2[l]]).astype(BF16)
        cb2 = jnp.stack([nsa_ck_b2[l], nsa_cv_b2[l]]).reshape(2, 1, dk)
        cmp = _compress(kv_half, pos, cw1, cb1, cw2, cb2)
        kc = cmp[0].astype(BF16)
        vcT = cmp[1].astype(BF16).transpose(0, 1, 3, 2)

        yT = _nsa_attention(qT, ksw, vsT, vwT, kc, vcT, glT, overlap_t)
        y_a = yT.transpose(0, 2, 1).reshape(bsz * seq, q_cols)

        alog_pad = jnp.zeros((1, LANES), F32).at[0, SMALL_A:SMALL_A + GDN_HEADS].set(gdn_a_log[l])
        dtb_pad = jnp.zeros((1, LANES), F32).at[0, SMALL_A:SMALL_A + GDN_HEADS].set(gdn_dt_bias[l])
        y_b = _gated_deltanet(pc, p_small, gdn_conv[l], alog_pad, dtb_pad, gdn_norm[l], bsz, seq)

        x1, h2 = _merge_project(y_a, y_b, pc, x2, mod, w_proj_nsa[l].astype(BF16),
                                w_proj_gdn[l].astype(BF16), w_out[l].astype(BF16),
                                norm_ffn[l], seq)
        out = _conv_ffn(h2, x1, mod, ffn_up[l].astype(BF16), ffn_conv[l],
                        ffn_conv_b[l].reshape(1, 2 * FFN_DIM), ffn_down[l].astype(BF16),
                        norm_final, seq)
    return out.reshape(bsz, seq, d)
```

```python
import functools

import numpy as np
import jax
import jax.numpy as jnp
from jax import lax
from jax.experimental import pallas as pl
from jax.experimental.pallas import tpu as pltpu

F32 = jnp.float32
BF16 = jnp.bfloat16
HIGHEST = lax.Precision.HIGHEST

D_MODEL = 1024
NSA_HEAD_DIM = 64
NSA_HEADS = 16
NSA_KV_HEADS = 4
NSA_REP = NSA_HEADS // NSA_KV_HEADS
CMP_BLK = 32
CMP_STRIDE = 16
CMP_HIDDEN = 256
SEL_BLK = 64
N_SEL = 16
WINDOW = 512
GDN_HEAD_DIM = 128
GDN_HEADS = 8
GDN_CONV = 4
GDN_CHUNK = 64
FFN_DIM = 2816
FFN_CONV = 3
NORM_EPS = 1e-6
MASK_VALUE = -1e30

LANES = 128
SUBLANES = 8
VMEM_LIMIT = 56 * 1024 * 1024

NSA_TQ = 128
NSA_TK = 512
NSA_AUX = 16
LOG2_E = 1.4426950408889634
ROW_TILE = 512
FFN_TF = 1408

SH_M, SC_M, GT_M, SH_F, SC_F, GT_F = range(6)
SMALL_GATE, SMALL_A, SMALL_B = 0, 48, 56


def _params(sem):
    return pltpu.CompilerParams(dimension_semantics=sem, vmem_limit_bytes=VMEM_LIMIT)


def _split3(a):
    hi = a.astype(BF16)
    r1 = a - hi.astype(F32)
    mid = r1.astype(BF16)
    lo = (r1 - mid.astype(F32)).astype(BF16)
    return hi, mid, lo


def _dot_exact_lhs(a_bf16, b, dims=None):
    if dims is None:
        dims = (((a_bf16.ndim - 1,), (0,)), ((), ()))
    out = None
    for part in _split3(b):
        term = lax.dot_general(a_bf16, part, dimension_numbers=dims, preferred_element_type=F32)
        out = term if out is None else out + term
    return out


NT_DIMS = (((1,), (1,)), ((), ()))
TN_DIMS = (((0,), (0,)), ((), ()))
NN_DIMS = (((1,), (0,)), ((), ()))


def _mod_kernel(c_ref, w_ref, b_ref, o_ref):
    o_ref[...] = jnp.dot(c_ref[...], w_ref[...], preferred_element_type=F32,
                         precision=HIGHEST) + b_ref[...]


def _adaln_mod(c, w_ada, b_ada):
    bsz, d = c.shape
    n = w_ada.shape[1]
    c8 = jnp.zeros((SUBLANES, d), F32).at[:bsz].set(c)
    out = pl.pallas_call(
        _mod_kernel,
        grid=(n // d,),
        in_specs=[pl.BlockSpec((SUBLANES, d), lambda j: (0, 0)),
                  pl.BlockSpec((d, d), lambda j: (0, j)),
                  pl.BlockSpec((1, d), lambda j: (0, j))],
        out_specs=pl.BlockSpec((SUBLANES, d), lambda j: (0, j)),
        out_shape=jax.ShapeDtypeStruct((SUBLANES, n), F32),
        compiler_params=_params(("arbitrary",)), name="adaln_mod",
    )(c8, w_ada, b_ada.reshape(1, n))
    return out[:bsz].reshape(bsz, n // d, d)


def _modulated_norm(x, norm_w, mod, sc_row, sh_row):
    ms = jnp.mean(x * x, axis=-1, keepdims=True)
    y = x * lax.rsqrt(ms + NORM_EPS) * norm_w
    return y * (1.0 + mod[sc_row:sc_row + 1, :]) + mod[sh_row:sh_row + 1, :]


def _nmm_kernel(x_ref, nw_ref, mod_ref, w_ref, o_ref, h_ref):
    @pl.when(pl.program_id(1) == 0)
    def _():
        h = _modulated_norm(x_ref[...], nw_ref[...], mod_ref[0], SC_M, SH_M)
        h_ref[...] = h.astype(BF16)

    o_ref[...] = jnp.dot(h_ref[...], w_ref[...],
                         preferred_element_type=F32).astype(o_ref.dtype)


def _norm_mod_matmul(x2, norm_w, mod, w, tn, out_dtype, seq):
    t, d = x2.shape
    n = w.shape[1]
    tiles_per_batch = seq // ROW_TILE
    return pl.pallas_call(
        _nmm_kernel,
        grid=(t // ROW_TILE, n // tn),
        in_specs=[pl.BlockSpec((ROW_TILE, d), lambda i, j: (i, 0)),
                  pl.BlockSpec((1, d), lambda i, j: (0, 0)),
                  pl.BlockSpec((1, 6, d), lambda i, j: (i // tiles_per_batch, 0, 0)),
                  pl.BlockSpec((d, tn), lambda i, j: (0, j))],
        out_specs=pl.BlockSpec((ROW_TILE, tn), lambda i, j: (i, j)),
        out_shape=jax.ShapeDtypeStruct((t, n), out_dtype),
        scratch_shapes=[pltpu.VMEM((ROW_TILE, d), BF16)],
        compiler_params=_params(("parallel", "arbitrary")), name="norm_mod_proj",
    )(x2, norm_w.reshape(1, d), mod, w)


def _compress_kernel(t_ref, pos_ref, w1_ref, b1_ref, w2_ref, b2_ref, o_ref):
    half = CMP_STRIDE * NSA_HEAD_DIM
    t = t_ref[0, 0, 0]
    top = jnp.dot(t, w1_ref[0, :half, :], preferred_element_type=F32)
    bot = jnp.dot(t, w1_ref[0, half:, :], preferred_element_type=F32)
    n_rows = t.shape[0]
    bot_next = pltpu.roll(bot, n_rows - 1, 0)
    pos_term = jnp.dot(pos_ref[0], w1_ref[0], preferred_element_type=F32)[0:1]
    hid = jax.nn.gelu(top + bot_next + pos_term + b1_ref[0])
    o_ref[0, 0, 0] = jnp.dot(hid.astype(BF16), w2_ref[0],
                             preferred_element_type=F32) + b2_ref[0]


def _compress(kv_half, pos, w1, b1, w2, b2):
    bsz, _, n_grp, n_half, width = kv_half.shape
    return pl.pallas_call(
        _compress_kernel,
        grid=(2, bsz, n_grp),
        in_specs=[pl.BlockSpec((1, 1, 1, n_half, width), lambda k, b, g: (b, k, g, 0, 0)),
                  pl.BlockSpec((1, 2 * SUBLANES, width * 2), lambda k, b, g: (k, 0, 0)),
                  pl.BlockSpec((1, width * 2, CMP_HIDDEN), lambda k, b, g: (k, 0, 0)),
                  pl.BlockSpec((1, 1, CMP_HIDDEN), lambda k, b, g: (k, 0, 0)),
                  pl.BlockSpec((1, CMP_HIDDEN, NSA_HEAD_DIM), lambda k, b, g: (k, 0, 0)),
                  pl.BlockSpec((1, 1, NSA_HEAD_DIM), lambda k, b, g: (k, 0, 0))],
        out_specs=pl.BlockSpec((1, 1, 1, n_half, NSA_HEAD_DIM), lambda k, b, g: (k, b, g, 0, 0)),
        out_shape=jax.ShapeDtypeStruct((2, bsz, n_grp, n_half, NSA_HEAD_DIM), F32),
        compiler_params=_params(("parallel", "parallel", "parallel")), name="nsa_compress",
    )(kv_half, pos, w1, b1, w2, b2)


def _nsa_kernel(qT_ref, ks_ref, kw_ref, vsT_ref, vwT_ref, kc_ref, vcT_ref, ov_ref,
                gl_ref, o_ref, sel_ref, s_ref, m_ref, acc_ref, part_ref):
    tq, tk = NSA_TQ, NSA_TK
    rows = NSA_REP * tq
    dk = NSA_HEAD_DIM
    qt = pl.program_id(2)
    qs = qt * tq

    q4 = qT_ref[0]
    qT = jnp.concatenate([q4[r * dk:(r + 1) * dk, :] for r in range(NSA_REP)], axis=1)
    t_q = qs + lax.broadcasted_iota(jnp.int32, (1, tq), 1)
    t_row = jnp.concatenate([t_q] * NSA_REP, axis=1)

    def per_head(a):
        return jnp.concatenate([a] * NSA_REP, axis=1)

    n_cmp = kc_ref.shape[2]
    s_c = jnp.dot(kc_ref[0, 0], qT, preferred_element_type=F32)
    c_end = lax.broadcasted_iota(jnp.int32, (n_cmp, 1), 0) * CMP_STRIDE + (CMP_BLK - 1)
    s_c = s_c + per_head(jnp.where(c_end <= t_q, 0.0, MASK_VALUE))
    mx_c = jnp.max(s_c, axis=0, keepdims=True)
    p_c = jnp.exp2(s_c - mx_c)
    l_c = jnp.sum(p_c, axis=0, keepdims=True)
    p_c = p_c * jnp.where(t_row >= CMP_BLK - 1, 1.0 / l_c, 0.0)
    o_c = jnp.dot(vcT_ref[0, 0], p_c.astype(BF16), preferred_element_type=F32)

    p_sum = p_c[:, 0:tq]
    for r in range(1, NSA_REP):
        p_sum = p_sum + p_c[:, r * tq:(r + 1) * tq]
    ov = ov_ref[...]
    imp = _dot_exact_lhs(ov, p_sum)
    n_blk = ov.shape[0]
    j_blk = lax.broadcasted_iota(jnp.int32, (n_blk, 1), 0)
    cur = t_q // SEL_BLK
    forced = (j_blk == 0) | (j_blk == cur) | (j_blk == cur - 1)
    valid = j_blk * SEL_BLK <= t_q
    score = jnp.where(forced, -jnp.inf, jnp.where(valid, imp, -1.0))
    j_f = lax.broadcasted_iota(jnp.int32, (n_blk, tq), 0).astype(F32)
    sel_bias = jnp.where(forced, 0.0, jnp.full((n_blk, tq), MASK_VALUE, F32))
    for _ in range(N_SEL - 3):
        mx = jnp.max(score, axis=0, keepdims=True)
        first = jnp.min(jnp.where(score == mx, j_f, float(n_blk)), axis=0, keepdims=True)
        pick = j_f == first
        sel_bias = jnp.where(pick & (mx >= 0.0), 0.0, sel_bias)
        score = jnp.where(pick, -jnp.inf, score)
    sel_ref[0:n_blk, :] = sel_bias
    sel_ref[n_blk:, :] = jnp.zeros((NSA_AUX, tq), F32)

    blk_per_tile = tk // SEL_BLK
    n_full = qs // tk
    zpad = jnp.zeros((LANES - dk - NSA_AUX, rows), BF16)

    def scores(kt, slot, diagonal=False):
        sb = sel_ref[pl.ds(pl.multiple_of(kt * blk_per_tile, blk_per_tile), NSA_AUX), :]
        rhs = jnp.concatenate([qT, per_head(sb).astype(BF16), zpad], axis=0)
        s = jnp.dot(ks_ref[0, 0, kt], rhs, preferred_element_type=F32)
        if diagonal:
            kpos = kt * tk + lax.broadcasted_iota(jnp.int32, (tk, 1), 0)
            s = s + per_head(jnp.where(kpos <= t_q, 0.0, MASK_VALUE))
        s_ref[slot] = s

    def softmax_pv(kt, slot):
        m_i = m_ref[...]
        m_new = jnp.maximum(m_i, jnp.max(s_ref[slot], axis=0, keepdims=True))
        alpha = jnp.exp2(m_i - m_new)
        p = jnp.exp2(s_ref[slot] - m_new).astype(BF16)
        pv = jnp.dot(vsT_ref[0, 0, kt], p, preferred_element_type=F32)
        acc_ref[...] = alpha * acc_ref[...] + pv
        m_ref[...] = m_new

    m_ref[...] = jnp.full((1, rows), MASK_VALUE, F32)
    acc_ref[...] = jnp.zeros((dk + NSA_AUX, rows), F32)
    scores(n_full, 0, diagonal=True)

    n_wt = WINDOW // tq + 1
    flag = jnp.where(lax.broadcasted_iota(jnp.int32, (NSA_AUX, rows), 0) == 0, MASK_VALUE, 0.0)
    rhs_w = jnp.concatenate([qT, flag.astype(BF16), zpad], axis=0)
    a_k = lax.broadcasted_iota(jnp.int32, (tq, 1), 0)
    b_q = lax.broadcasted_iota(jnp.int32, (1, tq), 1)
    s_parts = [jnp.dot(kw_ref[0, 0, qt + i], rhs_w, preferred_element_type=F32)
               for i in range(n_wt)]
    s_parts[0] = s_parts[0] + per_head(jnp.where(a_k > b_q, 0.0, MASK_VALUE))
    s_parts[-1] = s_parts[-1] + per_head(jnp.where(a_k <= b_q, 0.0, MASK_VALUE))
    s_w = jnp.concatenate(s_parts, axis=0)
    mx_w = jnp.max(s_w, axis=0, keepdims=True)
    p_w = jnp.exp2(s_w - mx_w).astype(BF16)
    acc_w = None
    for i in range(n_wt):
        term = jnp.dot(vwT_ref[0, 0, qt + i], p_w[i * tq:(i + 1) * tq, :],
                       preferred_element_type=F32)
        acc_w = term if acc_w is None else acc_w + term
    o_w = acc_w[:dk] * (1.0 / acc_w[dk:dk + 1])

    gates = jax.nn.sigmoid(gl_ref[0, 0])
    for r in range(NSA_REP):
        sl = slice(r * tq, (r + 1) * tq)
        part_ref[r * dk:(r + 1) * dk, :] = (gates[3 * r:3 * r + 1] * o_c[:, sl]
                                            + gates[3 * r + 2:3 * r + 3] * o_w[:, sl])

    n_pairs = n_full // 2

    def slot0_tile(i):
        return jnp.where(i == 0, n_full, 2 * i - 1)

    def pair_step(i, carry):
        scores(2 * i, 1)
        softmax_pv(slot0_tile(i), 0)
        scores(2 * i + 1, 0)
        softmax_pv(2 * i, 1)
        return carry

    lax.fori_loop(0, n_pairs, pair_step, 0)

    @pl.when(n_full % 2 == 1)
    def _():
        scores(2 * n_pairs, 1)
        softmax_pv(slot0_tile(n_pairs), 0)
        softmax_pv(2 * n_pairs, 1)

    @pl.when(n_full % 2 == 0)
    def _():
        softmax_pv(slot0_tile(n_pairs), 0)

    acc_s = acc_ref[...]
    o_s = acc_s[:dk] * (1.0 / acc_s[dk:dk + 1])
    gates = jax.nn.sigmoid(gl_ref[0, 0])
    for r in range(NSA_REP):
        sl = slice(r * tq, (r + 1) * tq)
        o_ref[0, r * dk:(r + 1) * dk, :] = (
            part_ref[r * dk:(r + 1) * dk, :] + gates[3 * r + 1:3 * r + 2] * o_s[:, sl]
        ).astype(o_ref.dtype)


def _nsa_key_value_layouts(k_s, v_s, k_w, v_w):
    bsz, n_grp, seq, dk = k_s.shape
    tq, tk = NSA_TQ, NSA_TK
    lead = (bsz, n_grp)
    blk_in_tile = (np.arange(seq) // SEL_BLK) % (tk // SEL_BLK)
    onehot = jnp.asarray(blk_in_tile[:, None] == np.arange(NSA_AUX)[None, :], BF16)
    ks = jnp.concatenate([k_s, jnp.broadcast_to(onehot, lead + (seq, NSA_AUX)),
                          jnp.zeros(lead + (seq, LANES - dk - NSA_AUX), BF16)], axis=-1)
    ks = ks.reshape(lead + (seq // tk, tk, LANES))
    pad_row = jnp.zeros((LANES,), BF16).at[dk].set(1.0)
    kw = jnp.concatenate([k_w, jnp.zeros(lead + (seq, LANES - dk), BF16)], axis=-1)
    kw = jnp.concatenate([jnp.broadcast_to(pad_row, lead + (WINDOW, LANES)), kw], axis=2)
    kw = kw.reshape(lead + ((seq + WINDOW) // tq, tq, LANES))
    ones_s = jnp.ones(lead + (seq // tk, NSA_AUX, tk), BF16)
    vs = v_s.reshape(lead + (seq // tk, tk, dk)).transpose(0, 1, 2, 4, 3)
    vs = jnp.concatenate([vs, ones_s], axis=3)
    vw = jnp.concatenate([jnp.zeros(lead + (WINDOW, dk), BF16), v_w], axis=2)
    vw = vw.reshape(lead + ((seq + WINDOW) // tq, tq, dk)).transpose(0, 1, 2, 4, 3)
    vw = jnp.concatenate([vw, jnp.ones(lead + ((seq + WINDOW) // tq, NSA_AUX, tq), BF16)], axis=3)
    return ks, kw, vs, vw


def _nsa_attention(qT, ks, kw, vsT, vwT, kc, vcT, glT, overlapT):
    bsz, hd, seq = qT.shape
    n_grp = NSA_KV_HEADS
    tq, tk = NSA_TQ, NSA_TK
    n_cmp = kc.shape[2]
    n_blk = seq // SEL_BLK
    gdim = glT.shape[2]
    v_rows = NSA_HEAD_DIM + NSA_AUX
    whole = lambda b, g, t: (b, g, 0, 0, 0)
    return pl.pallas_call(
        _nsa_kernel,
        grid=(bsz, n_grp, seq // tq),
        in_specs=[
            pl.BlockSpec((1, NSA_REP * NSA_HEAD_DIM, tq), lambda b, g, t: (b, g, t)),
            pl.BlockSpec((1, 1) + ks.shape[2:], whole),
            pl.BlockSpec((1, 1) + kw.shape[2:], whole),
            pl.BlockSpec((1, 1, seq // tk, v_rows, tk), whole),
            pl.BlockSpec((1, 1, (seq + WINDOW) // tq, v_rows, tq), whole),
            pl.BlockSpec((1, 1, n_cmp, NSA_HEAD_DIM), lambda b, g, t: (b, g, 0, 0)),
            pl.BlockSpec((1, 1, NSA_HEAD_DIM, n_cmp), lambda b, g, t: (b, g, 0, 0)),
            pl.BlockSpec((n_blk, n_cmp), lambda b, g, t: (0, 0)),
            pl.BlockSpec((1, 1, gdim, tq), lambda b, g, t: (b, g, 0, t)),
        ],
        out_specs=pl.BlockSpec((1, NSA_REP * NSA_HEAD_DIM, tq), lambda b, g, t: (b, g, t)),
        out_shape=jax.ShapeDtypeStruct((bsz, hd, seq), BF16),
        scratch_shapes=[pltpu.VMEM((n_blk + NSA_AUX, tq), F32),
                        pltpu.VMEM((2, tk, NSA_REP * tq), F32),
                        pltpu.VMEM((1, NSA_REP * tq), F32),
                        pltpu.VMEM((v_rows, NSA_REP * tq), F32),
                        pltpu.VMEM((NSA_REP * NSA_HEAD_DIM, tq), F32)],
        compiler_params=_params(("parallel", "parallel", "arbitrary")), name="nsa_attention",
    )(qT, ks, kw, vsT, vwT, kc, vcT, overlapT, glT)


def _bmm(a, b, dims):
    a = a.astype(BF16)
    b = b.astype(BF16)
    return jnp.stack([lax.dot_general(a[n], b[n], dims, preferred_element_type=F32)
                      for n in range(a.shape[0])])


def _gdn_kernel(x_ref, halo_ref, z_ref, small_ref, cw_ref, alog_ref, dtb_ref, nw_ref,
                o_ref, state_ref, xc_ref):
    bsz = x_ref.shape[0]
    cs = GDN_CHUNK
    hd = GDN_HEAD_DIM
    nh = GDN_HEADS
    c = pl.program_id(0)

    @pl.when(c == 0)
    def _():
        state_ref[...] = jnp.zeros_like(state_ref)

    xc_ref[:, 0:SUBLANES, :] = jnp.where(c > 0, halo_ref[...], 0.0)
    xc_ref[:, SUBLANES:, :] = x_ref[...]
    cw = cw_ref[...]
    y = cw[GDN_CONV - 1:GDN_CONV, :] * x_ref[...]
    for i in range(GDN_CONV - 1):
        off = SUBLANES - (GDN_CONV - 1) + i
        y = y + cw[i:i + 1, :] * xc_ref[:, off:off + cs, :]
    y = y * jax.nn.sigmoid(y)

    small = small_ref[...]
    sp_in = small + dtb_ref[...]
    softplus = jnp.maximum(sp_in, 0.0) + jnp.log1p(jnp.exp(-jnp.abs(sp_in)))
    g_all = -jnp.exp(alog_ref[...]) * softplus
    beta_all = jax.nn.sigmoid(small)
    ii = lax.broadcasted_iota(jnp.int32, (cs, cs), 0)
    jj = lax.broadcasted_iota(jnp.int32, (cs, cs), 1)
    incl = ii >= jj
    strict = ii > jj
    tril = jnp.where(incl, 1.0, 0.0).astype(BF16)
    e128 = jnp.where(lax.broadcasted_iota(jnp.int32, (LANES, LANES), 0)
                     == lax.broadcasted_iota(jnp.int32, (LANES, LANES), 1), 1.0, 0.0).astype(BF16)
    gc_all = [_dot_exact_lhs(tril, g_all[b]) for b in range(bsz)]
    gcT_all = [_dot_exact_lhs(e128, gc_all[b], NT_DIMS) for b in range(bsz)]

    chains = [(b, h) for b in range(bsz) for h in range(nh)]

    def heads(a, base):
        return jnp.stack([a[b][:, base + h * hd:base + (h + 1) * hd] for b, h in chains])

    def lane_bcast(cols, base):
        return jnp.stack([jnp.broadcast_to(cols[b][:, base + h:base + h + 1], (cs, hd))
                          for b, h in chains])

    n_qk = nh * hd
    q = heads(y, 0)
    k = heads(y, n_qk)
    v = heads(y, 2 * n_qk)
    q = q * lax.rsqrt(jnp.sum(q * q, axis=-1, keepdims=True) + NORM_EPS) * (hd ** -0.5)
    k = k * lax.rsqrt(jnp.sum(k * k, axis=-1, keepdims=True) + NORM_EPS)
    gc = lane_bcast(gc_all, SMALL_A)
    beta = lane_bcast([beta_all[b] for b in range(bsz)], SMALL_B)
    gc_row = jnp.stack([gcT_all[b][SMALL_A + h:SMALL_A + h + 1, :] for b, h in chains])

    decay = jnp.exp(jnp.where(incl, gc[:, :, :cs] - gc_row, MASK_VALUE))
    kk = _bmm(k, k, NT_DIMS)
    m_full = jnp.where(strict, -(beta[:, :, :cs] * kk * decay), 0.0)
    blk_diff = ii ^ jj
    inv_m1 = jnp.where(blk_diff < 2, m_full, 0.0)
    size = 2
    while size < cs:
        m_off = jnp.where((blk_diff >= size) & (blk_diff < 2 * size), m_full, 0.0)
        t_mat = m_off + _bmm(inv_m1, m_off, NN_DIMS)
        inv_m1 = inv_m1 + t_mat + _bmm(t_mat, inv_m1, NN_DIMS)
        size *= 2
    e_gc = jnp.exp(gc)
    rhs = jnp.concatenate([v * beta, k * (beta * e_gc)], axis=2)
    sol = rhs + _bmm(inv_m1, rhs, NN_DIMS)
    u, w = sol[:, :, :hd], sol[:, :, hd:]
    qk = _bmm(q, k, NT_DIMS) * decay
    q_dec = q * e_gc
    gc_last = gc[:, cs - 1:cs, :]
    k_dec = k * jnp.exp(gc_last - gc)
    state = state_ref[...]
    v_new = u - _bmm(w, state, NN_DIMS)
    o = _bmm(q_dec, state, NN_DIMS) + _bmm(qk, v_new, NN_DIMS)
    state_ref[...] = state * jnp.exp(gc_last) + _bmm(k_dec, v_new, TN_DIMS)

    o = o * lax.rsqrt(jnp.mean(o * o, axis=-1, keepdims=True) + NORM_EPS) * nw_ref[...]
    z = heads(z_ref[...], 0)
    o = o * (z * jax.nn.sigmoid(z))
    for b in range(bsz):
        o_ref[b] = jnp.concatenate([o[b * nh + h] for h in range(nh)], axis=1).astype(o_ref.dtype)


def _gated_deltanet(pc, small, conv_w, alog_pad, dtb_pad, norm_w, bsz, seq):
    cs = GDN_CHUNK
    qkv_cols = 3 * GDN_HEADS * GDN_HEAD_DIM
    z_cols = GDN_HEADS * GDN_HEAD_DIM
    halo_per_chunk = cs // SUBLANES
    pc3 = pc.reshape(bsz, seq, pc.shape[1])
    small3 = small.reshape(bsz, seq, LANES)
    out = pl.pallas_call(
        _gdn_kernel,
        grid=(seq // cs,),
        in_specs=[
            pl.BlockSpec((bsz, cs, qkv_cols), lambda c: (0, c, 0)),
            pl.BlockSpec((bsz, SUBLANES, qkv_cols),
                         lambda c: (0, jnp.maximum(c * halo_per_chunk - 1, 0), 0)),
            pl.BlockSpec((bsz, cs, z_cols), lambda c: (0, c, qkv_cols // z_cols)),
            pl.BlockSpec((bsz, cs, LANES), lambda c: (0, c, 0)),
            pl.BlockSpec((GDN_CONV, qkv_cols), lambda c: (0, 0)),
            pl.BlockSpec((1, LANES), lambda c: (0, 0)),
            pl.BlockSpec((1, LANES), lambda c: (0, 0)),
            pl.BlockSpec((1, GDN_HEAD_DIM), lambda c: (0, 0)),
        ],
        out_specs=pl.BlockSpec((bsz, cs, z_cols), lambda c: (0, c, 0)),
        out_shape=jax.ShapeDtypeStruct((bsz, seq, z_cols), BF16),
        scratch_shapes=[pltpu.VMEM((bsz * GDN_HEADS, GDN_HEAD_DIM, GDN_HEAD_DIM), F32),
                        pltpu.VMEM((bsz, cs + SUBLANES, qkv_cols), F32)],
        compiler_params=_params(("arbitrary",)), name="gated_deltanet",
    )(pc3, pc3, pc3, small3, conv_w, alog_pad, dtb_pad, norm_w.reshape(1, GDN_HEAD_DIM))
    return out.reshape(bsz * seq, z_cols)


def _merge_kernel(ya_ref, yb_ref, ma_ref, mb_ref, x_ref, mod_ref, wa_ref, wb_ref, wo_ref,
                  nf_ref, x1_ref, h2_ref):
    pa = jnp.dot(ya_ref[...], wa_ref[...], preferred_element_type=F32)
    pb = jnp.dot(yb_ref[...], wb_ref[...], preferred_element_type=F32)
    mixed = jax.nn.sigmoid(ma_ref[...]) * pa + jax.nn.sigmoid(mb_ref[...]) * pb
    mod = mod_ref[0]
    x1 = x_ref[...] + mod[GT_M:GT_M + 1, :] * jnp.dot(
        mixed.astype(BF16), wo_ref[...], preferred_element_type=F32)
    x1_ref[...] = x1
    h2_ref[...] = _modulated_norm(x1, nf_ref[...], mod, SC_F, SH_F).astype(BF16)


def _merge_project(ya, yb, pc, x2, mod, w_pa, w_pb, w_o, norm_ffn, seq):
    t, d = x2.shape
    tiles_per_batch = seq // ROW_TILE
    merge_blk = (3 * GDN_HEADS * GDN_HEAD_DIM + GDN_HEADS * GDN_HEAD_DIM) // d
    row = lambda i: (i, 0)
    const = lambda i: (0, 0)
    return pl.pallas_call(
        _merge_kernel,
        grid=(t // ROW_TILE,),
        in_specs=[pl.BlockSpec((ROW_TILE, d), row),
                  pl.BlockSpec((ROW_TILE, d), row),
                  pl.BlockSpec((ROW_TILE, d), lambda i: (i, merge_blk)),
                  pl.BlockSpec((ROW_TILE, d), lambda i: (i, merge_blk + 1)),
                  pl.BlockSpec((ROW_TILE, d), row),
                  pl.BlockSpec((1, 6, d), lambda i: (i // tiles_per_batch, 0, 0)),
                  pl.BlockSpec((d, d), const),
                  pl.BlockSpec((d, d), const),
                  pl.BlockSpec((d, d), const),
                  pl.BlockSpec((1, d), const)],
        out_specs=[pl.BlockSpec((ROW_TILE, d), row), pl.BlockSpec((ROW_TILE, d), row)],
        out_shape=[jax.ShapeDtypeStruct((t, d), F32), jax.ShapeDtypeStruct((t, d), BF16)],
        compiler_params=_params(("parallel",)), name="merge_project",
    )(ya, yb, pc, pc, x2, mod, w_pa, w_pb, w_o, norm_ffn.reshape(1, d))


def _ffn_kernel(h_ref, halo_ref, x1_ref, mod_ref, wg_ref, wv_ref, cwg_ref, cwv_ref,
                cbg_ref, cbv_ref, wd_ref, nfin_ref, o_ref, acc_ref, ug_ref, uv_ref,
                *, tiles_per_batch):
    i = pl.program_id(0)
    f = pl.program_id(1)
    tm = h_ref.shape[0]
    first_of_batch = (i % tiles_per_batch) == 0

    @pl.when(f == 0)
    def _():
        acc_ref[...] = jnp.zeros_like(acc_ref)

    h = h_ref[...]
    halo = halo_ref[...]

    def conv_half(w_ref, cw_ref, cb_ref, u_ref):
        u_ref[SUBLANES:, :] = jnp.dot(h, w_ref[...], preferred_element_type=F32)
        up_halo = jnp.dot(halo, w_ref[...], preferred_element_type=F32)[SUBLANES:]
        u_ref[0:SUBLANES, :] = jnp.where(first_of_batch, 0.0, up_halo)
        cw = cw_ref[...]
        out = cb_ref[...] + cw[FFN_CONV - 1:FFN_CONV, :] * u_ref[SUBLANES:, :]
        for j in range(FFN_CONV - 1):
            off = SUBLANES - (FFN_CONV - 1) + j
            out = out + cw[j:j + 1, :] * u_ref[off:off + tm, :]
        return out

    gate = conv_half(wg_ref, cwg_ref, cbg_ref, ug_ref)
    val = conv_half(wv_ref, cwv_ref, cbv_ref, uv_ref)
    act = gate * jax.nn.sigmoid(gate) * val
    acc_ref[...] += jnp.dot(act.astype(BF16), wd_ref[...], preferred_element_type=F32)

    @pl.when(f == pl.num_programs(1) - 1)
    def _():
        mod = mod_ref[0]
        x2 = x1_ref[...] + mod[GT_F:GT_F + 1, :] * acc_ref[...]
        ms = jnp.mean(x2 * x2, axis=-1, keepdims=True)
        o_ref[...] = x2 * lax.rsqrt(ms + NORM_EPS) * nfin_ref[...]


def _conv_ffn(h2, x1, mod, w_up, conv_w, conv_b, w_down, norm_final, seq):
    t, d = x1.shape
    tm, tf = ROW_TILE, FFN_TF
    n_f = FFN_DIM // tf
    tiles_per_batch = seq // tm
    halo_per_tile = tm // SUBLANES
    row = lambda i, f: (i, 0)
    return pl.pallas_call(
        functools.partial(_ffn_kernel, tiles_per_batch=tiles_per_batch),
        grid=(t // tm, n_f),
        in_specs=[pl.BlockSpec((tm, d), row),
                  pl.BlockSpec((SUBLANES * 2, d),
                               lambda i, f: (jnp.maximum(i * (halo_per_tile // 2) - 1, 0), 0)),
                  pl.BlockSpec((tm, d), row),
                  pl.BlockSpec((1, 6, d), lambda i, f: (i // tiles_per_batch, 0, 0)),
                  pl.BlockSpec((d, tf), lambda i, f: (0, f)),
                  pl.BlockSpec((d, tf), lambda i, f: (0, n_f + f)),
                  pl.BlockSpec((FFN_CONV, tf), lambda i, f: (0, f)),
                  pl.BlockSpec((FFN_CONV, tf), lambda i, f: (0, n_f + f)),
                  pl.BlockSpec((1, tf), lambda i, f: (0, f)),
                  pl.BlockSpec((1, tf), lambda i, f: (0, n_f + f)),
                  pl.BlockSpec((tf, d), lambda i, f: (f, 0)),
                  pl.BlockSpec((1, d), lambda i, f: (0, 0))],
        out_specs=pl.BlockSpec((tm, d), row),
        out_shape=jax.ShapeDtypeStruct((t, d), F32),
        scratch_shapes=[pltpu.VMEM((tm, d), F32),
                        pltpu.VMEM((tm + SUBLANES, tf), F32),
                        pltpu.VMEM((tm + SUBLANES, tf), F32)],
        compiler_params=_params(("parallel", "arbitrary")), name="conv_ffn",
    )(h2, h2, x1, mod, w_up, w_up, conv_w, conv_w, conv_b, conv_b, w_down,
      norm_final.reshape(1, d))


def _overlap_t(seq):
    n_cmp = (seq - CMP_BLK) // CMP_STRIDE + 1
    n_blk = seq // SEL_BLK
    c_start = np.arange(n_cmp) * CMP_STRIDE
    s_start = np.arange(n_blk) * SEL_BLK
    ov = np.clip(np.minimum(c_start[:, None] + CMP_BLK, s_start[None, :] + SEL_BLK)
                 - np.maximum(c_start[:, None], s_start[None, :]), 0, None) / CMP_BLK
    n_half = seq // CMP_STRIDE
    out = np.zeros((n_blk, n_half), np.float32)
    out[:, :n_cmp] = ov.T
    return jnp.asarray(out, BF16)


def kernel(x, c, w_ada, b_ada, norm_mix, w_in, nsa_pos_k, nsa_pos_v, nsa_ck_w1, nsa_ck_b1, nsa_ck_w2, nsa_ck_b2, nsa_cv_w1, nsa_cv_b1, nsa_cv_w2, nsa_cv_b2, gdn_conv, gdn_a_log, gdn_dt_bias, gdn_norm, w_proj_nsa, w_proj_gdn, w_out, norm_ffn, ffn_up, ffn_conv, ffn_conv_b, ffn_down, norm_final):
    bsz, seq, d = x.shape
    depth = w_ada.shape[0]
    n_grp, dk = NSA_KV_HEADS, NSA_HEAD_DIM
    q_cols = NSA_HEADS * dk
    kv_cols = 6 * n_grp * dk
    gate_cols = 3 * NSA_HEADS
    gqkv_cols = 3 * GDN_HEADS * GDN_HEAD_DIM
    z_cols = GDN_HEADS * GDN_HEAD_DIM
    o_kv = q_cols
    o_gl = o_kv + kv_cols
    o_gq = o_gl + gate_cols
    o_a = o_gq + gqkv_cols
    o_b = o_a + GDN_HEADS
    o_z = o_b + GDN_HEADS
    o_mg = o_z + z_cols
    overlap_t = _overlap_t(seq)
    x2 = x.reshape(bsz * seq, d)
    assert depth == 1, "kernel supports the problem's DEPTH == 1"
    for l in range(depth):
        mod = _adaln_mod(c, w_ada[l], b_ada[l])
        w = w_in[l]
        w_att = jnp.concatenate([w[:, :o_kv] * (dk ** -0.5 * LOG2_E), w[:, o_kv:o_gl]],
                                axis=1).astype(BF16)
        w_small = jnp.zeros((d, LANES), F32)
        w_small = w_small.at[:, SMALL_GATE:SMALL_GATE + gate_cols].set(w[:, o_gl:o_gq])
        w_small = w_small.at[:, SMALL_A:SMALL_A + 2 * GDN_HEADS].set(w[:, o_a:o_z]).astype(BF16)
        w_rest = jnp.concatenate([w[:, o_gq:o_a], w[:, o_z:]], axis=1).astype(BF16)

        p_att = _norm_mod_matmul(x2, norm_mix[l], mod, w_att, 1280, BF16, seq)
        p_small = _norm_mod_matmul(x2, norm_mix[l], mod, w_small, LANES, F32, seq)
        pc = _norm_mod_matmul(x2, norm_mix[l], mod, w_rest, 2048, F32, seq)

        qT = p_att[:, :q_cols].reshape(bsz, seq, q_cols).transpose(0, 2, 1)
        kv = p_att[:, q_cols:].reshape(bsz, seq, 6, n_grp, dk).transpose(0, 2, 3, 1, 4)
        kv_half = kv.reshape(bsz, 6, n_grp, seq // CMP_STRIDE, CMP_STRIDE * dk)
        ks, kw, vsT, vwT = _nsa_key_value_layouts(kv[:, 2], kv[:, 3], kv[:, 4], kv[:, 5])
        glT = p_small[:, SMALL_GATE:SMALL_GATE + gate_cols].reshape(bsz, seq, n_grp, 3 * NSA_REP)
        glT = jnp.pad(glT.transpose(0, 2, 3, 1), ((0, 0), (0, 0), (0, 4), (0, 0)))

        pos = jnp.stack([nsa_pos_k[l], nsa_pos_v[l]]).reshape(2, 1, CMP_BLK * dk)
        pos = jnp.broadcast_to(pos, (2, 2 * SUBLANES, CMP_BLK * dk)).astype(BF16)
        cw1 = jnp.stack([nsa_ck_w1[l], nsa_cv_w1[l]]).astype(BF16)
        cb1 = jnp.stack([nsa_ck_b1[l], nsa_cv_b1[l]]).reshape(2, 1, CMP_HIDDEN)
        cw2 = jnp.stack([nsa_ck_w2[l], nsa_cv_w2[l]]).astype(BF16)
        cb2 = jnp.stack([nsa_ck_b2[l], nsa_cv_b2[l]]).reshape(2, 1, dk)
        cmp = _compress(kv_half, pos, cw1, cb1, cw2, cb2)
        kc = cmp[0].astype(BF16)
        vcT = cmp[1].astype(BF16).transpose(0, 1, 3, 2)

        yT = _nsa_attention(qT, ks, kw, vsT, vwT, kc, vcT, glT, overlap_t)
        y_a = yT.transpose(0, 2, 1).reshape(bsz * seq, q_cols)

        alog_pad = jnp.zeros((1, LANES), F32).at[0, SMALL_A:SMALL_A + GDN_HEADS].set(gdn_a_log[l])
        dtb_pad = jnp.zeros((1, LANES), F32).at[0, SMALL_A:SMALL_A + GDN_HEADS].set(gdn_dt_bias[l])
        y_b = _gated_deltanet(pc, p_small, gdn_conv[l], alog_pad, dtb_pad, gdn_norm[l], bsz, seq)

        x1, h2 = _merge_project(y_a, y_b, pc, x2, mod, w_proj_nsa[l].astype(BF16),
                                w_proj_gdn[l].astype(BF16), w_out[l].astype(BF16),
                                norm_ffn[l], seq)
        out = _conv_ffn(h2, x1, mod, ffn_up[l].astype(BF16), ffn_conv[l],
                        ffn_conv_b[l].reshape(1, 2 * FFN_DIM), ffn_down[l].astype(BF16),
                        norm_final, seq)
    return out.reshape(bsz, seq, d)
```

```python
import functools

import numpy as np
import jax
import jax.numpy as jnp
from jax import lax
from jax.experimental import pallas as pl
from jax.experimental.pallas import tpu as pltpu

F32 = jnp.float32
BF16 = jnp.bfloat16
HIGHEST = lax.Precision.HIGHEST

NSA_HEAD_DIM = 64
NSA_HEADS = 16
NSA_KV_HEADS = 4
NSA_REP = NSA_HEADS // NSA_KV_HEADS
CMP_BLK = 32
CMP_STRIDE = 16
CMP_HIDDEN = 256
SEL_BLK = 64
N_SEL = 16
WINDOW = 512
GDN_HEAD_DIM = 128
GDN_HEADS = 8
GDN_CONV = 4
GDN_CHUNK = 64
FFN_DIM = 2816
FFN_CONV = 3
NORM_EPS = 1e-6
MASK_VALUE = -1e30

LANES = 128
SUBLANES = 8
VMEM_LIMIT = 56 * 1024 * 1024

NSA_TQ = 256
NSA_TK = 512
NSA_AUX = 16
LOG2_E = 1.4426950408889634
ROW_TILE = 512
FFN_TF = 1408

SH_M, SC_M, GT_M, SH_F, SC_F, GT_F = range(6)
SMALL_A = NSA_KV_HEADS * NSA_AUX
SMALL_B = SMALL_A + GDN_HEADS


def _params(sem):
    return pltpu.CompilerParams(dimension_semantics=sem, vmem_limit_bytes=VMEM_LIMIT)


def _split3(a):
    hi = a.astype(BF16)
    r1 = a - hi.astype(F32)
    mid = r1.astype(BF16)
    lo = (r1 - mid.astype(F32)).astype(BF16)
    return hi, mid, lo


def _dot_exact_lhs(a_bf16, b, dims=None):
    if dims is None:
        dims = (((a_bf16.ndim - 1,), (0,)), ((), ()))
    out = None
    for part in _split3(b):
        term = lax.dot_general(a_bf16, part, dimension_numbers=dims, preferred_element_type=F32)
        out = term if out is None else out + term
    return out


NT_DIMS = (((1,), (1,)), ((), ()))
TN_DIMS = (((0,), (0,)), ((), ()))
NN_DIMS = (((1,), (0,)), ((), ()))


def _mod_kernel(c_ref, w_ref, b_ref, o_ref):
    o_ref[...] = jnp.dot(c_ref[...], w_ref[...], preferred_element_type=F32,
                         precision=HIGHEST) + b_ref[...]


def _adaln_mod(c, w_ada, b_ada):
    bsz, d = c.shape
    n = w_ada.shape[1]
    c8 = jnp.zeros((SUBLANES, d), F32).at[:bsz].set(c)
    out = pl.pallas_call(
        _mod_kernel,
        grid=(n // d,),
        in_specs=[pl.BlockSpec((SUBLANES, d), lambda j: (0, 0)),
                  pl.BlockSpec((d, d), lambda j: (0, j)),
                  pl.BlockSpec((1, d), lambda j: (0, j))],
        out_specs=pl.BlockSpec((SUBLANES, d), lambda j: (0, j)),
        out_shape=jax.ShapeDtypeStruct((SUBLANES, n), F32),
        compiler_params=_params(("arbitrary",)), name="adaln_mod",
    )(c8, w_ada, b_ada.reshape(1, n))
    return out[:bsz].reshape(bsz, n // d, d)


def _modulated_norm(x, norm_w, mod, sc_row, sh_row):
    ms = jnp.mean(x * x, axis=-1, keepdims=True)
    y = x * lax.rsqrt(ms + NORM_EPS) * norm_w
    return y * (1.0 + mod[sc_row:sc_row + 1, :]) + mod[sh_row:sh_row + 1, :]


def _nmm_kernel(x_ref, nw_ref, mod_ref, w_ref, o_ref, h_ref):
    @pl.when(pl.program_id(1) == 0)
    def _():
        h = _modulated_norm(x_ref[...], nw_ref[...], mod_ref[0], SC_M, SH_M)
        h_ref[...] = h.astype(BF16)

    o_ref[...] = jnp.dot(h_ref[...], w_ref[...],
                         preferred_element_type=F32).astype(o_ref.dtype)


def _norm_mod_matmul(x2, norm_w, mod, w, tn, out_dtype, seq):
    t, d = x2.shape
    n = w.shape[1]
    tiles_per_batch = seq // ROW_TILE
    return pl.pallas_call(
        _nmm_kernel,
        grid=(t // ROW_TILE, n // tn),
        in_specs=[pl.BlockSpec((ROW_TILE, d), lambda i, j: (i, 0)),
                  pl.BlockSpec((1, d), lambda i, j: (0, 0)),
                  pl.BlockSpec((1, 6, d), lambda i, j: (i // tiles_per_batch, 0, 0)),
                  pl.BlockSpec((d, tn), lambda i, j: (0, j))],
        out_specs=pl.BlockSpec((ROW_TILE, tn), lambda i, j: (i, j)),
        out_shape=jax.ShapeDtypeStruct((t, n), out_dtype),
        scratch_shapes=[pltpu.VMEM((ROW_TILE, d), BF16)],
        compiler_params=_params(("parallel", "arbitrary")), name="norm_mod_proj",
    )(x2, norm_w.reshape(1, d), mod, w)


def _compress_kernel(t_ref, wc_ref, pos_ref, w1_ref, b1_ref, w2_ref, b2_ref, eye_ref,
                     kc_ref, vcT_ref):
    n_half = t_ref.shape[1] // CMP_STRIDE
    acc = None
    for l in range(CMP_STRIDE):
        tok = t_ref[0, pl.ds(l, n_half, stride=CMP_STRIDE), :].astype(BF16)
        term = jnp.dot(tok, wc_ref[l], preferred_element_type=F32)
        acc = term if acc is None else acc + term
    outs = []
    for kind in range(2):
        top = acc[:, 2 * kind * CMP_HIDDEN:(2 * kind + 1) * CMP_HIDDEN]
        bot = acc[:, (2 * kind + 1) * CMP_HIDDEN:(2 * kind + 2) * CMP_HIDDEN]
        bot_next = pltpu.roll(bot, n_half - 1, 0)
        pos_term = jnp.dot(pos_ref[kind], w1_ref[kind], preferred_element_type=F32)[0:1]
        hid = jax.nn.gelu(top + bot_next + pos_term + b1_ref[kind])
        outs.append(jnp.dot(hid.astype(BF16), w2_ref[kind], preferred_element_type=F32)
                    + b2_ref[kind])
    kc_ref[0, 0] = outs[0].astype(BF16)
    dk = NSA_HEAD_DIM
    vcT_ref[0, 0] = lax.dot_general(eye_ref[:dk, :dk], outs[1].astype(BF16), NT_DIMS,
                                    preferred_element_type=F32).astype(BF16)


def _compress(pc3, wc, pos, w1, b1, w2, b2, eye, first_col_block):
    bsz, seq, _ = pc3.shape
    n_grp, dk = NSA_KV_HEADS, NSA_HEAD_DIM
    n_half = seq // CMP_STRIDE
    const3 = lambda b, g: (0, 0, 0)
    return pl.pallas_call(
        _compress_kernel,
        grid=(bsz, n_grp),
        in_specs=[pl.BlockSpec((1, seq, LANES), lambda b, g: (b, 0, first_col_block + g)),
                  pl.BlockSpec(wc.shape, const3),
                  pl.BlockSpec(pos.shape, const3),
                  pl.BlockSpec(w1.shape, const3),
                  pl.BlockSpec(b1.shape, const3),
                  pl.BlockSpec(w2.shape, const3),
                  pl.BlockSpec(b2.shape, const3),
                  pl.BlockSpec(eye.shape, lambda b, g: (0, 0))],
        out_specs=[pl.BlockSpec((1, 1, n_half, dk), lambda b, g: (b, g, 0, 0)),
                   pl.BlockSpec((1, 1, dk, n_half), lambda b, g: (b, g, 0, 0))],
        out_shape=[jax.ShapeDtypeStruct((bsz, n_grp, n_half, dk), BF16),
                   jax.ShapeDtypeStruct((bsz, n_grp, dk, n_half), BF16)],
        compiler_params=_params(("parallel", "parallel")), name="nsa_compress",
    )(pc3, wc, pos, w1, b1, w2, b2, eye)


def _nsa_attn_kernel(q_ref, ks_ref, kw_ref, v_ref, kc_ref, vcT_ref, ov_ref, gl_ref, eye_ref,
                     aux_ref, o_ref,
                     ksa_ref, kwa_ref, vsT_ref, vwT_ref, sel_ref, s_ref, m_ref, acc_ref,
                     sw_ref, mw_ref, accw_ref, part_ref, gate_ref):
    tq, tk = NSA_TQ, NSA_TK
    rows = NSA_REP * tq
    dk = NSA_HEAD_DIM
    grp = pl.program_id(1)
    qt = pl.program_id(2)
    qs = qt * tq
    seq = ks_ref.shape[1]
    eye = eye_ref[...]
    tq_per_tk = tk // tq

    @pl.when(qt == 0)
    def _():
        lane = lax.broadcasted_iota(jnp.int32, (tk, LANES), 1)
        lane_flag = jnp.where(lane == dk, 1.0, 0.0).astype(BF16)
        ones_rows = jnp.ones((NSA_AUX, tk), BF16)

        def build(j, carry):
            r0 = pl.multiple_of(j * tk, tk)
            ksa_ref[pl.ds(r0, tk), :] = ks_ref[0, pl.ds(r0, tk), :] + aux_ref[...]
            kwa_ref[pl.ds(r0, tk), :] = kw_ref[0, pl.ds(r0, tk), :] + lane_flag
            vT = lax.dot_general(eye[:LANES, :LANES], v_ref[0, pl.ds(r0, tk), :], NT_DIMS,
                                 preferred_element_type=F32).astype(BF16)
            vsT_ref[j, 0:dk, :] = vT[:dk]
            vsT_ref[j, dk:, :] = ones_rows
            for i in range(tq_per_tk):
                vwT_ref[j * tq_per_tk + i, 0:dk, :] = vT[dk:, i * tq:(i + 1) * tq]
                vwT_ref[j * tq_per_tk + i, dk:, :] = ones_rows[:, :tq]
            return carry

        lax.fori_loop(0, seq // tk, build, 0)

    q4 = lax.dot_general(eye, q_ref[0], NT_DIMS, preferred_element_type=F32).astype(BF16)
    qT = jnp.concatenate([q4[r * dk:(r + 1) * dk, :] for r in range(NSA_REP)], axis=1)
    t_q = qs + lax.broadcasted_iota(jnp.int32, (1, tq), 1)
    t_row = jnp.concatenate([t_q] * NSA_REP, axis=1)
    zpad = jnp.zeros((LANES - dk - NSA_AUX, rows), BF16)

    def per_head(a):
        return jnp.concatenate([a] * NSA_REP, axis=1)

    gate_ref[...] = _dot_exact_lhs(eye[:LANES, :LANES], jax.nn.sigmoid(gl_ref[0]), NT_DIMS)
    gates = gate_ref[pl.ds(pl.multiple_of(grp * NSA_AUX, NSA_AUX), NSA_AUX), :]

    n_cmp = kc_ref.shape[2]
    s_c = jnp.dot(kc_ref[0, 0], qT, preferred_element_type=F32)
    c_end = lax.broadcasted_iota(jnp.int32, (n_cmp, 1), 0) * CMP_STRIDE + (CMP_BLK - 1)
    s_c = s_c + per_head(jnp.where(c_end <= t_q, 0.0, MASK_VALUE))
    mx_c = jnp.max(s_c, axis=0, keepdims=True)
    p_c = jnp.exp2(s_c - mx_c)
    l_c = jnp.sum(p_c, axis=0, keepdims=True)
    p_c = p_c * jnp.where(t_row >= CMP_BLK - 1, 1.0 / l_c, 0.0)
    o_c = jnp.dot(vcT_ref[0, 0], p_c.astype(BF16), preferred_element_type=F32)
    for r in range(NSA_REP):
        part_ref[r * dk:(r + 1) * dk, :] = gates[3 * r:3 * r + 1] * o_c[:, r * tq:(r + 1) * tq]

    p_sum = p_c[:, 0:tq]
    for r in range(1, NSA_REP):
        p_sum = p_sum + p_c[:, r * tq:(r + 1) * tq]
    ov = ov_ref[...]
    imp = _dot_exact_lhs(ov, p_sum)
    n_blk = ov.shape[0]
    j_blk = lax.broadcasted_iota(jnp.int32, (n_blk, 1), 0)
    cur = t_q // SEL_BLK
    forced = (j_blk == 0) | (j_blk == cur) | (j_blk == cur - 1)
    valid = j_blk * SEL_BLK <= t_q
    n_rank = N_SEL - 3
    cand = jnp.where(forced, -jnp.inf, jnp.where(valid, imp, -1.0))

    n_wt = WINDOW // tq + 1
    row0 = lax.broadcasted_iota(jnp.int32, (NSA_AUX, rows), 0) == 0
    a_k = lax.broadcasted_iota(jnp.int32, (tq, 1), 0)
    b_q = lax.broadcasted_iota(jnp.int32, (1, tq), 1)
    w_tiles = [jnp.maximum(qt - (n_wt - 1) + i, 0) for i in range(n_wt)]

    def win_scores(i):
        before_start = qt - (n_wt - 1) + i < 0
        flag = jnp.where(row0 & before_start, MASK_VALUE, 0.0).astype(BF16)
        rhs_w = jnp.concatenate([qT, flag, zpad], axis=0)
        k_tile = kwa_ref[pl.ds(pl.multiple_of(w_tiles[i] * tq, tq), tq), :]
        s = jnp.dot(k_tile, rhs_w, preferred_element_type=F32)
        if i == 0:
            s = s + per_head(jnp.where(a_k > b_q, 0.0, MASK_VALUE))
        if i == n_wt - 1:
            s = s + per_head(jnp.where(a_k <= b_q, 0.0, MASK_VALUE))
        sw_ref[i * tq:(i + 1) * tq, :] = s

    def win_max():
        mw_ref[...] = jnp.max(sw_ref[...], axis=0, keepdims=True)

    def win_pv(i):
        p = jnp.exp2(sw_ref[i * tq:(i + 1) * tq, :] - mw_ref[...]).astype(BF16)
        pv = jnp.dot(vwT_ref[w_tiles[i]], p, preferred_element_type=F32)
        if i == 0:
            accw_ref[...] = pv
        else:
            accw_ref[...] = accw_ref[...] + pv

    win_steps = ([functools.partial(win_scores, i) for i in range(n_wt)] + [win_max]
                 + [functools.partial(win_pv, i) for i in range(n_wt)])

    work = cand
    maxima = []
    for r in range(n_rank):
        mx = jnp.max(work, axis=0, keepdims=True)
        maxima.append(mx)
        work = jnp.where(work == mx, -jnp.inf, work)
        if r < len(win_steps):
            win_steps[r]()
    for step in win_steps[n_rank:]:
        step()

    ones_lhs = jnp.ones((SUBLANES, n_blk), BF16)

    def count(mask):
        return jnp.dot(ones_lhs, jnp.where(mask, 1.0, 0.0).astype(BF16),
                       preferred_element_type=F32)[0:1]

    thr = maxima[-1]
    for mx in reversed(maxima[:-1]):
        thr = jnp.where(count(cand >= mx) >= n_rank, mx, thr)
    ties = cand == thr
    below = (lax.broadcasted_iota(jnp.int32, (n_blk, n_blk), 1)
             < lax.broadcasted_iota(jnp.int32, (n_blk, n_blk), 0))
    tie_rank = jnp.dot(jnp.where(below, 1.0, 0.0).astype(BF16),
                       jnp.where(ties, 1.0, 0.0).astype(BF16), preferred_element_type=F32)
    picked = (cand > thr) | (ties & (tie_rank < n_rank - count(cand > thr)))
    sel_ref[0:n_blk, :] = jnp.where(forced | (picked & (cand >= 0.0)), 0.0, MASK_VALUE)
    sel_ref[n_blk:, :] = jnp.zeros((NSA_AUX, tq), F32)

    acc_w = accw_ref[...]
    o_w = acc_w[:dk] * (1.0 / acc_w[dk:dk + 1])
    for r in range(NSA_REP):
        part_ref[r * dk:(r + 1) * dk, :] = (part_ref[r * dk:(r + 1) * dk, :]
                                            + gates[3 * r + 2:3 * r + 3] * o_w[:, r * tq:(r + 1) * tq])

    blk_per_tile = tk // SEL_BLK
    n_full = qs // tk

    def scores(kt, slot, diagonal=False):
        sb = sel_ref[pl.ds(pl.multiple_of(kt * blk_per_tile, blk_per_tile), NSA_AUX), :]
        rhs = jnp.concatenate([qT, per_head(sb).astype(BF16), zpad], axis=0)
        k_tile = ksa_ref[pl.ds(pl.multiple_of(kt * tk, tk), tk), :]
        s = jnp.dot(k_tile, rhs, preferred_element_type=F32)
        if diagonal:
            kpos = kt * tk + lax.broadcasted_iota(jnp.int32, (tk, 1), 0)
            s = s + per_head(jnp.where(kpos <= t_q, 0.0, MASK_VALUE))
        s_ref[slot] = s

    def softmax_pv(kt, slot):
        m_i = m_ref[...]
        m_new = jnp.maximum(m_i, jnp.max(s_ref[slot], axis=0, keepdims=True))
        alpha = jnp.exp2(m_i - m_new)
        p = jnp.exp2(s_ref[slot] - m_new).astype(BF16)
        pv = jnp.dot(vsT_ref[kt], p, preferred_element_type=F32)
        acc_ref[...] = alpha * acc_ref[...] + pv
        m_ref[...] = m_new

    m_ref[...] = jnp.full((1, rows), MASK_VALUE, F32)
    acc_ref[...] = jnp.zeros((dk + NSA_AUX, rows), F32)
    scores(n_full, 0, diagonal=True)

    n_pairs = n_full // 2

    def slot0_tile(i):
        return jnp.where(i == 0, n_full, 2 * i - 1)

    def pair_step(i, carry):
        scores(2 * i, 1)
        softmax_pv(slot0_tile(i), 0)
        scores(2 * i + 1, 0)
        softmax_pv(2 * i, 1)
        return carry

    lax.fori_loop(0, n_pairs, pair_step, 0)

    @pl.when(n_full % 2 == 1)
    def _():
        scores(2 * n_pairs, 1)
        softmax_pv(slot0_tile(n_pairs), 0)
        softmax_pv(2 * n_pairs, 1)

    @pl.when(n_full % 2 == 0)
    def _():
        softmax_pv(slot0_tile(n_pairs), 0)

    acc_s = acc_ref[...]
    o_s = acc_s[:dk] * (1.0 / acc_s[dk:dk + 1])
    gates = gate_ref[pl.ds(pl.multiple_of(grp * NSA_AUX, NSA_AUX), NSA_AUX), :]
    out_t = jnp.concatenate(
        [part_ref[r * dk:(r + 1) * dk, :] + gates[3 * r + 1:3 * r + 2] * o_s[:, r * tq:(r + 1) * tq]
         for r in range(NSA_REP)], axis=0).astype(BF16)
    o_ref[0] = lax.dot_general(eye[:tq, :tq], out_t, NT_DIMS,
                               preferred_element_type=F32).astype(o_ref.dtype)


def _nsa_attn(p_att3, small3, kc, vcT, overlap_t, eye, aux_s):
    bsz, seq, _ = p_att3.shape
    n_grp, dk = NSA_KV_HEADS, NSA_HEAD_DIM
    tq, tk = NSA_TQ, NSA_TK
    rows = NSA_REP * tq
    q_w = NSA_REP * dk
    n_cmp = kc.shape[2]
    n_blk = seq // SEL_BLK
    v_rows = dk + NSA_AUX
    n_wt = WINDOW // tq + 1
    first = NSA_HEADS * dk // LANES
    kv_spec = lambda off: pl.BlockSpec((1, seq, LANES), lambda b, g, t: (b, 0, first + off + g))
    return pl.pallas_call(
        _nsa_attn_kernel,
        grid=(bsz, n_grp, seq // tq),
        in_specs=[
            pl.BlockSpec((1, tq, q_w), lambda b, g, t: (b, t, g)),
            kv_spec(0), kv_spec(n_grp), kv_spec(2 * n_grp),
            pl.BlockSpec((1, 1, n_cmp, dk), lambda b, g, t: (b, g, 0, 0)),
            pl.BlockSpec((1, 1, dk, n_cmp), lambda b, g, t: (b, g, 0, 0)),
            pl.BlockSpec((n_blk, n_cmp), lambda b, g, t: (0, 0)),
            pl.BlockSpec((1, tq, LANES), lambda b, g, t: (b, t, 0)),
            pl.BlockSpec(eye.shape, lambda b, g, t: (0, 0)),
            pl.BlockSpec((tk, LANES), lambda b, g, t: (0, 0)),
        ],
        out_specs=pl.BlockSpec((1, tq, q_w), lambda b, g, t: (b, t, g)),
        out_shape=jax.ShapeDtypeStruct((bsz, seq, NSA_HEADS * dk), BF16),
        scratch_shapes=[pltpu.VMEM((seq, LANES), BF16),
                        pltpu.VMEM((seq, LANES), BF16),
                        pltpu.VMEM((seq // tk, v_rows, tk), BF16),
                        pltpu.VMEM((seq // tq, v_rows, tq), BF16),
                        pltpu.VMEM((n_blk + NSA_AUX, tq), F32),
                        pltpu.VMEM((2, tk, rows), F32),
                        pltpu.VMEM((1, rows), F32),
                        pltpu.VMEM((v_rows, rows), F32),
                        pltpu.VMEM((n_wt * tq, rows), F32),
                        pltpu.VMEM((1, rows), F32),
                        pltpu.VMEM((v_rows, rows), F32),
                        pltpu.VMEM((q_w, tq), F32),
                        pltpu.VMEM((LANES, tq), F32)],
        compiler_params=_params(("parallel", "parallel", "arbitrary")), name="nsa_attention",
    )(p_att3, p_att3, p_att3, p_att3, kc, vcT, overlap_t, small3, eye, aux_s)


def _bmm(a, b, dims):
    a = a.astype(BF16)
    b = b.astype(BF16)
    return jnp.stack([lax.dot_general(a[n], b[n], dims, preferred_element_type=F32)
                      for n in range(a.shape[0])])


def _gdn_kernel(x_ref, halo_ref, z_ref, small_ref, cw_ref, alog_ref, dtb_ref, nw_ref,
                o_ref, state_ref, xc_ref):
    bsz = x_ref.shape[0]
    cs = GDN_CHUNK
    hd = GDN_HEAD_DIM
    nh = GDN_HEADS
    c = pl.program_id(0)

    @pl.when(c == 0)
    def _():
        state_ref[...] = jnp.zeros_like(state_ref)

    xc_ref[:, 0:SUBLANES, :] = jnp.where(c > 0, halo_ref[...], 0.0)
    xc_ref[:, SUBLANES:, :] = x_ref[...]
    cw = cw_ref[...]
    y = cw[GDN_CONV - 1:GDN_CONV, :] * x_ref[...]
    for i in range(GDN_CONV - 1):
        off = SUBLANES - (GDN_CONV - 1) + i
        y = y + cw[i:i + 1, :] * xc_ref[:, off:off + cs, :]
    y = y * jax.nn.sigmoid(y)

    small = small_ref[...]
    sp_in = small + dtb_ref[...]
    softplus = jnp.maximum(sp_in, 0.0) + jnp.log1p(jnp.exp(-jnp.abs(sp_in)))
    g_all = -jnp.exp(alog_ref[...]) * softplus
    beta_all = jax.nn.sigmoid(small)
    ii = lax.broadcasted_iota(jnp.int32, (cs, cs), 0)
    jj = lax.broadcasted_iota(jnp.int32, (cs, cs), 1)
    incl = ii >= jj
    strict = ii > jj
    tril = jnp.where(incl, 1.0, 0.0).astype(BF16)
    e128 = jnp.where(lax.broadcasted_iota(jnp.int32, (LANES, LANES), 0)
                     == lax.broadcasted_iota(jnp.int32, (LANES, LANES), 1), 1.0, 0.0).astype(BF16)
    gc_all = [_dot_exact_lhs(tril, g_all[b]) for b in range(bsz)]
    gcT_all = [_dot_exact_lhs(e128, gc_all[b], NT_DIMS) for b in range(bsz)]

    chains = [(b, h) for b in range(bsz) for h in range(nh)]

    def heads(a, base):
        return jnp.stack([a[b][:, base + h * hd:base + (h + 1) * hd] for b, h in chains])

    def lane_bcast(cols, base):
        return jnp.stack([jnp.broadcast_to(cols[b][:, base + h:base + h + 1], (cs, hd))
                          for b, h in chains])

    n_qk = nh * hd
    q = heads(y, 0)
    k = heads(y, n_qk)
    v = heads(y, 2 * n_qk)
    q = q * lax.rsqrt(jnp.sum(q * q, axis=-1, keepdims=True) + NORM_EPS) * (hd ** -0.5)
    k = k * lax.rsqrt(jnp.sum(k * k, axis=-1, keepdims=True) + NORM_EPS)
    gc = lane_bcast(gc_all, SMALL_A)
    beta = lane_bcast([beta_all[b] for b in range(bsz)], SMALL_B)
    gc_row = jnp.stack([gcT_all[b][SMALL_A + h:SMALL_A + h + 1, :] for b, h in chains])

    decay = jnp.exp(jnp.where(incl, gc[:, :, :cs] - gc_row, MASK_VALUE))
    kk = _bmm(k, k, NT_DIMS)
    m_full = jnp.where(strict, -(beta[:, :, :cs] * kk * decay), 0.0)
    blk_diff = ii ^ jj
    inv_m1 = jnp.where(blk_diff < 2, m_full, 0.0)
    size = 2
    while size < cs:
        m_off = jnp.where((blk_diff >= size) & (blk_diff < 2 * size), m_full, 0.0)
        t_mat = m_off + _bmm(inv_m1, m_off, NN_DIMS)
        inv_m1 = inv_m1 + t_mat + _bmm(t_mat, inv_m1, NN_DIMS)
        size *= 2
    e_gc = jnp.exp(gc)
    rhs = jnp.concatenate([v * beta, k * (beta * e_gc)], axis=2)
    sol = rhs + _bmm(inv_m1, rhs, NN_DIMS)
    u, w = sol[:, :, :hd], sol[:, :, hd:]
    qk = _bmm(q, k, NT_DIMS) * decay
    q_dec = q * e_gc
    gc_last = gc[:, cs - 1:cs, :]
    k_dec = k * jnp.exp(gc_last - gc)
    state = state_ref[...]
    v_new = u - _bmm(w, state, NN_DIMS)
    o = _bmm(q_dec, state, NN_DIMS) + _bmm(qk, v_new, NN_DIMS)
    state_ref[...] = state * jnp.exp(gc_last) + _bmm(k_dec, v_new, TN_DIMS)

    o = o * lax.rsqrt(jnp.mean(o * o, axis=-1, keepdims=True) + NORM_EPS) * nw_ref[...]
    z = heads(z_ref[...], 0)
    o = o * (z * jax.nn.sigmoid(z))
    for b in range(bsz):
        o_ref[b] = jnp.concatenate([o[b * nh + h] for h in range(nh)], axis=1).astype(o_ref.dtype)


def _gated_deltanet(pc, small, conv_w, alog_pad, dtb_pad, norm_w, bsz, seq):
    cs = GDN_CHUNK
    qkv_cols = 3 * GDN_HEADS * GDN_HEAD_DIM
    z_cols = GDN_HEADS * GDN_HEAD_DIM
    halo_per_chunk = cs // SUBLANES
    pc3 = pc.reshape(bsz, seq, pc.shape[1])
    small3 = small.reshape(bsz, seq, LANES)
    out = pl.pallas_call(
        _gdn_kernel,
        grid=(seq // cs,),
        in_specs=[
            pl.BlockSpec((bsz, cs, qkv_cols), lambda c: (0, c, 0)),
            pl.BlockSpec((bsz, SUBLANES, qkv_cols),
                         lambda c: (0, jnp.maximum(c * halo_per_chunk - 1, 0), 0)),
            pl.BlockSpec((bsz, cs, z_cols), lambda c: (0, c, qkv_cols // z_cols)),
            pl.BlockSpec((bsz, cs, LANES), lambda c: (0, c, 0)),
            pl.BlockSpec((GDN_CONV, qkv_cols), lambda c: (0, 0)),
            pl.BlockSpec((1, LANES), lambda c: (0, 0)),
            pl.BlockSpec((1, LANES), lambda c: (0, 0)),
            pl.BlockSpec((1, GDN_HEAD_DIM), lambda c: (0, 0)),
        ],
        out_specs=pl.BlockSpec((bsz, cs, z_cols), lambda c: (0, c, 0)),
        out_shape=jax.ShapeDtypeStruct((bsz, seq, z_cols), BF16),
        scratch_shapes=[pltpu.VMEM((bsz * GDN_HEADS, GDN_HEAD_DIM, GDN_HEAD_DIM), F32),
                        pltpu.VMEM((bsz, cs + SUBLANES, qkv_cols), F32)],
        compiler_params=_params(("arbitrary",)), name="gated_deltanet",
    )(pc3, pc3, pc3, small3, conv_w, alog_pad, dtb_pad, norm_w.reshape(1, GDN_HEAD_DIM))
    return out.reshape(bsz * seq, z_cols)


def _merge_kernel(ya_ref, yb_ref, ma_ref, mb_ref, x_ref, mod_ref, wa_ref, wb_ref, wo_ref,
                  nf_ref, x1_ref, h2_ref):
    pa = jnp.dot(ya_ref[...], wa_ref[...], preferred_element_type=F32)
    pb = jnp.dot(yb_ref[...], wb_ref[...], preferred_element_type=F32)
    mixed = jax.nn.sigmoid(ma_ref[...]) * pa + jax.nn.sigmoid(mb_ref[...]) * pb
    mod = mod_ref[0]
    x1 = x_ref[...] + mod[GT_M:GT_M + 1, :] * jnp.dot(
        mixed.astype(BF16), wo_ref[...], preferred_element_type=F32)
    x1_ref[...] = x1
    h2_ref[...] = _modulated_norm(x1, nf_ref[...], mod, SC_F, SH_F).astype(BF16)


def _merge_project(ya, yb, pc, x2, mod, w_pa, w_pb, w_o, norm_ffn, seq):
    t, d = x2.shape
    tiles_per_batch = seq // ROW_TILE
    merge_blk = (3 * GDN_HEADS * GDN_HEAD_DIM + GDN_HEADS * GDN_HEAD_DIM) // d
    row = lambda i: (i, 0)
    const = lambda i: (0, 0)
    return pl.pallas_call(
        _merge_kernel,
        grid=(t // ROW_TILE,),
        in_specs=[pl.BlockSpec((ROW_TILE, d), row),
                  pl.BlockSpec((ROW_TILE, d), row),
                  pl.BlockSpec((ROW_TILE, d), lambda i: (i, merge_blk)),
                  pl.BlockSpec((ROW_TILE, d), lambda i: (i, merge_blk + 1)),
                  pl.BlockSpec((ROW_TILE, d), row),
                  pl.BlockSpec((1, 6, d), lambda i: (i // tiles_per_batch, 0, 0)),
                  pl.BlockSpec((d, d), const),
                  pl.BlockSpec((d, d), const),
                  pl.BlockSpec((d, d), const),
                  pl.BlockSpec((1, d), const)],
        out_specs=[pl.BlockSpec((ROW_TILE, d), row), pl.BlockSpec((ROW_TILE, d), row)],
        out_shape=[jax.ShapeDtypeStruct((t, d), F32), jax.ShapeDtypeStruct((t, d), BF16)],
        compiler_params=_params(("parallel",)), name="merge_project",
    )(ya, yb, pc, pc, x2, mod, w_pa, w_pb, w_o, norm_ffn.reshape(1, d))


def _ffn_kernel(h_ref, halo_ref, x1_ref, mod_ref, wg_ref, wv_ref, cwg_ref, cwv_ref,
                cbg_ref, cbv_ref, wd_ref, nfin_ref, o_ref, acc_ref, ug_ref, uv_ref,
                *, tiles_per_batch):
    i = pl.program_id(0)
    f = pl.program_id(1)
    tm = h_ref.shape[0]
    first_of_batch = (i % tiles_per_batch) == 0

    @pl.when(f == 0)
    def _():
        acc_ref[...] = jnp.zeros_like(acc_ref)

    h = h_ref[...]
    halo = halo_ref[...]

    def conv_half(w_ref, cw_ref, cb_ref, u_ref):
        u_ref[SUBLANES:, :] = jnp.dot(h, w_ref[...], preferred_element_type=F32)
        up_halo = jnp.dot(halo, w_ref[...], preferred_element_type=F32)[SUBLANES:]
        u_ref[0:SUBLANES, :] = jnp.where(first_of_batch, 0.0, up_halo)
        cw = cw_ref[...]
        out = cb_ref[...] + cw[FFN_CONV - 1:FFN_CONV, :] * u_ref[SUBLANES:, :]
        for j in range(FFN_CONV - 1):
            off = SUBLANES - (FFN_CONV - 1) + j
            out = out + cw[j:j + 1, :] * u_ref[off:off + tm, :]
        return out

    gate = conv_half(wg_ref, cwg_ref, cbg_ref, ug_ref)
    val = conv_half(wv_ref, cwv_ref, cbv_ref, uv_ref)
    act = gate * jax.nn.sigmoid(gate) * val
    acc_ref[...] += jnp.dot(act.astype(BF16), wd_ref[...], preferred_element_type=F32)

    @pl.when(f == pl.num_programs(1) - 1)
    def _():
        mod = mod_ref[0]
        x2 = x1_ref[...] + mod[GT_F:GT_F + 1, :] * acc_ref[...]
        ms = jnp.mean(x2 * x2, axis=-1, keepdims=True)
        o_ref[...] = x2 * lax.rsqrt(ms + NORM_EPS) * nfin_ref[...]


def _conv_ffn(h2, x1, mod, w_up, conv_w, conv_b, w_down, norm_final, seq):
    t, d = x1.shape
    tm, tf = ROW_TILE, FFN_TF
    n_f = FFN_DIM // tf
    tiles_per_batch = seq // tm
    halo_per_tile = tm // SUBLANES
    row = lambda i, f: (i, 0)
    return pl.pallas_call(
        functools.partial(_ffn_kernel, tiles_per_batch=tiles_per_batch),
        grid=(t // tm, n_f),
        in_specs=[pl.BlockSpec((tm, d), row),
                  pl.BlockSpec((SUBLANES * 2, d),
                               lambda i, f: (jnp.maximum(i * (halo_per_tile // 2) - 1, 0), 0)),
                  pl.BlockSpec((tm, d), row),
                  pl.BlockSpec((1, 6, d), lambda i, f: (i // tiles_per_batch, 0, 0)),
                  pl.BlockSpec((d, tf), lambda i, f: (0, f)),
                  pl.BlockSpec((d, tf), lambda i, f: (0, n_f + f)),
                  pl.BlockSpec((FFN_CONV, tf), lambda i, f: (0, f)),
                  pl.BlockSpec((FFN_CONV, tf), lambda i, f: (0, n_f + f)),
                  pl.BlockSpec((1, tf), lambda i, f: (0, f)),
                  pl.BlockSpec((1, tf), lambda i, f: (0, n_f + f)),
                  pl.BlockSpec((tf, d), lambda i, f: (f, 0)),
                  pl.BlockSpec((1, d), lambda i, f: (0, 0))],
        out_specs=pl.BlockSpec((tm, d), row),
        out_shape=jax.ShapeDtypeStruct((t, d), F32),
        scratch_shapes=[pltpu.VMEM((tm, d), F32),
                        pltpu.VMEM((tm + SUBLANES, tf), F32),
                        pltpu.VMEM((tm + SUBLANES, tf), F32)],
        compiler_params=_params(("parallel", "arbitrary")), name="conv_ffn",
    )(h2, h2, x1, mod, w_up, w_up, conv_w, conv_w, conv_b, conv_b, w_down,
      norm_final.reshape(1, d))


def _overlap_t(seq):
    n_cmp = (seq - CMP_BLK) // CMP_STRIDE + 1
    n_blk = seq // SEL_BLK
    c_start = np.arange(n_cmp) * CMP_STRIDE
    s_start = np.arange(n_blk) * SEL_BLK
    ov = np.clip(np.minimum(c_start[:, None] + CMP_BLK, s_start[None, :] + SEL_BLK)
                 - np.maximum(c_start[:, None], s_start[None, :]), 0, None) / CMP_BLK
    n_half = seq // CMP_STRIDE
    out = np.zeros((n_blk, n_half), np.float32)
    out[:, :n_cmp] = ov.T
    return jnp.asarray(out, BF16)


def kernel(x, c, w_ada, b_ada, norm_mix, w_in, nsa_pos_k, nsa_pos_v, nsa_ck_w1, nsa_ck_b1, nsa_ck_w2, nsa_ck_b2, nsa_cv_w1, nsa_cv_b1, nsa_cv_w2, nsa_cv_b2, gdn_conv, gdn_a_log, gdn_dt_bias, gdn_norm, w_proj_nsa, w_proj_gdn, w_out, norm_ffn, ffn_up, ffn_conv, ffn_conv_b, ffn_down, norm_final):
    bsz, seq, d = x.shape
    depth = w_ada.shape[0]
    n_grp, dk = NSA_KV_HEADS, NSA_HEAD_DIM
    q_cols = NSA_HEADS * dk
    gqkv_cols = 3 * GDN_HEADS * GDN_HEAD_DIM
    z_cols = GDN_HEADS * GDN_HEAD_DIM
    o_kv = q_cols
    o_gl = o_kv + 6 * n_grp * dk
    o_gq = o_gl + 3 * NSA_HEADS
    o_a = o_gq + gqkv_cols
    o_z = o_a + 2 * GDN_HEADS
    overlap_t = _overlap_t(seq)
    x2 = x.reshape(bsz * seq, d)
    assert depth == 1, "kernel supports the problem's DEPTH == 1"
    for l in range(depth):
        mod = _adaln_mod(c, w_ada[l], b_ada[l])
        w = w_in[l]
        kv_w = w[:, o_kv:o_gl].reshape(d, 6, n_grp, dk)
        zero = jnp.zeros((d, n_grp, dk), F32)
        pair = lambda a, b: jnp.concatenate([a, b], axis=2).reshape(d, n_grp * 2 * dk)
        w_att = jnp.concatenate([w[:, :o_kv] * (dk ** -0.5 * LOG2_E), pair(kv_w[:, 2], zero),
                                 pair(kv_w[:, 4], zero), pair(kv_w[:, 3], kv_w[:, 5])],
                                axis=1).astype(BF16)
        gate_w = jnp.pad(w[:, o_gl:o_gq].reshape(d, n_grp, 3 * NSA_REP),
                         ((0, 0), (0, 0), (0, NSA_AUX - 3 * NSA_REP))).reshape(d, SMALL_A)
        w_small = jnp.concatenate([gate_w, w[:, o_a:o_z],
                                   jnp.zeros((d, LANES - SMALL_B - GDN_HEADS), F32)],
                                  axis=1).astype(BF16)
        w_rest = jnp.concatenate([w[:, o_gq:o_a], w[:, o_z:], pair(kv_w[:, 0], kv_w[:, 1])],
                                 axis=1).astype(BF16)
        cmp_block = (gqkv_cols + z_cols + 2 * d) // LANES

        p_att = _norm_mod_matmul(x2, norm_mix[l], mod, w_att, 1280, BF16, seq)
        p_small = _norm_mod_matmul(x2, norm_mix[l], mod, w_small, LANES, F32, seq)
        pc = _norm_mod_matmul(x2, norm_mix[l], mod, w_rest, 1664, F32, seq)
        pc3 = pc.reshape(bsz, seq, pc.shape[1])
        small3 = p_small.reshape(bsz, seq, LANES)

        w1k = nsa_ck_w1[l].reshape(2, CMP_STRIDE, dk, CMP_HIDDEN)
        w1v = nsa_cv_w1[l].reshape(2, CMP_STRIDE, dk, CMP_HIDDEN)
        zw = jnp.zeros((CMP_STRIDE, dk, 2 * CMP_HIDDEN), F32)
        wc = jnp.concatenate(
            [jnp.concatenate([w1k[0], w1k[1], zw], axis=2),
             jnp.concatenate([zw, w1v[0], w1v[1]], axis=2)], axis=1).astype(BF16)
        pos = jnp.stack([nsa_pos_k[l], nsa_pos_v[l]]).reshape(2, 1, CMP_BLK * dk)
        pos = jnp.broadcast_to(pos, (2, 2 * SUBLANES, CMP_BLK * dk)).astype(BF16)
        cw1 = jnp.stack([nsa_ck_w1[l], nsa_cv_w1[l]]).astype(BF16)
        cb1 = jnp.stack([nsa_ck_b1[l], nsa_cv_b1[l]]).reshape(2, 1, CMP_HIDDEN)
        cw2 = jnp.stack([nsa_ck_w2[l], nsa_cv_w2[l]]).astype(BF16)
        cb2 = jnp.stack([nsa_ck_b2[l], nsa_cv_b2[l]]).reshape(2, 1, dk)
        eye = jnp.eye(NSA_REP * dk, dtype=BF16)
        kc, vcT = _compress(pc3, wc, pos, cw1, cb1, cw2, cb2, eye, cmp_block)

        blk_in_tile = np.arange(NSA_TK) // SEL_BLK
        aux_s = np.zeros((NSA_TK, LANES), np.float32)
        aux_s[np.arange(NSA_TK), dk + blk_in_tile] = 1.0
        y_a = _nsa_attn(p_att.reshape(bsz, seq, p_att.shape[1]), small3, kc, vcT, overlap_t,
                        eye, jnp.asarray(aux_s, BF16)).reshape(bsz * seq, q_cols)

        alog_pad = jnp.zeros((1, LANES), F32).at[0, SMALL_A:SMALL_A + GDN_HEADS].set(gdn_a_log[l])
        dtb_pad = jnp.zeros((1, LANES), F32).at[0, SMALL_A:SMALL_A + GDN_HEADS].set(gdn_dt_bias[l])
        y_b = _gated_deltanet(pc, p_small, gdn_conv[l], alog_pad, dtb_pad, gdn_norm[l], bsz, seq)

        x1, h2 = _merge_project(y_a, y_b, pc, x2, mod, w_proj_nsa[l].astype(BF16),
                                w_proj_gdn[l].astype(BF16), w_out[l].astype(BF16),
                                norm_ffn[l], seq)
        out = _conv_ffn(h2, x1, mod, ffn_up[l].astype(BF16), ffn_conv[l],
                        ffn_conv_b[l].reshape(1, 2 * FFN_DIM), ffn_down[l].astype(BF16),
                        norm_final, seq)
    return out.reshape(bsz, seq, d)
```

```python
import functools

import numpy as np
import jax
import jax.numpy as jnp
from jax import lax
from jax.experimental import pallas as pl
from jax.experimental.pallas import tpu as pltpu

F32 = jnp.float32
BF16 = jnp.bfloat16
HIGHEST = lax.Precision.HIGHEST

NSA_HEAD_DIM = 64
NSA_HEADS = 16
NSA_KV_HEADS = 4
NSA_REP = NSA_HEADS // NSA_KV_HEADS
CMP_BLK = 32
CMP_STRIDE = 16
CMP_HIDDEN = 256
SEL_BLK = 64
N_SEL = 16
WINDOW = 512
GDN_HEAD_DIM = 128
GDN_HEADS = 8
GDN_CONV = 4
GDN_CHUNK = 64
FFN_DIM = 2816
FFN_CONV = 3
NORM_EPS = 1e-6
MASK_VALUE = -1e30

LANES = 128
SUBLANES = 8
VMEM_LIMIT = 56 * 1024 * 1024

NSA_TQ = 256
NSA_TK = 512
NSA_AUX = 16
LOG2_E = 1.4426950408889634
PROJ_TILE = 1024
ROW_TILE = 512
FFN_TF = 1408

SH_M, SC_M, GT_M, SH_F, SC_F, GT_F = range(6)
SMALL_A = NSA_KV_HEADS * NSA_AUX
SMALL_B = SMALL_A + GDN_HEADS


def _params(sem):
    return pltpu.CompilerParams(dimension_semantics=sem, vmem_limit_bytes=VMEM_LIMIT)


def _split3(a):
    hi = a.astype(BF16)
    r1 = a - hi.astype(F32)
    mid = r1.astype(BF16)
    lo = (r1 - mid.astype(F32)).astype(BF16)
    return hi, mid, lo


def _dot_exact_lhs(a_bf16, b, dims=None):
    if dims is None:
        dims = (((a_bf16.ndim - 1,), (0,)), ((), ()))
    out = None
    for part in _split3(b):
        term = lax.dot_general(a_bf16, part, dimension_numbers=dims, preferred_element_type=F32)
        out = term if out is None else out + term
    return out


NT_DIMS = (((1,), (1,)), ((), ()))
TN_DIMS = (((0,), (0,)), ((), ()))
NN_DIMS = (((1,), (0,)), ((), ()))


def _mod_kernel(c_ref, w_ref, b_ref, o_ref):
    o_ref[...] = jnp.dot(c_ref[...], w_ref[...], preferred_element_type=F32,
                         precision=HIGHEST) + b_ref[...]


def _adaln_mod(c, w_ada, b_ada):
    bsz, d = c.shape
    n = w_ada.shape[1]
    c8 = jnp.zeros((SUBLANES, d), F32).at[:bsz].set(c)
    out = pl.pallas_call(
        _mod_kernel,
        grid=(n // d,),
        in_specs=[pl.BlockSpec((SUBLANES, d), lambda j: (0, 0)),
                  pl.BlockSpec((d, d), lambda j: (0, j)),
                  pl.BlockSpec((1, d), lambda j: (0, j))],
        out_specs=pl.BlockSpec((SUBLANES, d), lambda j: (0, j)),
        out_shape=jax.ShapeDtypeStruct((SUBLANES, n), F32),
        compiler_params=_params(("arbitrary",)), name="adaln_mod",
    )(c8, w_ada, b_ada.reshape(1, n))
    return out[:bsz].reshape(bsz, n // d, d)


def _modulated_norm(x, norm_w, mod, sc_row, sh_row):
    ms = jnp.mean(x * x, axis=-1, keepdims=True)
    y = x * lax.rsqrt(ms + NORM_EPS) * norm_w
    return y * (1.0 + mod[sc_row:sc_row + 1, :]) + mod[sh_row:sh_row + 1, :]


def _nmm_kernel(x_ref, nw_ref, mod_ref, w_ref, o_ref, h_ref):
    @pl.when(pl.program_id(1) == 0)
    def _():
        h = _modulated_norm(x_ref[...], nw_ref[...], mod_ref[0], SC_M, SH_M)
        h_ref[...] = h.astype(BF16)

    o_ref[...] = jnp.dot(h_ref[...], w_ref[...],
                         preferred_element_type=F32).astype(o_ref.dtype)


def _norm_mod_matmul(x2, norm_w, mod, w, tn, out_dtype, seq):
    t, d = x2.shape
    n = w.shape[1]
    tiles_per_batch = seq // PROJ_TILE
    return pl.pallas_call(
        _nmm_kernel,
        grid=(t // PROJ_TILE, n // tn),
        in_specs=[pl.BlockSpec((PROJ_TILE, d), lambda i, j: (i, 0)),
                  pl.BlockSpec((1, d), lambda i, j: (0, 0)),
                  pl.BlockSpec((1, 6, d), lambda i, j: (i // tiles_per_batch, 0, 0)),
                  pl.BlockSpec((d, tn), lambda i, j: (0, j))],
        out_specs=[pl.BlockSpec((PROJ_TILE, tn), lambda i, j: (i, j)),
                   pl.BlockSpec((PROJ_TILE, d), lambda i, j: (i, 0))],
        out_shape=[jax.ShapeDtypeStruct((t, n), out_dtype), jax.ShapeDtypeStruct((t, d), BF16)],
        compiler_params=_params(("parallel", "arbitrary")), name="norm_mod_proj",
    )(x2, norm_w.reshape(1, d), mod, w)


def _mm_kernel(h_ref, w_ref, o_ref):
    o_ref[...] = jnp.dot(h_ref[...], w_ref[...], preferred_element_type=F32).astype(o_ref.dtype)


def _matmul(h, w, tn, out_dtype):
    t, d = h.shape
    n = w.shape[1]
    return pl.pallas_call(
        _mm_kernel,
        grid=(n // tn, t // PROJ_TILE),
        in_specs=[pl.BlockSpec((PROJ_TILE, d), lambda j, i: (i, 0)),
                  pl.BlockSpec((d, tn), lambda j, i: (0, j))],
        out_specs=pl.BlockSpec((PROJ_TILE, tn), lambda j, i: (i, j)),
        out_shape=jax.ShapeDtypeStruct((t, n), out_dtype),
        compiler_params=_params(("parallel", "parallel")), name="proj",
    )(h, w)


def _compress_kernel(t_ref, wc_ref, pos_ref, w1_ref, b1_ref, w2_ref, b2_ref, eye_ref,
                     kc_ref, vcT_ref):
    n_half = t_ref.shape[1] // CMP_STRIDE
    acc = None
    for l in range(CMP_STRIDE):
        tok = t_ref[0, pl.ds(l, n_half, stride=CMP_STRIDE), :].astype(BF16)
        term = jnp.dot(tok, wc_ref[l], preferred_element_type=F32)
        acc = term if acc is None else acc + term
    outs = []
    for kind in range(2):
        top = acc[:, 2 * kind * CMP_HIDDEN:(2 * kind + 1) * CMP_HIDDEN]
        bot = acc[:, (2 * kind + 1) * CMP_HIDDEN:(2 * kind + 2) * CMP_HIDDEN]
        bot_next = pltpu.roll(bot, n_half - 1, 0)
        pos_term = jnp.dot(pos_ref[kind], w1_ref[kind], preferred_element_type=F32)[0:1]
        hid = jax.nn.gelu(top + bot_next + pos_term + b1_ref[kind])
        outs.append(jnp.dot(hid.astype(BF16), w2_ref[kind], preferred_element_type=F32)
                    + b2_ref[kind])
    kc_ref[0, 0] = outs[0].astype(BF16)
    dk = NSA_HEAD_DIM
    vcT_ref[0, 0] = lax.dot_general(eye_ref[:dk, :dk], outs[1].astype(BF16), NT_DIMS,
                                    preferred_element_type=F32).astype(BF16)


def _compress(pc3, wc, pos, w1, b1, w2, b2, eye, first_col_block):
    bsz, seq, _ = pc3.shape
    n_grp, dk = NSA_KV_HEADS, NSA_HEAD_DIM
    n_half = seq // CMP_STRIDE
    const3 = lambda b, g: (0, 0, 0)
    return pl.pallas_call(
        _compress_kernel,
        grid=(bsz, n_grp),
        in_specs=[pl.BlockSpec((1, seq, LANES), lambda b, g: (b, 0, first_col_block + g)),
                  pl.BlockSpec(wc.shape, const3),
                  pl.BlockSpec(pos.shape, const3),
                  pl.BlockSpec(w1.shape, const3),
                  pl.BlockSpec(b1.shape, const3),
                  pl.BlockSpec(w2.shape, const3),
                  pl.BlockSpec(b2.shape, const3),
                  pl.BlockSpec(eye.shape, lambda b, g: (0, 0))],
        out_specs=[pl.BlockSpec((1, 1, n_half, dk), lambda b, g: (b, g, 0, 0)),
                   pl.BlockSpec((1, 1, dk, n_half), lambda b, g: (b, g, 0, 0))],
        out_shape=[jax.ShapeDtypeStruct((bsz, n_grp, n_half, dk), BF16),
                   jax.ShapeDtypeStruct((bsz, n_grp, dk, n_half), BF16)],
        compiler_params=_params(("parallel", "parallel")), name="nsa_compress",
    )(pc3, wc, pos, w1, b1, w2, b2, eye)


def _nsa_attn_kernel(q_ref, ks_ref, kw_ref, v_ref, kc_ref, vcT_ref, ov_ref, gl_ref, eye_ref,
                     aux_ref, o_ref,
                     ksa_ref, kwa_ref, vsT_ref, vwT_ref, sel_ref, s_ref, m_ref, acc_ref,
                     sw_ref, mw_ref, accw_ref, part_ref, gate_ref):
    tq, tk = NSA_TQ, NSA_TK
    rows = NSA_REP * tq
    dk = NSA_HEAD_DIM
    grp = pl.program_id(1)
    qt = pl.program_id(2)
    qs = qt * tq
    seq = ks_ref.shape[1]
    eye = eye_ref[...]
    tq_per_tk = tk // tq

    @pl.when(qt == 0)
    def _():
        lane = lax.broadcasted_iota(jnp.int32, (tk, LANES), 1)
        lane_flag = jnp.where(lane == dk, 1.0, 0.0).astype(BF16)
        ones_rows = jnp.ones((NSA_AUX, tk), BF16)

        def build(j, carry):
            r0 = pl.multiple_of(j * tk, tk)
            ksa_ref[pl.ds(r0, tk), :] = ks_ref[0, pl.ds(r0, tk), :] + aux_ref[...]
            kwa_ref[pl.ds(r0, tk), :] = kw_ref[0, pl.ds(r0, tk), :] + lane_flag
            vT = lax.dot_general(eye[:LANES, :LANES], v_ref[0, pl.ds(r0, tk), :], NT_DIMS,
                                 preferred_element_type=F32).astype(BF16)
            vsT_ref[j, 0:dk, :] = vT[:dk]
            vsT_ref[j, dk:, :] = ones_rows
            for i in range(tq_per_tk):
                vwT_ref[j * tq_per_tk + i, 0:dk, :] = vT[dk:, i * tq:(i + 1) * tq]
                vwT_ref[j * tq_per_tk + i, dk:, :] = ones_rows[:, :tq]
            return carry

        lax.fori_loop(0, seq // tk, build, 0)

    q4 = lax.dot_general(eye, q_ref[0], NT_DIMS, preferred_element_type=F32).astype(BF16)
    qT = jnp.concatenate([q4[r * dk:(r + 1) * dk, :] for r in range(NSA_REP)], axis=1)
    t_q = qs + lax.broadcasted_iota(jnp.int32, (1, tq), 1)
    t_row = jnp.concatenate([t_q] * NSA_REP, axis=1)
    zpad = jnp.zeros((LANES - dk - NSA_AUX, rows), BF16)

    def per_head(a):
        return jnp.concatenate([a] * NSA_REP, axis=1)

    gate_ref[...] = _dot_exact_lhs(eye[:LANES, :LANES], jax.nn.sigmoid(gl_ref[0]), NT_DIMS)
    gates = gate_ref[pl.ds(pl.multiple_of(grp * NSA_AUX, NSA_AUX), NSA_AUX), :]

    n_cmp = kc_ref.shape[2]
    s_c = jnp.dot(kc_ref[0, 0], qT, preferred_element_type=F32)
    c_end = lax.broadcasted_iota(jnp.int32, (n_cmp, 1), 0) * CMP_STRIDE + (CMP_BLK - 1)
    s_c = s_c + per_head(jnp.where(c_end <= t_q, 0.0, MASK_VALUE))
    mx_c = jnp.max(s_c, axis=0, keepdims=True)
    p_c = jnp.exp2(s_c - mx_c)
    l_c = jnp.sum(p_c, axis=0, keepdims=True)
    p_c = p_c * jnp.where(t_row >= CMP_BLK - 1, 1.0 / l_c, 0.0)
    o_c = jnp.dot(vcT_ref[0, 0], p_c.astype(BF16), preferred_element_type=F32)
    for r in range(NSA_REP):
        part_ref[r * dk:(r + 1) * dk, :] = gates[3 * r:3 * r + 1] * o_c[:, r * tq:(r + 1) * tq]

    p_sum = p_c[:, 0:tq]
    for r in range(1, NSA_REP):
        p_sum = p_sum + p_c[:, r * tq:(r + 1) * tq]
    ov = ov_ref[...]
    imp = _dot_exact_lhs(ov, p_sum)
    n_blk = ov.shape[0]
    j_blk = lax.broadcasted_iota(jnp.int32, (n_blk, 1), 0)
    cur = t_q // SEL_BLK
    forced = (j_blk == 0) | (j_blk == cur) | (j_blk == cur - 1)
    valid = j_blk * SEL_BLK <= t_q
    n_rank = N_SEL - 3
    cand = jnp.where(forced, -jnp.inf, jnp.where(valid, imp, -1.0))

    n_wt = WINDOW // tq + 1
    row0 = lax.broadcasted_iota(jnp.int32, (NSA_AUX, rows), 0) == 0
    a_k = lax.broadcasted_iota(jnp.int32, (tq, 1), 0)
    b_q = lax.broadcasted_iota(jnp.int32, (1, tq), 1)
    w_tiles = [jnp.maximum(qt - (n_wt - 1) + i, 0) for i in range(n_wt)]

    def win_scores(i):
        before_start = qt - (n_wt - 1) + i < 0
        flag = jnp.where(row0 & before_start, MASK_VALUE, 0.0).astype(BF16)
        rhs_w = jnp.concatenate([qT, flag, zpad], axis=0)
        k_tile = kwa_ref[pl.ds(pl.multiple_of(w_tiles[i] * tq, tq), tq), :]
        s = jnp.dot(k_tile, rhs_w, preferred_element_type=F32)
        if i == 0:
            s = s + per_head(jnp.where(a_k > b_q, 0.0, MASK_VALUE))
        if i == n_wt - 1:
            s = s + per_head(jnp.where(a_k <= b_q, 0.0, MASK_VALUE))
        sw_ref[i * tq:(i + 1) * tq, :] = s

    def win_max():
        mw_ref[...] = jnp.max(sw_ref[...], axis=0, keepdims=True)

    def win_pv(i):
        p = jnp.exp2(sw_ref[i * tq:(i + 1) * tq, :] - mw_ref[...]).astype(BF16)
        pv = jnp.dot(vwT_ref[w_tiles[i]], p, preferred_element_type=F32)
        if i == 0:
            accw_ref[...] = pv
        else:
            accw_ref[...] = accw_ref[...] + pv

    win_steps = ([functools.partial(win_scores, i) for i in range(n_wt)] + [win_max]
                 + [functools.partial(win_pv, i) for i in range(n_wt)])

    work = cand
    maxima = []
    for r in range(n_rank):
        mx = jnp.max(work, axis=0, keepdims=True)
        maxima.append(mx)
        work = jnp.where(work == mx, -jnp.inf, work)
        if r < len(win_steps):
            win_steps[r]()
    for step in win_steps[n_rank:]:
        step()

    ones_lhs = jnp.ones((SUBLANES, n_blk), BF16)

    def count(mask):
        return jnp.dot(ones_lhs, jnp.where(mask, 1.0, 0.0).astype(BF16),
                       preferred_element_type=F32)[0:1]

    thr = maxima[-1]
    for mx in reversed(maxima[:-1]):
        thr = jnp.where(count(cand >= mx) >= n_rank, mx, thr)
    ties = cand == thr
    below = (lax.broadcasted_iota(jnp.int32, (n_blk, n_blk), 1)
             < lax.broadcasted_iota(jnp.int32, (n_blk, n_blk), 0))
    tie_rank = jnp.dot(jnp.where(below, 1.0, 0.0).astype(BF16),
                       jnp.where(ties, 1.0, 0.0).astype(BF16), preferred_element_type=F32)
    picked = (cand > thr) | (ties & (tie_rank < n_rank - count(cand > thr)))
    sel_ref[0:n_blk, :] = jnp.where(forced | (picked & (cand >= 0.0)), 0.0, MASK_VALUE)
    sel_ref[n_blk:, :] = jnp.zeros((NSA_AUX, tq), F32)

    acc_w = accw_ref[...]
    o_w = acc_w[:dk] * (1.0 / acc_w[dk:dk + 1])
    for r in range(NSA_REP):
        part_ref[r * dk:(r + 1) * dk, :] = (part_ref[r * dk:(r + 1) * dk, :]
                                            + gates[3 * r + 2:3 * r + 3] * o_w[:, r * tq:(r + 1) * tq])

    blk_per_tile = tk // SEL_BLK
    n_full = qs // tk

    def scores(kt, slot, diagonal=False):
        sb = sel_ref[pl.ds(pl.multiple_of(kt * blk_per_tile, blk_per_tile), NSA_AUX), :]
        rhs = jnp.concatenate([qT, per_head(sb).astype(BF16), zpad], axis=0)
        k_tile = ksa_ref[pl.ds(pl.multiple_of(kt * tk, tk), tk), :]
        s = jnp.dot(k_tile, rhs, preferred_element_type=F32)
        if diagonal:
            kpos = kt * tk + lax.broadcasted_iota(jnp.int32, (tk, 1), 0)
            s = s + per_head(jnp.where(kpos <= t_q, 0.0, MASK_VALUE))
        s_ref[slot] = s

    def softmax_pv(kt, slot):
        m_i = m_ref[...]
        m_new = jnp.maximum(m_i, jnp.max(s_ref[slot], axis=0, keepdims=True))
        alpha = jnp.exp2(m_i - m_new)
        p = jnp.exp2(s_ref[slot] - m_new).astype(BF16)
        pv = jnp.dot(vsT_ref[kt], p, preferred_element_type=F32)
        acc_ref[...] = alpha * acc_ref[...] + pv
        m_ref[...] = m_new

    m_ref[...] = jnp.full((1, rows), MASK_VALUE, F32)
    acc_ref[...] = jnp.zeros((dk + NSA_AUX, rows), F32)
    scores(n_full, 0, diagonal=True)

    n_pairs = n_full // 2

    def slot0_tile(i):
        return jnp.where(i == 0, n_full, 2 * i - 1)

    def pair_step(i, carry):
        scores(2 * i, 1)
        softmax_pv(slot0_tile(i), 0)
        scores(2 * i + 1, 0)
        softmax_pv(2 * i, 1)
        return carry

    lax.fori_loop(0, n_pairs, pair_step, 0)

    @pl.when(n_full % 2 == 1)
    def _():
        scores(2 * n_pairs, 1)
        softmax_pv(slot0_tile(n_pairs), 0)
        softmax_pv(2 * n_pairs, 1)

    @pl.when(n_full % 2 == 0)
    def _():
        softmax_pv(slot0_tile(n_pairs), 0)

    acc_s = acc_ref[...]
    o_s = acc_s[:dk] * (1.0 / acc_s[dk:dk + 1])
    gates = gate_ref[pl.ds(pl.multiple_of(grp * NSA_AUX, NSA_AUX), NSA_AUX), :]
    out_t = jnp.concatenate(
        [part_ref[r * dk:(r + 1) * dk, :] + gates[3 * r + 1:3 * r + 2] * o_s[:, r * tq:(r + 1) * tq]
         for r in range(NSA_REP)], axis=0).astype(BF16)
    o_ref[0] = lax.dot_general(eye[:tq, :tq], out_t, NT_DIMS,
                               preferred_element_type=F32).astype(o_ref.dtype)


def _nsa_attn(p_att3, small3, kc, vcT, overlap_t, eye, aux_s):
    bsz, seq, _ = p_att3.shape
    n_grp, dk = NSA_KV_HEADS, NSA_HEAD_DIM
    tq, tk = NSA_TQ, NSA_TK
    rows = NSA_REP * tq
    q_w = NSA_REP * dk
    n_cmp = kc.shape[2]
    n_blk = seq // SEL_BLK
    v_rows = dk + NSA_AUX
    n_wt = WINDOW // tq + 1
    first = NSA_HEADS * dk // LANES
    kv_spec = lambda off: pl.BlockSpec((1, seq, LANES), lambda b, g, t: (b, 0, first + off + g))
    return pl.pallas_call(
        _nsa_attn_kernel,
        grid=(bsz, n_grp, seq // tq),
        in_specs=[
            pl.BlockSpec((1, tq, q_w), lambda b, g, t: (b, t, g)),
            kv_spec(0), kv_spec(n_grp), kv_spec(2 * n_grp),
            pl.BlockSpec((1, 1, n_cmp, dk), lambda b, g, t: (b, g, 0, 0)),
            pl.BlockSpec((1, 1, dk, n_cmp), lambda b, g, t: (b, g, 0, 0)),
            pl.BlockSpec((n_blk, n_cmp), lambda b, g, t: (0, 0)),
            pl.BlockSpec((1, tq, LANES), lambda b, g, t: (b, t, 0)),
            pl.BlockSpec(eye.shape, lambda b, g, t: (0, 0)),
            pl.BlockSpec((tk, LANES), lambda b, g, t: (0, 0)),
        ],
        out_specs=pl.BlockSpec((1, tq, q_w), lambda b, g, t: (b, t, g)),
        out_shape=jax.ShapeDtypeStruct((bsz, seq, NSA_HEADS * dk), BF16),
        scratch_shapes=[pltpu.VMEM((seq, LANES), BF16),
                        pltpu.VMEM((seq, LANES), BF16),
                        pltpu.VMEM((seq // tk, v_rows, tk), BF16),
                        pltpu.VMEM((seq // tq, v_rows, tq), BF16),
                        pltpu.VMEM((n_blk + NSA_AUX, tq), F32),
                        pltpu.VMEM((2, tk, rows), F32),
                        pltpu.VMEM((1, rows), F32),
                        pltpu.VMEM((v_rows, rows), F32),
                        pltpu.VMEM((n_wt * tq, rows), F32),
                        pltpu.VMEM((1, rows), F32),
                        pltpu.VMEM((v_rows, rows), F32),
                        pltpu.VMEM((q_w, tq), F32),
                        pltpu.VMEM((LANES, tq), F32)],
        compiler_params=_params(("parallel", "parallel", "arbitrary")), name="nsa_attention",
    )(p_att3, p_att3, p_att3, p_att3, kc, vcT, overlap_t, small3, eye, aux_s)


def _bmm(a, b, dims):
    a = a.astype(BF16)
    b = b.astype(BF16)
    return jnp.stack([lax.dot_general(a[n], b[n], dims, preferred_element_type=F32)
                      for n in range(a.shape[0])])


def _gdn_kernel(x_ref, halo_ref, z_ref, small_ref, cw_ref, alog_ref, dtb_ref, nw_ref,
                o_ref, state_ref, xc_ref):
    bsz = x_ref.shape[0]
    cs = GDN_CHUNK
    hd = GDN_HEAD_DIM
    nh = GDN_HEADS
    c = pl.program_id(0)

    @pl.when(c == 0)
    def _():
        state_ref[...] = jnp.zeros_like(state_ref)

    xc_ref[:, 0:SUBLANES, :] = jnp.where(c > 0, halo_ref[...], 0.0)
    xc_ref[:, SUBLANES:, :] = x_ref[...]
    cw = cw_ref[...]
    y = cw[GDN_CONV - 1:GDN_CONV, :] * x_ref[...]
    for i in range(GDN_CONV - 1):
        off = SUBLANES - (GDN_CONV - 1) + i
        y = y + cw[i:i + 1, :] * xc_ref[:, off:off + cs, :]
    y = y * jax.nn.sigmoid(y)

    small = small_ref[...]
    sp_in = small + dtb_ref[...]
    softplus = jnp.maximum(sp_in, 0.0) + jnp.log1p(jnp.exp(-jnp.abs(sp_in)))
    g_all = -jnp.exp(alog_ref[...]) * softplus
    beta_all = jax.nn.sigmoid(small)
    ii = lax.broadcasted_iota(jnp.int32, (cs, cs), 0)
    jj = lax.broadcasted_iota(jnp.int32, (cs, cs), 1)
    incl = ii >= jj
    strict = ii > jj
    tril = jnp.where(incl, 1.0, 0.0).astype(BF16)
    e128 = jnp.where(lax.broadcasted_iota(jnp.int32, (LANES, LANES), 0)
                     == lax.broadcasted_iota(jnp.int32, (LANES, LANES), 1), 1.0, 0.0).astype(BF16)
    gc_all = [_dot_exact_lhs(tril, g_all[b]) for b in range(bsz)]
    gcT_all = [_dot_exact_lhs(e128, gc_all[b], NT_DIMS) for b in range(bsz)]

    chains = [(b, h) for b in range(bsz) for h in range(nh)]

    def heads(a, base):
        return jnp.stack([a[b][:, base + h * hd:base + (h + 1) * hd] for b, h in chains])

    def lane_bcast(cols, base):
        return jnp.stack([jnp.broadcast_to(cols[b][:, base + h:base + h + 1], (cs, hd))
                          for b, h in chains])

    n_qk = nh * hd
    q = heads(y, 0)
    k = heads(y, n_qk)
    v = heads(y, 2 * n_qk)
    q = q * lax.rsqrt(jnp.sum(q * q, axis=-1, keepdims=True) + NORM_EPS) * (hd ** -0.5)
    k = k * lax.rsqrt(jnp.sum(k * k, axis=-1, keepdims=True) + NORM_EPS)
    gc = lane_bcast(gc_all, SMALL_A)
    beta = lane_bcast([beta_all[b] for b in range(bsz)], SMALL_B)
    gc_row = jnp.stack([gcT_all[b][SMALL_A + h:SMALL_A + h + 1, :] for b, h in chains])

    decay = jnp.exp(jnp.where(incl, gc[:, :, :cs] - gc_row, MASK_VALUE))
    kk = _bmm(k, k, NT_DIMS)
    m_full = jnp.where(strict, -(beta[:, :, :cs] * kk * decay), 0.0)
    blk_diff = ii ^ jj
    inv_m1 = jnp.where(blk_diff < 2, m_full, 0.0)
    size = 2
    while size < cs:
        m_off = jnp.where((blk_diff >= size) & (blk_diff < 2 * size), m_full, 0.0)
        t_mat = m_off + _bmm(inv_m1, m_off, NN_DIMS)
        inv_m1 = inv_m1 + t_mat + _bmm(t_mat, inv_m1, NN_DIMS)
        size *= 2
    e_gc = jnp.exp(gc)
    rhs = jnp.concatenate([v * beta, k * (beta * e_gc)], axis=2)
    sol = rhs + _bmm(inv_m1, rhs, NN_DIMS)
    u, w = sol[:, :, :hd], sol[:, :, hd:]
    qk = _bmm(q, k, NT_DIMS) * decay
    q_dec = q * e_gc
    gc_last = gc[:, cs - 1:cs, :]
    k_dec = k * jnp.exp(gc_last - gc)
    state = state_ref[...]
    v_new = u - _bmm(w, state, NN_DIMS)
    o = _bmm(q_dec, state, NN_DIMS) + _bmm(qk, v_new, NN_DIMS)
    state_ref[...] = state * jnp.exp(gc_last) + _bmm(k_dec, v_new, TN_DIMS)

    o = o * lax.rsqrt(jnp.mean(o * o, axis=-1, keepdims=True) + NORM_EPS) * nw_ref[...]
    z = heads(z_ref[...], 0)
    o = o * (z * jax.nn.sigmoid(z))
    for b in range(bsz):
        o_ref[b] = jnp.concatenate([o[b * nh + h] for h in range(nh)], axis=1).astype(o_ref.dtype)


def _gated_deltanet(pc, small, conv_w, alog_pad, dtb_pad, norm_w, bsz, seq):
    cs = GDN_CHUNK
    qkv_cols = 3 * GDN_HEADS * GDN_HEAD_DIM
    z_cols = GDN_HEADS * GDN_HEAD_DIM
    halo_per_chunk = cs // SUBLANES
    pc3 = pc.reshape(bsz, seq, pc.shape[1])
    small3 = small.reshape(bsz, seq, LANES)
    out = pl.pallas_call(
        _gdn_kernel,
        grid=(seq // cs,),
        in_specs=[
            pl.BlockSpec((bsz, cs, qkv_cols), lambda c: (0, c, 0)),
            pl.BlockSpec((bsz, SUBLANES, qkv_cols),
                         lambda c: (0, jnp.maximum(c * halo_per_chunk - 1, 0), 0)),
            pl.BlockSpec((bsz, cs, z_cols), lambda c: (0, c, qkv_cols // z_cols)),
            pl.BlockSpec((bsz, cs, LANES), lambda c: (0, c, 0)),
            pl.BlockSpec((GDN_CONV, qkv_cols), lambda c: (0, 0)),
            pl.BlockSpec((1, LANES), lambda c: (0, 0)),
            pl.BlockSpec((1, LANES), lambda c: (0, 0)),
            pl.BlockSpec((1, GDN_HEAD_DIM), lambda c: (0, 0)),
        ],
        out_specs=pl.BlockSpec((bsz, cs, z_cols), lambda c: (0, c, 0)),
        out_shape=jax.ShapeDtypeStruct((bsz, seq, z_cols), BF16),
        scratch_shapes=[pltpu.VMEM((bsz * GDN_HEADS, GDN_HEAD_DIM, GDN_HEAD_DIM), F32),
                        pltpu.VMEM((bsz, cs + SUBLANES, qkv_cols), F32)],
        compiler_params=_params(("arbitrary",)), name="gated_deltanet",
    )(pc3, pc3, pc3, small3, conv_w, alog_pad, dtb_pad, norm_w.reshape(1, GDN_HEAD_DIM))
    return out.reshape(bsz * seq, z_cols)


def _merge_kernel(ya_ref, yb_ref, ma_ref, mb_ref, x_ref, mod_ref, wa_ref, wb_ref, wo_ref,
                  nf_ref, x1_ref, h2_ref):
    pa = jnp.dot(ya_ref[...], wa_ref[...], preferred_element_type=F32)
    pb = jnp.dot(yb_ref[...], wb_ref[...], preferred_element_type=F32)
    mixed = jax.nn.sigmoid(ma_ref[...]) * pa + jax.nn.sigmoid(mb_ref[...]) * pb
    mod = mod_ref[0]
    x1 = x_ref[...] + mod[GT_M:GT_M + 1, :] * jnp.dot(
        mixed.astype(BF16), wo_ref[...], preferred_element_type=F32)
    x1_ref[...] = x1
    h2_ref[...] = _modulated_norm(x1, nf_ref[...], mod, SC_F, SH_F).astype(BF16)


def _merge_project(ya, yb, pc, x2, mod, w_pa, w_pb, w_o, norm_ffn, seq):
    t, d = x2.shape
    tiles_per_batch = seq // ROW_TILE
    merge_blk = (3 * GDN_HEADS * GDN_HEAD_DIM + GDN_HEADS * GDN_HEAD_DIM) // d
    row = lambda i: (i, 0)
    const = lambda i: (0, 0)
    return pl.pallas_call(
        _merge_kernel,
        grid=(t // ROW_TILE,),
        in_specs=[pl.BlockSpec((ROW_TILE, d), row),
                  pl.BlockSpec((ROW_TILE, d), row),
                  pl.BlockSpec((ROW_TILE, d), lambda i: (i, merge_blk)),
                  pl.BlockSpec((ROW_TILE, d), lambda i: (i, merge_blk + 1)),
                  pl.BlockSpec((ROW_TILE, d), row),
                  pl.BlockSpec((1, 6, d), lambda i: (i // tiles_per_batch, 0, 0)),
                  pl.BlockSpec((d, d), const),
                  pl.BlockSpec((d, d), const),
                  pl.BlockSpec((d, d), const),
                  pl.BlockSpec((1, d), const)],
        out_specs=[pl.BlockSpec((ROW_TILE, d), row), pl.BlockSpec((ROW_TILE, d), row)],
        out_shape=[jax.ShapeDtypeStruct((t, d), F32), jax.ShapeDtypeStruct((t, d), BF16)],
        compiler_params=_params(("parallel",)), name="merge_project",
    )(ya, yb, pc, pc, x2, mod, w_pa, w_pb, w_o, norm_ffn.reshape(1, d))


def _ffn_kernel(h_ref, halo_ref, res_ref, mod_ref, wg_ref, wv_ref, cwg_ref, cwv_ref,
                cbg_ref, cbv_ref, wd_ref, nfin_ref, o_ref, ug_ref, uv_ref,
                *, tiles_per_batch, last):
    i = pl.program_id(0)
    tm = h_ref.shape[0]
    first_of_batch = (i % tiles_per_batch) == 0
    h = h_ref[...]
    halo = halo_ref[...]

    def conv_half(w_ref, cw_ref, cb_ref, u_ref):
        u_ref[SUBLANES:, :] = jnp.dot(h, w_ref[...], preferred_element_type=F32)
        up_halo = jnp.dot(halo, w_ref[...], preferred_element_type=F32)[SUBLANES:]
        u_ref[0:SUBLANES, :] = jnp.where(first_of_batch, 0.0, up_halo)
        cw = cw_ref[...]
        out = cb_ref[...] + cw[FFN_CONV - 1:FFN_CONV, :] * u_ref[SUBLANES:, :]
        for j in range(FFN_CONV - 1):
            off = SUBLANES - (FFN_CONV - 1) + j
            out = out + cw[j:j + 1, :] * u_ref[off:off + tm, :]
        return out

    gate = conv_half(wg_ref, cwg_ref, cbg_ref, ug_ref)
    val = conv_half(wv_ref, cwv_ref, cbv_ref, uv_ref)
    act = gate * jax.nn.sigmoid(gate) * val
    down = jnp.dot(act.astype(BF16), wd_ref[...], preferred_element_type=F32)
    y = res_ref[...] + mod_ref[0][GT_F:GT_F + 1, :] * down
    if last:
        ms = jnp.mean(y * y, axis=-1, keepdims=True)
        y = y * lax.rsqrt(ms + NORM_EPS) * nfin_ref[...]
    o_ref[...] = y


def _conv_ffn(h2, x1, mod, w_up, conv_w, conv_b, w_down, norm_final, seq):
    t, d = x1.shape
    tm, tf = ROW_TILE, FFN_TF
    n_f = FFN_DIM // tf
    tiles_per_batch = seq // tm
    halo_per_tile = tm // SUBLANES
    row = lambda i: (i, 0)
    res = x1
    for f in range(n_f):
        col = lambda i, c=f: (0, c)
        col_val = lambda i, c=n_f + f: (0, c)
        res = pl.pallas_call(
            functools.partial(_ffn_kernel, tiles_per_batch=tiles_per_batch, last=f == n_f - 1),
            grid=(t // tm,),
            in_specs=[pl.BlockSpec((tm, d), row),
                      pl.BlockSpec((SUBLANES * 2, d),
                                   lambda i: (jnp.maximum(i * (halo_per_tile // 2) - 1, 0), 0)),
                      pl.BlockSpec((tm, d), row),
                      pl.BlockSpec((1, 6, d), lambda i: (i // tiles_per_batch, 0, 0)),
                      pl.BlockSpec((d, tf), col),
                      pl.BlockSpec((d, tf), col_val),
                      pl.BlockSpec((FFN_CONV, tf), col),
                      pl.BlockSpec((FFN_CONV, tf), col_val),
                      pl.BlockSpec((1, tf), col),
                      pl.BlockSpec((1, tf), col_val),
                      pl.BlockSpec((tf, d), lambda i, c=f: (c, 0)),
                      pl.BlockSpec((1, d), lambda i: (0, 0))],
            out_specs=pl.BlockSpec((tm, d), row),
            out_shape=jax.ShapeDtypeStruct((t, d), F32),
            scratch_shapes=[pltpu.VMEM((tm + SUBLANES, tf), F32),
                            pltpu.VMEM((tm + SUBLANES, tf), F32)],
            compiler_params=_params(("parallel",)), name="conv_ffn",
        )(h2, h2, res, mod, w_up, w_up, conv_w, conv_w, conv_b, conv_b, w_down,
          norm_final.reshape(1, d))
    return res


def _overlap_t(seq):
    n_cmp = (seq - CMP_BLK) // CMP_STRIDE + 1
    n_blk = seq // SEL_BLK
    c_start = np.arange(n_cmp) * CMP_STRIDE
    s_start = np.arange(n_blk) * SEL_BLK
    ov = np.clip(np.minimum(c_start[:, None] + CMP_BLK, s_start[None, :] + SEL_BLK)
                 - np.maximum(c_start[:, None], s_start[None, :]), 0, None) / CMP_BLK
    n_half = seq // CMP_STRIDE
    out = np.zeros((n_blk, n_half), np.float32)
    out[:, :n_cmp] = ov.T
    return jnp.asarray(out, BF16)


def kernel(x, c, w_ada, b_ada, norm_mix, w_in, nsa_pos_k, nsa_pos_v, nsa_ck_w1, nsa_ck_b1, nsa_ck_w2, nsa_ck_b2, nsa_cv_w1, nsa_cv_b1, nsa_cv_w2, nsa_cv_b2, gdn_conv, gdn_a_log, gdn_dt_bias, gdn_norm, w_proj_nsa, w_proj_gdn, w_out, norm_ffn, ffn_up, ffn_conv, ffn_conv_b, ffn_down, norm_final):
    bsz, seq, d = x.shape
    depth = w_ada.shape[0]
    n_grp, dk = NSA_KV_HEADS, NSA_HEAD_DIM
    q_cols = NSA_HEADS * dk
    gqkv_cols = 3 * GDN_HEADS * GDN_HEAD_DIM
    z_cols = GDN_HEADS * GDN_HEAD_DIM
    o_kv = q_cols
    o_gl = o_kv + 6 * n_grp * dk
    o_gq = o_gl + 3 * NSA_HEADS
    o_a = o_gq + gqkv_cols
    o_z = o_a + 2 * GDN_HEADS
    overlap_t = _overlap_t(seq)
    x2 = x.reshape(bsz * seq, d)
    assert depth == 1, "kernel supports the problem's DEPTH == 1"
    for l in range(depth):
        mod = _adaln_mod(c, w_ada[l], b_ada[l])
        w = w_in[l]
        kv_w = w[:, o_kv:o_gl].reshape(d, 6, n_grp, dk)
        zero = jnp.zeros((d, n_grp, dk), F32)
        pair = lambda a, b: jnp.concatenate([a, b], axis=2).reshape(d, n_grp * 2 * dk)
        w_att = jnp.concatenate([w[:, :o_kv] * (dk ** -0.5 * LOG2_E), pair(kv_w[:, 2], zero),
                                 pair(kv_w[:, 4], zero), pair(kv_w[:, 3], kv_w[:, 5])],
                                axis=1).astype(BF16)
        gate_w = jnp.pad(w[:, o_gl:o_gq].reshape(d, n_grp, 3 * NSA_REP),
                         ((0, 0), (0, 0), (0, NSA_AUX - 3 * NSA_REP))).reshape(d, SMALL_A)
        w_small = jnp.concatenate([gate_w, w[:, o_a:o_z],
                                   jnp.zeros((d, LANES - SMALL_B - GDN_HEADS), F32)],
                                  axis=1).astype(BF16)
        w_rest = jnp.concatenate([w[:, o_gq:o_a], w[:, o_z:], pair(kv_w[:, 0], kv_w[:, 1])],
                                 axis=1).astype(BF16)
        cmp_block = (gqkv_cols + z_cols + 2 * d) // LANES

        p_att, h = _norm_mod_matmul(x2, norm_mix[l], mod, w_att, 1280, BF16, seq)
        p_small = _matmul(h, w_small, LANES, F32)
        pc = _matmul(h, w_rest, 1664, F32)
        pc3 = pc.reshape(bsz, seq, pc.shape[1])
        small3 = p_small.reshape(bsz, seq, LANES)

        w1k = nsa_ck_w1[l].reshape(2, CMP_STRIDE, dk, CMP_HIDDEN)
        w1v = nsa_cv_w1[l].reshape(2, CMP_STRIDE, dk, CMP_HIDDEN)
        zw = jnp.zeros((CMP_STRIDE, dk, 2 * CMP_HIDDEN), F32)
        wc = jnp.concatenate(
            [jnp.concatenate([w1k[0], w1k[1], zw], axis=2),
             jnp.concatenate([zw, w1v[0], w1v[1]], axis=2)], axis=1).astype(BF16)
        pos = jnp.stack([nsa_pos_k[l], nsa_pos_v[l]]).reshape(2, 1, CMP_BLK * dk)
        pos = jnp.broadcast_to(pos, (2, 2 * SUBLANES, CMP_BLK * dk)).astype(BF16)
        cw1 = jnp.stack([nsa_ck_w1[l], nsa_cv_w1[l]]).astype(BF16)
        cb1 = jnp.stack([nsa_ck_b1[l], nsa_cv_b1[l]]).reshape(2, 1, CMP_HIDDEN)
        cw2 = jnp.stack([nsa_ck_w2[l], nsa_cv_w2[l]]).astype(BF16)
        cb2 = jnp.stack([nsa_ck_b2[l], nsa_cv_b2[l]]).reshape(2, 1, dk)
        eye = jnp.eye(NSA_REP * dk, dtype=BF16)
        kc, vcT = _compress(pc3, wc, pos, cw1, cb1, cw2, cb2, eye, cmp_block)

        blk_in_tile = np.arange(NSA_TK) // SEL_BLK
        aux_s = np.zeros((NSA_TK, LANES), np.float32)
        aux_s[np.arange(NSA_TK), dk + blk_in_tile] = 1.0
        y_a = _nsa_attn(p_att.reshape(bsz, seq, p_att.shape[1]), small3, kc, vcT, overlap_t,
                        eye, jnp.asarray(aux_s, BF16)).reshape(bsz * seq, q_cols)

        alog_pad = jnp.zeros((1, LANES), F32).at[0, SMALL_A:SMALL_A + GDN_HEADS].set(gdn_a_log[l])
        dtb_pad = jnp.zeros((1, LANES), F32).at[0, SMALL_A:SMALL_A + GDN_HEADS].set(gdn_dt_bias[l])
        y_b = _gated_deltanet(pc, p_small, gdn_conv[l], alog_pad, dtb_pad, gdn_norm[l], bsz, seq)

        x1, h2 = _merge_project(y_a, y_b, pc, x2, mod, w_proj_nsa[l].astype(BF16),
                                w_proj_gdn[l].astype(BF16), w_out[l].astype(BF16),
                                norm_ffn[l], seq)
        out = _conv_ffn(h2, x1, mod, ffn_up[l].astype(BF16), ffn_conv[l],
                        ffn_conv_b[l].reshape(1, 2 * FFN_DIM), ffn_down[l].astype(BF16),
                        norm_final, seq)
    return out.reshape(bsz, seq, d)
```

```python
import functools

import numpy as np
import jax
import jax.numpy as jnp
from jax import lax
from jax.experimental import pallas as pl
from jax.experimental.pallas import tpu as pltpu

F32 = jnp.float32
BF16 = jnp.bfloat16
HIGHEST = lax.Precision.HIGHEST

NSA_HEAD_DIM = 64
NSA_HEADS = 16
NSA_KV_HEADS = 4
NSA_REP = NSA_HEADS // NSA_KV_HEADS
CMP_BLK = 32
CMP_STRIDE = 16
CMP_HIDDEN = 256
SEL_BLK = 64
N_SEL = 16
WINDOW = 512
GDN_HEAD_DIM = 128
GDN_HEADS = 8
GDN_CONV = 4
GDN_CHUNK = 64
FFN_DIM = 2816
FFN_CONV = 3
NORM_EPS = 1e-6
MASK_VALUE = -1e30

LANES = 128
SUBLANES = 8
VMEM_LIMIT = 56 * 1024 * 1024

NSA_TQ = 256
NSA_TK = 512
NSA_AUX = 16
LOG2_E = 1.4426950408889634
PROJ_TILE = 1024
ROW_TILE = 512
FFN_TF = 1408

SH_M, SC_M, GT_M, SH_F, SC_F, GT_F = range(6)
SMALL_A = NSA_KV_HEADS * NSA_AUX
SMALL_B = SMALL_A + GDN_HEADS


def _params(sem):
    return pltpu.CompilerParams(dimension_semantics=sem, vmem_limit_bytes=VMEM_LIMIT)


def _split3(a):
    hi = a.astype(BF16)
    r1 = a - hi.astype(F32)
    mid = r1.astype(BF16)
    lo = (r1 - mid.astype(F32)).astype(BF16)
    return hi, mid, lo


def _dot_exact_lhs(a_bf16, b, dims=None):
    if dims is None:
        dims = (((a_bf16.ndim - 1,), (0,)), ((), ()))
    out = None
    for part in _split3(b):
        term = lax.dot_general(a_bf16, part, dimension_numbers=dims, preferred_element_type=F32)
        out = term if out is None else out + term
    return out


NT_DIMS = (((1,), (1,)), ((), ()))
TN_DIMS = (((0,), (0,)), ((), ()))
NN_DIMS = (((1,), (0,)), ((), ()))


def _mod_kernel(c_ref, w_ref, b_ref, o_ref):
    o_ref[...] = jnp.dot(c_ref[...], w_ref[...], preferred_element_type=F32,
                         precision=HIGHEST) + b_ref[...]


def _adaln_mod(c, w_ada, b_ada):
    bsz, d = c.shape
    n = w_ada.shape[1]
    c8 = jnp.zeros((SUBLANES, d), F32).at[:bsz].set(c)
    out = pl.pallas_call(
        _mod_kernel,
        grid=(n // d,),
        in_specs=[pl.BlockSpec((SUBLANES, d), lambda j: (0, 0)),
                  pl.BlockSpec((d, d), lambda j: (0, j)),
                  pl.BlockSpec((1, d), lambda j: (0, j))],
        out_specs=pl.BlockSpec((SUBLANES, d), lambda j: (0, j)),
        out_shape=jax.ShapeDtypeStruct((SUBLANES, n), F32),
        compiler_params=_params(("arbitrary",)), name="adaln_mod",
    )(c8, w_ada, b_ada.reshape(1, n))
    return out[:bsz].reshape(bsz, n // d, d)


def _modulated_norm(x, norm_w, mod, sc_row, sh_row):
    ms = jnp.mean(x * x, axis=-1, keepdims=True)
    y = x * lax.rsqrt(ms + NORM_EPS) * norm_w
    return y * (1.0 + mod[sc_row:sc_row + 1, :]) + mod[sh_row:sh_row + 1, :]


def _nmm_kernel(x_ref, nw_ref, mod_ref, w_ref, o_ref, h_ref):
    @pl.when(pl.program_id(1) == 0)
    def _():
        h = _modulated_norm(x_ref[...], nw_ref[...], mod_ref[0], SC_M, SH_M)
        h_ref[...] = h.astype(BF16)

    o_ref[...] = jnp.dot(h_ref[...], w_ref[...],
                         preferred_element_type=F32).astype(o_ref.dtype)


def _norm_mod_matmul(x2, norm_w, mod, w, tn, out_dtype, seq):
    t, d = x2.shape
    n = w.shape[1]
    tiles_per_batch = seq // PROJ_TILE
    return pl.pallas_call(
        _nmm_kernel,
        grid=(t // PROJ_TILE, n // tn),
        in_specs=[pl.BlockSpec((PROJ_TILE, d), lambda i, j: (i, 0)),
                  pl.BlockSpec((1, d), lambda i, j: (0, 0)),
                  pl.BlockSpec((1, 6, d), lambda i, j: (i // tiles_per_batch, 0, 0)),
                  pl.BlockSpec((d, tn), lambda i, j: (0, j))],
        out_specs=[pl.BlockSpec((PROJ_TILE, tn), lambda i, j: (i, j)),
                   pl.BlockSpec((PROJ_TILE, d), lambda i, j: (i, 0))],
        out_shape=[jax.ShapeDtypeStruct((t, n), out_dtype), jax.ShapeDtypeStruct((t, d), BF16)],
        compiler_params=_params(("parallel", "arbitrary")), name="norm_mod_proj",
    )(x2, norm_w.reshape(1, d), mod, w)


def _mm_kernel(h_ref, w_ref, o_ref):
    o_ref[...] = jnp.dot(h_ref[...], w_ref[...], preferred_element_type=F32).astype(o_ref.dtype)


def _matmul(h, w, tn, out_dtype):
    t, d = h.shape
    n = w.shape[1]
    return pl.pallas_call(
        _mm_kernel,
        grid=(n // tn, t // PROJ_TILE),
        in_specs=[pl.BlockSpec((PROJ_TILE, d), lambda j, i: (i, 0)),
                  pl.BlockSpec((d, tn), lambda j, i: (0, j))],
        out_specs=pl.BlockSpec((PROJ_TILE, tn), lambda j, i: (i, j)),
        out_shape=jax.ShapeDtypeStruct((t, n), out_dtype),
        compiler_params=_params(("parallel", "parallel")), name="proj",
    )(h, w)


def _mm_conv_kernel(h_ref, w_ref, cw_ref, o_ref, u_ref, *, tiles_per_batch):
    tm = h_ref.shape[0]
    width = cw_ref.shape[0]

    @pl.when(pl.program_id(1) % tiles_per_batch == 0)
    def _():
        u_ref[0:SUBLANES, :] = jnp.zeros((SUBLANES, u_ref.shape[1]), F32)

    cw = cw_ref[...]
    n_sub = 4
    sub = tm // n_sub

    def matmul(s):
        u_ref[SUBLANES + s * sub:SUBLANES + (s + 1) * sub, :] = jnp.dot(
            h_ref[s * sub:(s + 1) * sub, :], w_ref[...], preferred_element_type=F32)

    def conv(s):
        base = SUBLANES + s * sub
        y = cw[width - 1:width, :] * u_ref[base:base + sub, :]
        for i in range(width - 1):
            off = base - (width - 1) + i
            y = y + cw[i:i + 1, :] * u_ref[off:off + sub, :]
        o_ref[s * sub:(s + 1) * sub, :] = y * jax.nn.sigmoid(y)

    matmul(0)
    for s in range(1, n_sub):
        matmul(s)
        conv(s - 1)
    conv(n_sub - 1)
    u_ref[0:SUBLANES, :] = u_ref[tm:tm + SUBLANES, :]


def _matmul_conv_silu(h, w, conv_w, tn, seq):
    t, d = h.shape
    n = w.shape[1]
    return pl.pallas_call(
        functools.partial(_mm_conv_kernel, tiles_per_batch=seq // PROJ_TILE),
        grid=(n // tn, t // PROJ_TILE),
        in_specs=[pl.BlockSpec((PROJ_TILE, d), lambda j, i: (i, 0)),
                  pl.BlockSpec((d, tn), lambda j, i: (0, j)),
                  pl.BlockSpec((conv_w.shape[0], tn), lambda j, i: (0, j))],
        out_specs=pl.BlockSpec((PROJ_TILE, tn), lambda j, i: (i, j)),
        out_shape=jax.ShapeDtypeStruct((t, n), F32),
        scratch_shapes=[pltpu.VMEM((PROJ_TILE + SUBLANES, tn), F32)],
        compiler_params=_params(("parallel", "arbitrary")), name="proj_conv_silu",
    )(h, w, conv_w)


def _compress_kernel(t_ref, wc_ref, pos_ref, w1_ref, b1_ref, w2_ref, b2_ref, eye_ref,
                     kc_ref, vcT_ref):
    n_half = t_ref.shape[1] // CMP_STRIDE
    acc = None
    for l in range(CMP_STRIDE):
        tok = t_ref[0, pl.ds(l, n_half, stride=CMP_STRIDE), :].astype(BF16)
        term = jnp.dot(tok, wc_ref[l], preferred_element_type=F32)
        acc = term if acc is None else acc + term
    outs = []
    for kind in range(2):
        top = acc[:, 2 * kind * CMP_HIDDEN:(2 * kind + 1) * CMP_HIDDEN]
        bot = acc[:, (2 * kind + 1) * CMP_HIDDEN:(2 * kind + 2) * CMP_HIDDEN]
        bot_next = pltpu.roll(bot, n_half - 1, 0)
        pos_term = jnp.dot(pos_ref[kind], w1_ref[kind], preferred_element_type=F32)[0:1]
        hid = jax.nn.gelu(top + bot_next + pos_term + b1_ref[kind])
        outs.append(jnp.dot(hid.astype(BF16), w2_ref[kind], preferred_element_type=F32)
                    + b2_ref[kind])
    kc_ref[0, 0] = outs[0].astype(BF16)
    dk = NSA_HEAD_DIM
    vcT_ref[0, 0] = lax.dot_general(eye_ref[:dk, :dk], outs[1].astype(BF16), NT_DIMS,
                                    preferred_element_type=F32).astype(BF16)


def _compress(pc3, wc, pos, w1, b1, w2, b2, eye, first_col_block):
    bsz, seq, _ = pc3.shape
    n_grp, dk = NSA_KV_HEADS, NSA_HEAD_DIM
    n_half = seq // CMP_STRIDE
    const3 = lambda b, g: (0, 0, 0)
    return pl.pallas_call(
        _compress_kernel,
        grid=(bsz, n_grp),
        in_specs=[pl.BlockSpec((1, seq, LANES), lambda b, g: (b, 0, first_col_block + g)),
                  pl.BlockSpec(wc.shape, const3),
                  pl.BlockSpec(pos.shape, const3),
                  pl.BlockSpec(w1.shape, const3),
                  pl.BlockSpec(b1.shape, const3),
                  pl.BlockSpec(w2.shape, const3),
                  pl.BlockSpec(b2.shape, const3),
                  pl.BlockSpec(eye.shape, lambda b, g: (0, 0))],
        out_specs=[pl.BlockSpec((1, 1, n_half, dk), lambda b, g: (b, g, 0, 0)),
                   pl.BlockSpec((1, 1, dk, n_half), lambda b, g: (b, g, 0, 0))],
        out_shape=[jax.ShapeDtypeStruct((bsz, n_grp, n_half, dk), BF16),
                   jax.ShapeDtypeStruct((bsz, n_grp, dk, n_half), BF16)],
        compiler_params=_params(("parallel", "parallel")), name="nsa_compress",
    )(pc3, wc, pos, w1, b1, w2, b2, eye)


def _nsa_attn_kernel(q_ref, ks_ref, kw_ref, v_ref, kc_ref, vcT_ref, ov_ref, gl_ref, eye_ref,
                     aux_ref, o_ref,
                     ksa_ref, kwa_ref, vsT_ref, vwT_ref, sel_ref, s_ref, m_ref, acc_ref,
                     sw_ref, mw_ref, accw_ref, part_ref, gate_ref):
    tq, tk = NSA_TQ, NSA_TK
    rows = NSA_REP * tq
    dk = NSA_HEAD_DIM
    grp = pl.program_id(1)
    qt = pl.program_id(2)
    qs = qt * tq
    seq = ks_ref.shape[1]
    eye = eye_ref[...]
    tq_per_tk = tk // tq

    @pl.when(qt == 0)
    def _():
        lane = lax.broadcasted_iota(jnp.int32, (tk, LANES), 1)
        lane_flag = jnp.where(lane == dk, 1.0, 0.0).astype(BF16)
        ones_rows = jnp.ones((NSA_AUX, tk), BF16)

        def build(j, carry):
            r0 = pl.multiple_of(j * tk, tk)
            ksa_ref[pl.ds(r0, tk), :] = ks_ref[0, pl.ds(r0, tk), :] + aux_ref[...]
            kwa_ref[pl.ds(r0, tk), :] = kw_ref[0, pl.ds(r0, tk), :] + lane_flag
            vT = lax.dot_general(eye[:LANES, :LANES], v_ref[0, pl.ds(r0, tk), :], NT_DIMS,
                                 preferred_element_type=F32).astype(BF16)
            vsT_ref[j, 0:dk, :] = vT[:dk]
            vsT_ref[j, dk:, :] = ones_rows
            for i in range(tq_per_tk):
                vwT_ref[j * tq_per_tk + i, 0:dk, :] = vT[dk:, i * tq:(i + 1) * tq]
                vwT_ref[j * tq_per_tk + i, dk:, :] = ones_rows[:, :tq]
            return carry

        lax.fori_loop(0, seq // tk, build, 0)

    q4 = lax.dot_general(eye[:NSA_REP * dk, :NSA_REP * dk], q_ref[0], NT_DIMS,
                         preferred_element_type=F32).astype(BF16)
    qT = jnp.concatenate([q4[r * dk:(r + 1) * dk, :] for r in range(NSA_REP)], axis=1)
    t_q = qs + lax.broadcasted_iota(jnp.int32, (1, tq), 1)
    t_row = jnp.concatenate([t_q] * NSA_REP, axis=1)
    zpad = jnp.zeros((LANES - dk - NSA_AUX, rows), BF16)

    def per_head(a):
        return jnp.concatenate([a] * NSA_REP, axis=1)

    gate_ref[...] = _dot_exact_lhs(eye[:LANES, :LANES], jax.nn.sigmoid(gl_ref[0]), NT_DIMS)
    gates = gate_ref[pl.ds(pl.multiple_of(grp * NSA_AUX, NSA_AUX), NSA_AUX), :]

    n_cmp = kc_ref.shape[2]
    s_c = jnp.dot(kc_ref[0, 0], qT, preferred_element_type=F32)
    c_end = lax.broadcasted_iota(jnp.int32, (n_cmp, 1), 0) * CMP_STRIDE + (CMP_BLK - 1)
    s_c = s_c + per_head(jnp.where(c_end <= t_q, 0.0, MASK_VALUE))
    mx_c = jnp.max(s_c, axis=0, keepdims=True)
    p_c = jnp.exp2(s_c - mx_c)
    l_c = jnp.sum(p_c, axis=0, keepdims=True)
    p_c = p_c * jnp.where(t_row >= CMP_BLK - 1, 1.0 / l_c, 0.0)
    o_c = jnp.dot(vcT_ref[0, 0], p_c.astype(BF16), preferred_element_type=F32)
    for r in range(NSA_REP):
        part_ref[r * dk:(r + 1) * dk, :] = gates[3 * r:3 * r + 1] * o_c[:, r * tq:(r + 1) * tq]

    p_sum = p_c[:, 0:tq]
    for r in range(1, NSA_REP):
        p_sum = p_sum + p_c[:, r * tq:(r + 1) * tq]
    ov = ov_ref[...]
    imp = _dot_exact_lhs(ov, p_sum)
    n_blk = ov.shape[0]
    j_blk = lax.broadcasted_iota(jnp.int32, (n_blk, 1), 0)
    cur = t_q // SEL_BLK
    forced = (j_blk == 0) | (j_blk == cur) | (j_blk == cur - 1)
    valid = j_blk * SEL_BLK <= t_q
    n_rank = N_SEL - 3
    cand = jnp.where(forced, -jnp.inf, jnp.where(valid, imp, -1.0))

    n_wt = WINDOW // tq + 1
    row0 = lax.broadcasted_iota(jnp.int32, (NSA_AUX, rows), 0) == 0
    a_k = lax.broadcasted_iota(jnp.int32, (tq, 1), 0)
    b_q = lax.broadcasted_iota(jnp.int32, (1, tq), 1)
    w_tiles = [jnp.maximum(qt - (n_wt - 1) + i, 0) for i in range(n_wt)]

    def win_scores(i):
        before_start = qt - (n_wt - 1) + i < 0
        flag = jnp.where(row0 & before_start, MASK_VALUE, 0.0).astype(BF16)
        rhs_w = jnp.concatenate([qT, flag, zpad], axis=0)
        k_tile = kwa_ref[pl.ds(pl.multiple_of(w_tiles[i] * tq, tq), tq), :]
        s = jnp.dot(k_tile, rhs_w, preferred_element_type=F32)
        if i == 0:
            s = s + per_head(jnp.where(a_k > b_q, 0.0, MASK_VALUE))
        if i == n_wt - 1:
            s = s + per_head(jnp.where(a_k <= b_q, 0.0, MASK_VALUE))
        sw_ref[i * tq:(i + 1) * tq, :] = s

    def win_max():
        mw_ref[...] = jnp.max(sw_ref[...], axis=0, keepdims=True)

    def win_pv(i):
        p = jnp.exp2(sw_ref[i * tq:(i + 1) * tq, :] - mw_ref[...]).astype(BF16)
        pv = jnp.dot(vwT_ref[w_tiles[i]], p, preferred_element_type=F32)
        if i == 0:
            accw_ref[...] = pv
        else:
            accw_ref[...] = accw_ref[...] + pv

    win_steps = ([functools.partial(win_scores, i) for i in range(n_wt)] + [win_max]
                 + [functools.partial(win_pv, i) for i in range(n_wt)])

    work = cand
    maxima = []
    for r in range(n_rank):
        mx = jnp.max(work, axis=0, keepdims=True)
        maxima.append(mx)
        work = jnp.where(work == mx, -jnp.inf, work)
        if r < len(win_steps):
            win_steps[r]()
    for step in win_steps[n_rank:]:
        step()

    ones_lhs = jnp.ones((SUBLANES, n_blk), BF16)

    def count(mask):
        return jnp.dot(ones_lhs, jnp.where(mask, 1.0, 0.0).astype(BF16),
                       preferred_element_type=F32)[0:1]

    thr = maxima[-1]
    for mx in reversed(maxima[:-1]):
        thr = jnp.where(count(cand >= mx) >= n_rank, mx, thr)
    ties = cand == thr
    below = (lax.broadcasted_iota(jnp.int32, (n_blk, n_blk), 1)
             < lax.broadcasted_iota(jnp.int32, (n_blk, n_blk), 0))
    tie_rank = jnp.dot(jnp.where(below, 1.0, 0.0).astype(BF16),
                       jnp.where(ties, 1.0, 0.0).astype(BF16), preferred_element_type=F32)
    picked = (cand > thr) | (ties & (tie_rank < n_rank - count(cand > thr)))
    sel_ref[0:n_blk, :] = jnp.where(forced | (picked & (cand >= 0.0)), 0.0, MASK_VALUE)
    sel_ref[n_blk:, :] = jnp.zeros((NSA_AUX, tq), F32)

    acc_w = accw_ref[...]
    o_w = acc_w[:dk] * (1.0 / acc_w[dk:dk + 1])
    for r in range(NSA_REP):
        part_ref[r * dk:(r + 1) * dk, :] = (part_ref[r * dk:(r + 1) * dk, :]
                                            + gates[3 * r + 2:3 * r + 3] * o_w[:, r * tq:(r + 1) * tq])

    blk_per_tile = tk // SEL_BLK
    n_full = qs // tk

    def scores(kt, slot, diagonal=False):
        sb = sel_ref[pl.ds(pl.multiple_of(kt * blk_per_tile, blk_per_tile), NSA_AUX), :]
        rhs = jnp.concatenate([qT, per_head(sb).astype(BF16), zpad], axis=0)
        k_tile = ksa_ref[pl.ds(pl.multiple_of(kt * tk, tk), tk), :]
        s = jnp.dot(k_tile, rhs, preferred_element_type=F32)
        if diagonal:
            kpos = kt * tk + lax.broadcasted_iota(jnp.int32, (tk, 1), 0)
            s = s + per_head(jnp.where(kpos <= t_q, 0.0, MASK_VALUE))
        s_ref[slot] = s

    def softmax_pv(kt, slot):
        m_i = m_ref[...]
        m_new = jnp.maximum(m_i, jnp.max(s_ref[slot], axis=0, keepdims=True))
        alpha = jnp.exp2(m_i - m_new)
        p = jnp.exp2(s_ref[slot] - m_new).astype(BF16)
        pv = jnp.dot(vsT_ref[kt], p, preferred_element_type=F32)
        acc_ref[...] = alpha * acc_ref[...] + pv
        m_ref[...] = m_new

    m_ref[...] = jnp.full((1, rows), MASK_VALUE, F32)
    acc_ref[...] = jnp.zeros((dk + NSA_AUX, rows), F32)
    scores(n_full, 0, diagonal=True)

    n_pairs = n_full // 2

    def slot0_tile(i):
        return jnp.where(i == 0, n_full, 2 * i - 1)

    def pair_step(i, carry):
        scores(2 * i, 1)
        softmax_pv(slot0_tile(i), 0)
        scores(2 * i + 1, 0)
        softmax_pv(2 * i, 1)
        return carry

    lax.fori_loop(0, n_pairs, pair_step, 0)

    @pl.when(n_full % 2 == 1)
    def _():
        scores(2 * n_pairs, 1)
        softmax_pv(slot0_tile(n_pairs), 0)
        softmax_pv(2 * n_pairs, 1)

    @pl.when(n_full % 2 == 0)
    def _():
        softmax_pv(slot0_tile(n_pairs), 0)

    acc_s = acc_ref[...]
    o_s = acc_s[:dk] * (1.0 / acc_s[dk:dk + 1])
    gates = gate_ref[pl.ds(pl.multiple_of(grp * NSA_AUX, NSA_AUX), NSA_AUX), :]
    out_t = jnp.concatenate(
        [part_ref[r * dk:(r + 1) * dk, :] + gates[3 * r + 1:3 * r + 2] * o_s[:, r * tq:(r + 1) * tq]
         for r in range(NSA_REP)], axis=0).astype(BF16)
    o_ref[0] = lax.dot_general(eye[:tq, :tq], out_t, NT_DIMS,
                               preferred_element_type=F32).astype(o_ref.dtype)


def _nsa_attn(p_att3, aux3, small_block, kc, vcT, overlap_t, eye, aux_s):
    bsz, seq, _ = p_att3.shape
    n_grp, dk = NSA_KV_HEADS, NSA_HEAD_DIM
    tq, tk = NSA_TQ, NSA_TK
    rows = NSA_REP * tq
    q_w = NSA_REP * dk
    n_cmp = kc.shape[2]
    n_blk = seq // SEL_BLK
    v_rows = dk + NSA_AUX
    n_wt = WINDOW // tq + 1
    first = NSA_HEADS * dk // LANES
    kv_spec = lambda off: pl.BlockSpec((1, seq, LANES), lambda b, g, t: (b, 0, first + off + g))
    return pl.pallas_call(
        _nsa_attn_kernel,
        grid=(bsz, n_grp, seq // tq),
        in_specs=[
            pl.BlockSpec((1, tq, q_w), lambda b, g, t: (b, t, g)),
            kv_spec(0), kv_spec(n_grp), kv_spec(2 * n_grp),
            pl.BlockSpec((1, 1, n_cmp, dk), lambda b, g, t: (b, g, 0, 0)),
            pl.BlockSpec((1, 1, dk, n_cmp), lambda b, g, t: (b, g, 0, 0)),
            pl.BlockSpec((n_blk, n_cmp), lambda b, g, t: (0, 0)),
            pl.BlockSpec((1, tq, LANES), lambda b, g, t: (b, t, small_block)),
            pl.BlockSpec(eye.shape, lambda b, g, t: (0, 0)),
            pl.BlockSpec((tk, LANES), lambda b, g, t: (0, 0)),
        ],
        out_specs=pl.BlockSpec((1, tq, q_w), lambda b, g, t: (b, t, g)),
        out_shape=jax.ShapeDtypeStruct((bsz, seq, NSA_HEADS * dk), BF16),
        scratch_shapes=[pltpu.VMEM((seq, LANES), BF16),
                        pltpu.VMEM((seq, LANES), BF16),
                        pltpu.VMEM((seq // tk, v_rows, tk), BF16),
                        pltpu.VMEM((seq // tq, v_rows, tq), BF16),
                        pltpu.VMEM((n_blk + NSA_AUX, tq), F32),
                        pltpu.VMEM((2, tk, rows), F32),
                        pltpu.VMEM((1, rows), F32),
                        pltpu.VMEM((v_rows, rows), F32),
                        pltpu.VMEM((n_wt * tq, rows), F32),
                        pltpu.VMEM((1, rows), F32),
                        pltpu.VMEM((v_rows, rows), F32),
                        pltpu.VMEM((q_w, tq), F32),
                        pltpu.VMEM((LANES, tq), F32)],
        compiler_params=_params(("parallel", "parallel", "arbitrary")), name="nsa_attention",
    )(p_att3, p_att3, p_att3, p_att3, kc, vcT, overlap_t, aux3, eye, aux_s)


def _bmm(a, b, dims):
    a = a.astype(BF16)
    b = b.astype(BF16)
    return jnp.stack([lax.dot_general(a[n], b[n], dims, preferred_element_type=F32)
                      for n in range(a.shape[0])])


def _gdn_kernel(y_ref, z_ref, small_ref, alog_ref, dtb_ref, nw_ref, o_ref, state_ref):
    bsz = y_ref.shape[0]
    cs = GDN_CHUNK
    hd = GDN_HEAD_DIM
    nh = GDN_HEADS
    c = pl.program_id(0)

    @pl.when(c == 0)
    def _():
        state_ref[...] = jnp.zeros_like(state_ref)

    y = y_ref[...]

    small = small_ref[...]
    sp_in = small + dtb_ref[...]
    softplus = jnp.maximum(sp_in, 0.0) + jnp.log1p(jnp.exp(-jnp.abs(sp_in)))
    g_all = -jnp.exp(alog_ref[...]) * softplus
    beta_all = jax.nn.sigmoid(small)
    ii = lax.broadcasted_iota(jnp.int32, (cs, cs), 0)
    jj = lax.broadcasted_iota(jnp.int32, (cs, cs), 1)
    incl = ii >= jj
    strict = ii > jj
    tril = jnp.where(incl, 1.0, 0.0).astype(BF16)
    e128 = jnp.where(lax.broadcasted_iota(jnp.int32, (LANES, LANES), 0)
                     == lax.broadcasted_iota(jnp.int32, (LANES, LANES), 1), 1.0, 0.0).astype(BF16)
    gc_all = [_dot_exact_lhs(tril, g_all[b]) for b in range(bsz)]
    gcT_all = [_dot_exact_lhs(e128, gc_all[b], NT_DIMS) for b in range(bsz)]

    chains = [(b, h) for b in range(bsz) for h in range(nh)]

    def heads(a, base):
        return jnp.stack([a[b][:, base + h * hd:base + (h + 1) * hd] for b, h in chains])

    def lane_bcast(cols, base):
        return jnp.stack([jnp.broadcast_to(cols[b][:, base + h:base + h + 1], (cs, hd))
                          for b, h in chains])

    n_qk = nh * hd
    q = heads(y, 0)
    k = heads(y, n_qk)
    v = heads(y, 2 * n_qk)
    q = q * lax.rsqrt(jnp.sum(q * q, axis=-1, keepdims=True) + NORM_EPS) * (hd ** -0.5)
    k = k * lax.rsqrt(jnp.sum(k * k, axis=-1, keepdims=True) + NORM_EPS)
    gc = lane_bcast(gc_all, SMALL_A)
    beta = lane_bcast([beta_all[b] for b in range(bsz)], SMALL_B)
    gc_row = jnp.stack([gcT_all[b][SMALL_A + h:SMALL_A + h + 1, :] for b, h in chains])

    decay = jnp.exp(jnp.where(incl, gc[:, :, :cs] - gc_row, MASK_VALUE))
    kk = _bmm(k, k, NT_DIMS)
    m_full = jnp.where(strict, -(beta[:, :, :cs] * kk * decay), 0.0)
    blk_diff = ii ^ jj
    inv_m1 = jnp.where(blk_diff < 2, m_full, 0.0)
    size = 2
    while size < cs:
        m_off = jnp.where((blk_diff >= size) & (blk_diff < 2 * size), m_full, 0.0)
        t_mat = m_off + _bmm(inv_m1, m_off, NN_DIMS)
        inv_m1 = inv_m1 + t_mat + _bmm(t_mat, inv_m1, NN_DIMS)
        size *= 2
    e_gc = jnp.exp(gc)
    rhs = jnp.concatenate([v * beta, k * (beta * e_gc)], axis=2)
    sol = rhs + _bmm(inv_m1, rhs, NN_DIMS)
    u, w = sol[:, :, :hd], sol[:, :, hd:]
    qk = _bmm(q, k, NT_DIMS) * decay
    q_dec = q * e_gc
    gc_last = gc[:, cs - 1:cs, :]
    k_dec = k * jnp.exp(gc_last - gc)
    state = state_ref[...]
    v_new = u - _bmm(w, state, NN_DIMS)
    o = _bmm(q_dec, state, NN_DIMS) + _bmm(qk, v_new, NN_DIMS)
    state_ref[...] = state * jnp.exp(gc_last) + _bmm(k_dec, v_new, TN_DIMS)

    o = o * lax.rsqrt(jnp.mean(o * o, axis=-1, keepdims=True) + NORM_EPS) * nw_ref[...]
    z = heads(z_ref[...].astype(F32), 0)
    o = o * (z * jax.nn.sigmoid(z))
    for b in range(bsz):
        o_ref[b] = jnp.concatenate([o[b * nh + h] for h in range(nh)], axis=1).astype(o_ref.dtype)


def _gated_deltanet(qkv3, zm3, aux3, small_block, alog_pad, dtb_pad, norm_w):
    bsz, seq, qkv_cols = qkv3.shape
    cs = GDN_CHUNK
    z_cols = GDN_HEADS * GDN_HEAD_DIM
    out = pl.pallas_call(
        _gdn_kernel,
        grid=(seq // cs,),
        in_specs=[
            pl.BlockSpec((bsz, cs, qkv_cols), lambda c: (0, c, 0)),
            pl.BlockSpec((bsz, cs, z_cols), lambda c: (0, c, 0)),
            pl.BlockSpec((bsz, cs, LANES), lambda c: (0, c, small_block)),
            pl.BlockSpec((1, LANES), lambda c: (0, 0)),
            pl.BlockSpec((1, LANES), lambda c: (0, 0)),
            pl.BlockSpec((1, GDN_HEAD_DIM), lambda c: (0, 0)),
        ],
        out_specs=pl.BlockSpec((bsz, cs, z_cols), lambda c: (0, c, 0)),
        out_shape=jax.ShapeDtypeStruct((bsz, seq, z_cols), BF16),
        scratch_shapes=[pltpu.VMEM((bsz * GDN_HEADS, GDN_HEAD_DIM, GDN_HEAD_DIM), F32)],
        compiler_params=_params(("arbitrary",)), name="gated_deltanet",
    )(qkv3, zm3, aux3, alog_pad, dtb_pad, norm_w.reshape(1, GDN_HEAD_DIM))
    return out.reshape(bsz * seq, z_cols)


def _merge_kernel(ya_ref, yb_ref, ma_ref, mb_ref, x_ref, mod_ref, wa_ref, wb_ref, wo_ref,
                  nf_ref, x1_ref, h2_ref):
    pa = jnp.dot(ya_ref[...], wa_ref[...], preferred_element_type=F32)
    pb = jnp.dot(yb_ref[...], wb_ref[...], preferred_element_type=F32)
    mixed = (jax.nn.sigmoid(ma_ref[...].astype(F32)) * pa
             + jax.nn.sigmoid(mb_ref[...].astype(F32)) * pb)
    mod = mod_ref[0]
    x1 = x_ref[...] + mod[GT_M:GT_M + 1, :] * jnp.dot(
        mixed.astype(BF16), wo_ref[...], preferred_element_type=F32)
    x1_ref[...] = x1
    h2_ref[...] = _modulated_norm(x1, nf_ref[...], mod, SC_F, SH_F).astype(BF16)


def _merge_project(ya, yb, pc, merge_blk, x2, mod, w_pa, w_pb, w_o, norm_ffn, seq):
    t, d = x2.shape
    tiles_per_batch = seq // ROW_TILE
    row = lambda i: (i, 0)
    const = lambda i: (0, 0)
    return pl.pallas_call(
        _merge_kernel,
        grid=(t // ROW_TILE,),
        in_specs=[pl.BlockSpec((ROW_TILE, d), row),
                  pl.BlockSpec((ROW_TILE, d), row),
                  pl.BlockSpec((ROW_TILE, d), lambda i: (i, merge_blk)),
                  pl.BlockSpec((ROW_TILE, d), lambda i: (i, merge_blk + 1)),
                  pl.BlockSpec((ROW_TILE, d), row),
                  pl.BlockSpec((1, 6, d), lambda i: (i // tiles_per_batch, 0, 0)),
                  pl.BlockSpec((d, d), const),
                  pl.BlockSpec((d, d), const),
                  pl.BlockSpec((d, d), const),
                  pl.BlockSpec((1, d), const)],
        out_specs=[pl.BlockSpec((ROW_TILE, d), row), pl.BlockSpec((ROW_TILE, d), row)],
        out_shape=[jax.ShapeDtypeStruct((t, d), F32), jax.ShapeDtypeStruct((t, d), BF16)],
        compiler_params=_params(("parallel",)), name="merge_project",
    )(ya, yb, pc, pc, x2, mod, w_pa, w_pb, w_o, norm_ffn.reshape(1, d))


def _ffn_kernel(h_ref, halo_ref, res_ref, mod_ref, wg_ref, wv_ref, cwg_ref, cwv_ref,
                cbg_ref, cbv_ref, wd_ref, nfin_ref, o_ref, ug_ref, uv_ref,
                *, tiles_per_batch, last):
    i = pl.program_id(0)
    tm = h_ref.shape[0]
    first_of_batch = (i % tiles_per_batch) == 0
    h = h_ref[...]
    halo = halo_ref[...]

    def conv_half(w_ref, cw_ref, cb_ref, u_ref):
        u_ref[SUBLANES:, :] = jnp.dot(h, w_ref[...], preferred_element_type=F32)
        up_halo = jnp.dot(halo, w_ref[...], preferred_element_type=F32)[SUBLANES:]
        u_ref[0:SUBLANES, :] = jnp.where(first_of_batch, 0.0, up_halo)
        cw = cw_ref[...]
        out = cb_ref[...] + cw[FFN_CONV - 1:FFN_CONV, :] * u_ref[SUBLANES:, :]
        for j in range(FFN_CONV - 1):
            off = SUBLANES - (FFN_CONV - 1) + j
            out = out + cw[j:j + 1, :] * u_ref[off:off + tm, :]
        return out

    gate = conv_half(wg_ref, cwg_ref, cbg_ref, ug_ref)
    val = conv_half(wv_ref, cwv_ref, cbv_ref, uv_ref)
    act = gate * jax.nn.sigmoid(gate) * val
    down = jnp.dot(act.astype(BF16), wd_ref[...], preferred_element_type=F32)
    y = res_ref[...] + mod_ref[0][GT_F:GT_F + 1, :] * down
    if last:
        ms = jnp.mean(y * y, axis=-1, keepdims=True)
        y = y * lax.rsqrt(ms + NORM_EPS) * nfin_ref[...]
    o_ref[...] = y


def _conv_ffn(h2, x1, mod, w_up, conv_w, conv_b, w_down, norm_final, seq):
    t, d = x1.shape
    tm, tf = ROW_TILE, FFN_TF
    n_f = FFN_DIM // tf
    tiles_per_batch = seq // tm
    halo_per_tile = tm // SUBLANES
    row = lambda i: (i, 0)
    res = x1
    for f in range(n_f):
        col = lambda i, c=f: (0, c)
        col_val = lambda i, c=n_f + f: (0, c)
        res = pl.pallas_call(
            functools.partial(_ffn_kernel, tiles_per_batch=tiles_per_batch, last=f == n_f - 1),
            grid=(t // tm,),
            in_specs=[pl.BlockSpec((tm, d), row),
                      pl.BlockSpec((SUBLANES * 2, d),
                                   lambda i: (jnp.maximum(i * (halo_per_tile // 2) - 1, 0), 0)),
                      pl.BlockSpec((tm, d), row),
                      pl.BlockSpec((1, 6, d), lambda i: (i // tiles_per_batch, 0, 0)),
                      pl.BlockSpec((d, tf), col),
                      pl.BlockSpec((d, tf), col_val),
                      pl.BlockSpec((FFN_CONV, tf), col),
                      pl.BlockSpec((FFN_CONV, tf), col_val),
                      pl.BlockSpec((1, tf), col),
                      pl.BlockSpec((1, tf), col_val),
                      pl.BlockSpec((tf, d), lambda i, c=f: (c, 0)),
                      pl.BlockSpec((1, d), lambda i: (0, 0))],
            out_specs=pl.BlockSpec((tm, d), row),
            out_shape=jax.ShapeDtypeStruct((t, d), F32),
            scratch_shapes=[pltpu.VMEM((tm + SUBLANES, tf), F32),
                            pltpu.VMEM((tm + SUBLANES, tf), F32)],
            compiler_params=_params(("parallel",)), name="conv_ffn",
        )(h2, h2, res, mod, w_up, w_up, conv_w, conv_w, conv_b, conv_b, w_down,
          norm_final.reshape(1, d))
    return res


def _overlap_t(seq):
    n_cmp = (seq - CMP_BLK) // CMP_STRIDE + 1
    n_blk = seq // SEL_BLK
    c_start = np.arange(n_cmp) * CMP_STRIDE
    s_start = np.arange(n_blk) * SEL_BLK
    ov = np.clip(np.minimum(c_start[:, None] + CMP_BLK, s_start[None, :] + SEL_BLK)
                 - np.maximum(c_start[:, None], s_start[None, :]), 0, None) / CMP_BLK
    n_half = seq // CMP_STRIDE
    out = np.zeros((n_blk, n_half), np.float32)
    out[:, :n_cmp] = ov.T
    return jnp.asarray(out, BF16)


def kernel(x, c, w_ada, b_ada, norm_mix, w_in, nsa_pos_k, nsa_pos_v, nsa_ck_w1, nsa_ck_b1, nsa_ck_w2, nsa_ck_b2, nsa_cv_w1, nsa_cv_b1, nsa_cv_w2, nsa_cv_b2, gdn_conv, gdn_a_log, gdn_dt_bias, gdn_norm, w_proj_nsa, w_proj_gdn, w_out, norm_ffn, ffn_up, ffn_conv, ffn_conv_b, ffn_down, norm_final):
    bsz, seq, d = x.shape
    depth = w_ada.shape[0]
    n_grp, dk = NSA_KV_HEADS, NSA_HEAD_DIM
    q_cols = NSA_HEADS * dk
    gqkv_cols = 3 * GDN_HEADS * GDN_HEAD_DIM
    z_cols = GDN_HEADS * GDN_HEAD_DIM
    o_kv = q_cols
    o_gl = o_kv + 6 * n_grp * dk
    o_gq = o_gl + 3 * NSA_HEADS
    o_a = o_gq + gqkv_cols
    o_z = o_a + 2 * GDN_HEADS
    overlap_t = _overlap_t(seq)
    x2 = x.reshape(bsz * seq, d)
    assert depth == 1, "kernel supports the problem's DEPTH == 1"
    for l in range(depth):
        mod = _adaln_mod(c, w_ada[l], b_ada[l])
        w = w_in[l]
        kv_w = w[:, o_kv:o_gl].reshape(d, 6, n_grp, dk)
        zero = jnp.zeros((d, n_grp, dk), F32)
        pair = lambda a, b: jnp.concatenate([a, b], axis=2).reshape(d, n_grp * 2 * dk)
        w_att = jnp.concatenate([w[:, :o_kv] * (dk ** -0.5 * LOG2_E), pair(kv_w[:, 2], zero),
                                 pair(kv_w[:, 4], zero), pair(kv_w[:, 3], kv_w[:, 5])],
                                axis=1).astype(BF16)
        gate_w = jnp.pad(w[:, o_gl:o_gq].reshape(d, n_grp, 3 * NSA_REP),
                         ((0, 0), (0, 0), (0, NSA_AUX - 3 * NSA_REP))).reshape(d, SMALL_A)
        w_small = jnp.concatenate([gate_w, w[:, o_a:o_z],
                                   jnp.zeros((d, LANES - SMALL_B - GDN_HEADS), F32)],
                                  axis=1).astype(BF16)
        w_aux = jnp.concatenate([pair(kv_w[:, 0], kv_w[:, 1]).astype(BF16), w_small], axis=1)
        small_block = n_grp
        w_zm = w[:, o_z:].astype(BF16)

        p_att, h = _norm_mod_matmul(x2, norm_mix[l], mod, w_att, 1280, BF16, seq)
        qkv = _matmul_conv_silu(h, w[:, o_gq:o_a].astype(BF16), gdn_conv[l], 1536, seq)
        zm = _matmul(h, w_zm, 1536, BF16)
        aux = _matmul(h, w_aux, w_aux.shape[1], F32)
        aux3 = aux.reshape(bsz, seq, aux.shape[1])

        w1k = nsa_ck_w1[l].reshape(2, CMP_STRIDE, dk, CMP_HIDDEN)
        w1v = nsa_cv_w1[l].reshape(2, CMP_STRIDE, dk, CMP_HIDDEN)
        zw = jnp.zeros((CMP_STRIDE, dk, 2 * CMP_HIDDEN), F32)
        wc = jnp.concatenate(
            [jnp.concatenate([w1k[0], w1k[1], zw], axis=2),
             jnp.concatenate([zw, w1v[0], w1v[1]], axis=2)], axis=1).astype(BF16)
        pos = jnp.stack([nsa_pos_k[l], nsa_pos_v[l]]).reshape(2, 1, CMP_BLK * dk)
        pos = jnp.broadcast_to(pos, (2, 2 * SUBLANES, CMP_BLK * dk)).astype(BF16)
        cw1 = jnp.stack([nsa_ck_w1[l], nsa_cv_w1[l]]).astype(BF16)
        cb1 = jnp.stack([nsa_ck_b1[l], nsa_cv_b1[l]]).reshape(2, 1, CMP_HIDDEN)
        cw2 = jnp.stack([nsa_ck_w2[l], nsa_cv_w2[l]]).astype(BF16)
        cb2 = jnp.stack([nsa_ck_b2[l], nsa_cv_b2[l]]).reshape(2, 1, dk)
        eye = jnp.eye(max(NSA_REP * dk, NSA_TQ), dtype=BF16)
        kc, vcT = _compress(aux3, wc, pos, cw1, cb1, cw2, cb2, eye, 0)

        blk_in_tile = np.arange(NSA_TK) // SEL_BLK
        aux_s = np.zeros((NSA_TK, LANES), np.float32)
        aux_s[np.arange(NSA_TK), dk + blk_in_tile] = 1.0
        y_a = _nsa_attn(p_att.reshape(bsz, seq, p_att.shape[1]), aux3, small_block, kc, vcT,
                        overlap_t, eye, jnp.asarray(aux_s, BF16)).reshape(bsz * seq, q_cols)

        alog_pad = jnp.zeros((1, LANES), F32).at[0, SMALL_A:SMALL_A + GDN_HEADS].set(gdn_a_log[l])
        dtb_pad = jnp.zeros((1, LANES), F32).at[0, SMALL_A:SMALL_A + GDN_HEADS].set(gdn_dt_bias[l])
        y_b = _gated_deltanet(qkv.reshape(bsz, seq, gqkv_cols), zm.reshape(bsz, seq, zm.shape[1]),
                              aux3, small_block, alog_pad, dtb_pad, gdn_norm[l])

        x1, h2 = _merge_project(y_a, y_b, zm, z_cols // d, x2, mod, w_proj_nsa[l].astype(BF16),
                                w_proj_gdn[l].astype(BF16), w_out[l].astype(BF16),
                                norm_ffn[l], seq)
        out = _conv_ffn(h2, x1, mod, ffn_up[l].astype(BF16), ffn_conv[l],
                        ffn_conv_b[l].reshape(1, 2 * FFN_DIM), ffn_down[l].astype(BF16),
                        norm_final, seq)
    return out.reshape(bsz, seq, d)
```

```python
import functools

import numpy as np
import jax
import jax.numpy as jnp
from jax import lax
from jax.experimental import pallas as pl
from jax.experimental.pallas import tpu as pltpu

F32 = jnp.float32
BF16 = jnp.bfloat16
HIGHEST = lax.Precision.HIGHEST

NSA_HEAD_DIM = 64
NSA_HEADS = 16
NSA_KV_HEADS = 4
NSA_REP = NSA_HEADS // NSA_KV_HEADS
CMP_BLK = 32
CMP_STRIDE = 16
CMP_HIDDEN = 256
SEL_BLK = 64
N_SEL = 16
WINDOW = 512
GDN_HEAD_DIM = 128
GDN_HEADS = 8
GDN_CONV = 4
GDN_CHUNK = 64
FFN_DIM = 2816
FFN_CONV = 3
NORM_EPS = 1e-6
MASK_VALUE = -1e30

LANES = 128
SUBLANES = 8
VMEM_LIMIT = 56 * 1024 * 1024

NSA_TQ = 256
NSA_TILES_PER_STEP = 4
NSA_TK = 512
NSA_AUX = 16
LOG2_E = 1.4426950408889634
PROJ_TILE = 1024
ROW_TILE = 512
FFN_TF = 1408

SH_M, SC_M, GT_M, SH_F, SC_F, GT_F = range(6)
SMALL_A = NSA_KV_HEADS * NSA_AUX
SMALL_B = SMALL_A + GDN_HEADS


def _params(sem):
    return pltpu.CompilerParams(dimension_semantics=sem, vmem_limit_bytes=VMEM_LIMIT)


def _split3(a):
    hi = a.astype(BF16)
    r1 = a - hi.astype(F32)
    mid = r1.astype(BF16)
    lo = (r1 - mid.astype(F32)).astype(BF16)
    return hi, mid, lo


def _dot_exact_lhs(a_bf16, b, dims=None):
    if dims is None:
        dims = (((a_bf16.ndim - 1,), (0,)), ((), ()))
    out = None
    for part in _split3(b):
        term = lax.dot_general(a_bf16, part, dimension_numbers=dims, preferred_element_type=F32)
        out = term if out is None else out + term
    return out


NT_DIMS = (((1,), (1,)), ((), ()))
TN_DIMS = (((0,), (0,)), ((), ()))
NN_DIMS = (((1,), (0,)), ((), ()))


def _mod_kernel(c_ref, w_ref, b_ref, o_ref):
    o_ref[...] = jnp.dot(c_ref[...], w_ref[...], preferred_element_type=F32,
                         precision=HIGHEST) + b_ref[...]


def _adaln_mod(c, w_ada, b_ada):
    bsz, d = c.shape
    n = w_ada.shape[1]
    c8 = jnp.zeros((SUBLANES, d), F32).at[:bsz].set(c)
    out = pl.pallas_call(
        _mod_kernel,
        grid=(n // d,),
        in_specs=[pl.BlockSpec((SUBLANES, d), lambda j: (0, 0)),
                  pl.BlockSpec((d, d), lambda j: (0, j)),
                  pl.BlockSpec((1, d), lambda j: (0, j))],
        out_specs=pl.BlockSpec((SUBLANES, d), lambda j: (0, j)),
        out_shape=jax.ShapeDtypeStruct((SUBLANES, n), F32),
        compiler_params=_params(("arbitrary",)), name="adaln_mod",
    )(c8, w_ada, b_ada.reshape(1, n))
    return out[:bsz].reshape(bsz, n // d, d)


def _modulated_norm(x, norm_w, mod, sc_row, sh_row):
    ms = jnp.mean(x * x, axis=-1, keepdims=True)
    y = x * lax.rsqrt(ms + NORM_EPS) * norm_w
    return y * (1.0 + mod[sc_row:sc_row + 1, :]) + mod[sh_row:sh_row + 1, :]


def _nmm_kernel(x_ref, nw_ref, mod_ref, w_ref, o_ref, h_ref):
    @pl.when(pl.program_id(1) == 0)
    def _():
        h = _modulated_norm(x_ref[...], nw_ref[...], mod_ref[0], SC_M, SH_M)
        h_ref[...] = h.astype(BF16)

    o_ref[...] = jnp.dot(h_ref[...], w_ref[...],
                         preferred_element_type=F32).astype(o_ref.dtype)


def _norm_mod_matmul(x2, norm_w, mod, w, tn, out_dtype, seq):
    t, d = x2.shape
    n = w.shape[1]
    tiles_per_batch = seq // PROJ_TILE
    return pl.pallas_call(
        _nmm_kernel,
        grid=(t // PROJ_TILE, n // tn),
        in_specs=[pl.BlockSpec((PROJ_TILE, d), lambda i, j: (i, 0)),
                  pl.BlockSpec((1, d), lambda i, j: (0, 0)),
                  pl.BlockSpec((1, 6, d), lambda i, j: (i // tiles_per_batch, 0, 0)),
                  pl.BlockSpec((d, tn), lambda i, j: (0, j))],
        out_specs=[pl.BlockSpec((PROJ_TILE, tn), lambda i, j: (i, j)),
                   pl.BlockSpec((PROJ_TILE, d), lambda i, j: (i, 0))],
        out_shape=[jax.ShapeDtypeStruct((t, n), out_dtype), jax.ShapeDtypeStruct((t, d), BF16)],
        compiler_params=_params(("parallel", "arbitrary")), name="norm_mod_proj",
    )(x2, norm_w.reshape(1, d), mod, w)


def _mm_kernel(h_ref, w_ref, o_ref):
    o_ref[...] = jnp.dot(h_ref[...], w_ref[...], preferred_element_type=F32).astype(o_ref.dtype)


def _matmul(h, w, tn, out_dtype):
    t, d = h.shape
    n = w.shape[1]
    return pl.pallas_call(
        _mm_kernel,
        grid=(n // tn, t // PROJ_TILE),
        in_specs=[pl.BlockSpec((PROJ_TILE, d), lambda j, i: (i, 0)),
                  pl.BlockSpec((d, tn), lambda j, i: (0, j))],
        out_specs=pl.BlockSpec((PROJ_TILE, tn), lambda j, i: (i, j)),
        out_shape=jax.ShapeDtypeStruct((t, n), out_dtype),
        compiler_params=_params(("parallel", "parallel")), name="proj",
    )(h, w)


def _mm_conv_kernel(h_ref, w_ref, cw_ref, o_ref, u_ref, *, tiles_per_batch):
    tm = h_ref.shape[0]
    width = cw_ref.shape[0]

    @pl.when(pl.program_id(1) % tiles_per_batch == 0)
    def _():
        u_ref[0:SUBLANES, :] = jnp.zeros((SUBLANES, u_ref.shape[1]), F32)

    cw = cw_ref[...]
    n_sub = 4
    sub = tm // n_sub

    def matmul(s):
        u_ref[SUBLANES + s * sub:SUBLANES + (s + 1) * sub, :] = jnp.dot(
            h_ref[s * sub:(s + 1) * sub, :], w_ref[...], preferred_element_type=F32)

    def conv(s):
        base = SUBLANES + s * sub
        y = cw[width - 1:width, :] * u_ref[base:base + sub, :]
        for i in range(width - 1):
            off = base - (width - 1) + i
            y = y + cw[i:i + 1, :] * u_ref[off:off + sub, :]
        o_ref[s * sub:(s + 1) * sub, :] = y * jax.nn.sigmoid(y)

    matmul(0)
    for s in range(1, n_sub):
        matmul(s)
        conv(s - 1)
    conv(n_sub - 1)
    u_ref[0:SUBLANES, :] = u_ref[tm:tm + SUBLANES, :]


def _matmul_conv_silu(h, w, conv_w, tn, seq):
    t, d = h.shape
    n = w.shape[1]
    return pl.pallas_call(
        functools.partial(_mm_conv_kernel, tiles_per_batch=seq // PROJ_TILE),
        grid=(n // tn, t // PROJ_TILE),
        in_specs=[pl.BlockSpec((PROJ_TILE, d), lambda j, i: (i, 0)),
                  pl.BlockSpec((d, tn), lambda j, i: (0, j)),
                  pl.BlockSpec((conv_w.shape[0], tn), lambda j, i: (0, j))],
        out_specs=pl.BlockSpec((PROJ_TILE, tn), lambda j, i: (i, j)),
        out_shape=jax.ShapeDtypeStruct((t, n), F32),
        scratch_shapes=[pltpu.VMEM((PROJ_TILE + SUBLANES, tn), F32)],
        compiler_params=_params(("parallel", "arbitrary")), name="proj_conv_silu",
    )(h, w, conv_w)


def _compress_kernel(t_ref, wc_ref, pos_ref, w1_ref, b1_ref, w2_ref, b2_ref, eye_ref,
                     kc_ref, vcT_ref):
    n_half = t_ref.shape[1] // CMP_STRIDE
    acc = None
    for l in range(CMP_STRIDE):
        tok = t_ref[0, pl.ds(l, n_half, stride=CMP_STRIDE), :].astype(BF16)
        term = jnp.dot(tok, wc_ref[l], preferred_element_type=F32)
        acc = term if acc is None else acc + term
    outs = []
    for kind in range(2):
        top = acc[:, 2 * kind * CMP_HIDDEN:(2 * kind + 1) * CMP_HIDDEN]
        bot = acc[:, (2 * kind + 1) * CMP_HIDDEN:(2 * kind + 2) * CMP_HIDDEN]
        bot_next = pltpu.roll(bot, n_half - 1, 0)
        pos_term = jnp.dot(pos_ref[kind], w1_ref[kind], preferred_element_type=F32)[0:1]
        hid = jax.nn.gelu(top + bot_next + pos_term + b1_ref[kind])
        outs.append(jnp.dot(hid.astype(BF16), w2_ref[kind], preferred_element_type=F32)
                    + b2_ref[kind])
    kc_ref[0, 0] = outs[0].astype(BF16)
    dk = NSA_HEAD_DIM
    vcT_ref[0, 0] = lax.dot_general(eye_ref[:dk, :dk], outs[1].astype(BF16), NT_DIMS,
                                    preferred_element_type=F32).astype(BF16)


def _compress(pc3, wc, pos, w1, b1, w2, b2, eye, first_col_block):
    bsz, seq, _ = pc3.shape
    n_grp, dk = NSA_KV_HEADS, NSA_HEAD_DIM
    n_half = seq // CMP_STRIDE
    const3 = lambda b, g: (0, 0, 0)
    return pl.pallas_call(
        _compress_kernel,
        grid=(bsz, n_grp),
        in_specs=[pl.BlockSpec((1, seq, LANES), lambda b, g: (b, 0, first_col_block + g)),
                  pl.BlockSpec(wc.shape, const3),
                  pl.BlockSpec(pos.shape, const3),
                  pl.BlockSpec(w1.shape, const3),
                  pl.BlockSpec(b1.shape, const3),
                  pl.BlockSpec(w2.shape, const3),
                  pl.BlockSpec(b2.shape, const3),
                  pl.BlockSpec(eye.shape, lambda b, g: (0, 0))],
        out_specs=[pl.BlockSpec((1, 1, n_half, dk), lambda b, g: (b, g, 0, 0)),
                   pl.BlockSpec((1, 1, dk, n_half), lambda b, g: (b, g, 0, 0))],
        out_shape=[jax.ShapeDtypeStruct((bsz, n_grp, n_half, dk), BF16),
                   jax.ShapeDtypeStruct((bsz, n_grp, dk, n_half), BF16)],
        compiler_params=_params(("parallel", "parallel")), name="nsa_compress",
    )(pc3, wc, pos, w1, b1, w2, b2, eye)


def _nsa_attn_kernel(q_ref, ks_ref, kw_ref, v_ref, kc_ref, vcT_ref, ov_ref, gl_ref, eye_ref,
                     aux_ref, o_ref, *scratch):
    tq = NSA_TQ
    first_tile = pl.program_id(2) * NSA_TILES_PER_STEP

    def one_tile(i, carry):
        rows_i = pl.ds(pl.multiple_of(i * tq, tq), tq)
        _nsa_query_tile(first_tile + i, q_ref.at[:, rows_i, :], ks_ref, kw_ref, v_ref, kc_ref,
                        vcT_ref, ov_ref, gl_ref.at[:, rows_i, :], eye_ref, aux_ref,
                        o_ref.at[:, rows_i, :], *scratch)
        return carry

    lax.fori_loop(0, NSA_TILES_PER_STEP, one_tile, 0)


def _nsa_query_tile(qt, q_ref, ks_ref, kw_ref, v_ref, kc_ref, vcT_ref, ov_ref, gl_ref, eye_ref,
                    aux_ref, o_ref,
                    ksa_ref, kwa_ref, vsT_ref, vwT_ref, sel_ref, s_ref, m_ref, acc_ref,
                    sw_ref, mw_ref, accw_ref, part_ref, gate_ref):
    tq, tk = NSA_TQ, NSA_TK
    rows = NSA_REP * tq
    dk = NSA_HEAD_DIM
    grp = pl.program_id(1)
    qs = qt * tq
    seq = ks_ref.shape[1]
    eye = eye_ref[...]
    tq_per_tk = tk // tq

    @pl.when(qt == 0)
    def _():
        lane = lax.broadcasted_iota(jnp.int32, (tk, LANES), 1)
        lane_flag = jnp.where(lane == dk, 1.0, 0.0).astype(BF16)
        ones_rows = jnp.ones((NSA_AUX, tk), BF16)

        def build(j, carry):
            r0 = pl.multiple_of(j * tk, tk)
            ksa_ref[pl.ds(r0, tk), :] = ks_ref[0, pl.ds(r0, tk), :] + aux_ref[...]
            kwa_ref[pl.ds(r0, tk), :] = kw_ref[0, pl.ds(r0, tk), :] + lane_flag
            vT = lax.dot_general(eye[:LANES, :LANES], v_ref[0, pl.ds(r0, tk), :], NT_DIMS,
                                 preferred_element_type=F32).astype(BF16)
            vsT_ref[j, 0:dk, :] = vT[:dk]
            vsT_ref[j, dk:, :] = ones_rows
            for i in range(tq_per_tk):
                vwT_ref[j * tq_per_tk + i, 0:dk, :] = vT[dk:, i * tq:(i + 1) * tq]
                vwT_ref[j * tq_per_tk + i, dk:, :] = ones_rows[:, :tq]
            return carry

        lax.fori_loop(0, seq // tk, build, 0)

    q4 = lax.dot_general(eye[:NSA_REP * dk, :NSA_REP * dk], q_ref[0], NT_DIMS,
                         preferred_element_type=F32).astype(BF16)
    qT = jnp.concatenate([q4[r * dk:(r + 1) * dk, :] for r in range(NSA_REP)], axis=1)
    t_q = qs + lax.broadcasted_iota(jnp.int32, (1, tq), 1)
    t_row = jnp.concatenate([t_q] * NSA_REP, axis=1)
    zpad = jnp.zeros((LANES - dk - NSA_AUX, rows), BF16)

    def per_head(a):
        return jnp.concatenate([a] * NSA_REP, axis=1)

    gate_ref[...] = _dot_exact_lhs(eye[:LANES, :LANES], jax.nn.sigmoid(gl_ref[0]), NT_DIMS)
    gates = gate_ref[pl.ds(pl.multiple_of(grp * NSA_AUX, NSA_AUX), NSA_AUX), :]

    n_cmp = kc_ref.shape[2]
    s_c = jnp.dot(kc_ref[0, 0], qT, preferred_element_type=F32)
    c_end = lax.broadcasted_iota(jnp.int32, (n_cmp, 1), 0) * CMP_STRIDE + (CMP_BLK - 1)
    s_c = s_c + per_head(jnp.where(c_end <= t_q, 0.0, MASK_VALUE))
    mx_c = jnp.max(s_c, axis=0, keepdims=True)
    p_c = jnp.exp2(s_c - mx_c)
    l_c = jnp.sum(p_c, axis=0, keepdims=True)
    p_c = p_c * jnp.where(t_row >= CMP_BLK - 1, 1.0 / l_c, 0.0)
    o_c = jnp.dot(vcT_ref[0, 0], p_c.astype(BF16), preferred_element_type=F32)
    for r in range(NSA_REP):
        part_ref[r * dk:(r + 1) * dk, :] = gates[3 * r:3 * r + 1] * o_c[:, r * tq:(r + 1) * tq]

    p_sum = p_c[:, 0:tq]
    for r in range(1, NSA_REP):
        p_sum = p_sum + p_c[:, r * tq:(r + 1) * tq]
    ov = ov_ref[...]
    imp = _dot_exact_lhs(ov, p_sum)
    n_blk = ov.shape[0]
    j_blk = lax.broadcasted_iota(jnp.int32, (n_blk, 1), 0)
    cur = t_q // SEL_BLK
    forced = (j_blk == 0) | (j_blk == cur) | (j_blk == cur - 1)
    valid = j_blk * SEL_BLK <= t_q
    n_rank = N_SEL - 3
    cand = jnp.where(forced, -jnp.inf, jnp.where(valid, imp, -1.0))

    n_wt = WINDOW // tq + 1
    row0 = lax.broadcasted_iota(jnp.int32, (NSA_AUX, rows), 0) == 0
    a_k = lax.broadcasted_iota(jnp.int32, (tq, 1), 0)
    b_q = lax.broadcasted_iota(jnp.int32, (1, tq), 1)
    w_tiles = [jnp.maximum(qt - (n_wt - 1) + i, 0) for i in range(n_wt)]

    def win_scores(i):
        before_start = qt - (n_wt - 1) + i < 0
        flag = jnp.where(row0 & before_start, MASK_VALUE, 0.0).astype(BF16)
        rhs_w = jnp.concatenate([qT, flag, zpad], axis=0)
        k_tile = kwa_ref[pl.ds(pl.multiple_of(w_tiles[i] * tq, tq), tq), :]
        s = jnp.dot(k_tile, rhs_w, preferred_element_type=F32)
        if i == 0:
            s = s + per_head(jnp.where(a_k > b_q, 0.0, MASK_VALUE))
        if i == n_wt - 1:
            s = s + per_head(jnp.where(a_k <= b_q, 0.0, MASK_VALUE))
        sw_ref[i * tq:(i + 1) * tq, :] = s

    def win_max():
        mw_ref[...] = jnp.max(sw_ref[...], axis=0, keepdims=True)

    def win_pv(i):
        p = jnp.exp2(sw_ref[i * tq:(i + 1) * tq, :] - mw_ref[...]).astype(BF16)
        pv = jnp.dot(vwT_ref[w_tiles[i]], p, preferred_element_type=F32)
        if i == 0:
            accw_ref[...] = pv
        else:
            accw_ref[...] = accw_ref[...] + pv

    win_steps = ([functools.partial(win_scores, i) for i in range(n_wt)] + [win_max]
                 + [functools.partial(win_pv, i) for i in range(n_wt)])

    work = cand
    maxima = []
    for r in range(n_rank):
        mx = jnp.max(work, axis=0, keepdims=True)
        maxima.append(mx)
        work = jnp.where(work == mx, -jnp.inf, work)
        if r < len(win_steps):
            win_steps[r]()
    for step in win_steps[n_rank:]:
        step()

    ones_lhs = jnp.ones((SUBLANES, n_blk), BF16)

    def count(mask):
        return jnp.dot(ones_lhs, jnp.where(mask, 1.0, 0.0).astype(BF16),
                       preferred_element_type=F32)[0:1]

    thr = maxima[-1]
    for mx in reversed(maxima[:-1]):
        thr = jnp.where(count(cand >= mx) >= n_rank, mx, thr)
    ties = cand == thr
    below = (lax.broadcasted_iota(jnp.int32, (n_blk, n_blk), 1)
             < lax.broadcasted_iota(jnp.int32, (n_blk, n_blk), 0))
    tie_rank = jnp.dot(jnp.where(below, 1.0, 0.0).astype(BF16),
                       jnp.where(ties, 1.0, 0.0).astype(BF16), preferred_element_type=F32)
    picked = (cand > thr) | (ties & (tie_rank < n_rank - count(cand > thr)))
    sel_ref[0:n_blk, :] = jnp.where(forced | (picked & (cand >= 0.0)), 0.0, MASK_VALUE)
    sel_ref[n_blk:, :] = jnp.zeros((NSA_AUX, tq), F32)

    acc_w = accw_ref[...]
    o_w = acc_w[:dk] * (1.0 / acc_w[dk:dk + 1])
    for r in range(NSA_REP):
        part_ref[r * dk:(r + 1) * dk, :] = (part_ref[r * dk:(r + 1) * dk, :]
                                            + gates[3 * r + 2:3 * r + 3] * o_w[:, r * tq:(r + 1) * tq])

    blk_per_tile = tk // SEL_BLK
    n_full = qs // tk

    def scores(kt, slot, diagonal=False):
        sb = sel_ref[pl.ds(pl.multiple_of(kt * blk_per_tile, blk_per_tile), NSA_AUX), :]
        rhs = jnp.concatenate([qT, per_head(sb).astype(BF16), zpad], axis=0)
        k_tile = ksa_ref[pl.ds(pl.multiple_of(kt * tk, tk), tk), :]
        s = jnp.dot(k_tile, rhs, preferred_element_type=F32)
        if diagonal:
            kpos = kt * tk + lax.broadcasted_iota(jnp.int32, (tk, 1), 0)
            s = s + per_head(jnp.where(kpos <= t_q, 0.0, MASK_VALUE))
        s_ref[slot] = s

    def softmax_pv(kt, slot):
        m_i = m_ref[...]
        m_new = jnp.maximum(m_i, jnp.max(s_ref[slot], axis=0, keepdims=True))
        alpha = jnp.exp2(m_i - m_new)
        p = jnp.exp2(s_ref[slot] - m_new).astype(BF16)
        pv = jnp.dot(vsT_ref[kt], p, preferred_element_type=F32)
        acc_ref[...] = alpha * acc_ref[...] + pv
        m_ref[...] = m_new

    m_ref[...] = jnp.full((1, rows), MASK_VALUE, F32)
    acc_ref[...] = jnp.zeros((dk + NSA_AUX, rows), F32)
    scores(n_full, 0, diagonal=True)

    n_pairs = n_full // 2

    def slot0_tile(i):
        return jnp.where(i == 0, n_full, 2 * i - 1)

    def pair_step(i, carry):
        scores(2 * i, 1)
        softmax_pv(slot0_tile(i), 0)
        scores(2 * i + 1, 0)
        softmax_pv(2 * i, 1)
        return carry

    lax.fori_loop(0, n_pairs, pair_step, 0)

    @pl.when(n_full % 2 == 1)
    def _():
        scores(2 * n_pairs, 1)
        softmax_pv(slot0_tile(n_pairs), 0)
        softmax_pv(2 * n_pairs, 1)

    @pl.when(n_full % 2 == 0)
    def _():
        softmax_pv(slot0_tile(n_pairs), 0)

    acc_s = acc_ref[...]
    o_s = acc_s[:dk] * (1.0 / acc_s[dk:dk + 1])
    gates = gate_ref[pl.ds(pl.multiple_of(grp * NSA_AUX, NSA_AUX), NSA_AUX), :]
    out_t = jnp.concatenate(
        [part_ref[r * dk:(r + 1) * dk, :] + gates[3 * r + 1:3 * r + 2] * o_s[:, r * tq:(r + 1) * tq]
         for r in range(NSA_REP)], axis=0).astype(BF16)
    o_ref[0] = lax.dot_general(eye[:tq, :tq], out_t, NT_DIMS,
                               preferred_element_type=F32).astype(o_ref.dtype)


def _nsa_attn(p_att3, aux3, small_block, kc, vcT, overlap_t, eye, aux_s):
    bsz, seq, _ = p_att3.shape
    n_grp, dk = NSA_KV_HEADS, NSA_HEAD_DIM
    tq, tk = NSA_TQ, NSA_TK
    rows = NSA_REP * tq
    q_w = NSA_REP * dk
    n_cmp = kc.shape[2]
    n_blk = seq // SEL_BLK
    v_rows = dk + NSA_AUX
    n_wt = WINDOW // tq + 1
    first = NSA_HEADS * dk // LANES
    kv_spec = lambda off: pl.BlockSpec((1, seq, LANES), lambda b, g, t: (b, 0, first + off + g))
    step_rows = NSA_TILES_PER_STEP * tq
    return pl.pallas_call(
        _nsa_attn_kernel,
        grid=(bsz, n_grp, seq // step_rows),
        in_specs=[
            pl.BlockSpec((1, step_rows, q_w), lambda b, g, t: (b, t, g)),
            kv_spec(0), kv_spec(n_grp), kv_spec(2 * n_grp),
            pl.BlockSpec((1, 1, n_cmp, dk), lambda b, g, t: (b, g, 0, 0)),
            pl.BlockSpec((1, 1, dk, n_cmp), lambda b, g, t: (b, g, 0, 0)),
            pl.BlockSpec((n_blk, n_cmp), lambda b, g, t: (0, 0)),
            pl.BlockSpec((1, step_rows, LANES), lambda b, g, t: (b, t, small_block)),
            pl.BlockSpec(eye.shape, lambda b, g, t: (0, 0)),
            pl.BlockSpec((tk, LANES), lambda b, g, t: (0, 0)),
        ],
        out_specs=pl.BlockSpec((1, step_rows, q_w), lambda b, g, t: (b, t, g)),
        out_shape=jax.ShapeDtypeStruct((bsz, seq, NSA_HEADS * dk), BF16),
        scratch_shapes=[pltpu.VMEM((seq, LANES), BF16),
                        pltpu.VMEM((seq, LANES), BF16),
                        pltpu.VMEM((seq // tk, v_rows, tk), BF16),
                        pltpu.VMEM((seq // tq, v_rows, tq), BF16),
                        pltpu.VMEM((n_blk + NSA_AUX, tq), F32),
                        pltpu.VMEM((2, tk, rows), F32),
                        pltpu.VMEM((1, rows), F32),
                        pltpu.VMEM((v_rows, rows), F32),
                        pltpu.VMEM((n_wt * tq, rows), F32),
                        pltpu.VMEM((1, rows), F32),
                        pltpu.VMEM((v_rows, rows), F32),
                        pltpu.VMEM((q_w, tq), F32),
                        pltpu.VMEM((LANES, tq), F32)],
        compiler_params=_params(("parallel", "parallel", "arbitrary")), name="nsa_attention",
    )(p_att3, p_att3, p_att3, p_att3, kc, vcT, overlap_t, aux3, eye, aux_s)


def _bmm(a, b, dims):
    a = a.astype(BF16)
    b = b.astype(BF16)
    return jnp.stack([lax.dot_general(a[n], b[n], dims, preferred_element_type=F32)
                      for n in range(a.shape[0])])


def _gdn_kernel(y_ref, z_ref, small_ref, alog_ref, dtb_ref, nw_ref, o_ref, state_ref):
    bsz = y_ref.shape[0]
    cs = GDN_CHUNK
    hd = GDN_HEAD_DIM
    nh = GDN_HEADS
    c = pl.program_id(0)

    @pl.when(c == 0)
    def _():
        state_ref[...] = jnp.zeros_like(state_ref)

    y = y_ref[...]

    small = small_ref[...]
    sp_in = small + dtb_ref[...]
    softplus = jnp.maximum(sp_in, 0.0) + jnp.log1p(jnp.exp(-jnp.abs(sp_in)))
    g_all = -jnp.exp(alog_ref[...]) * softplus
    beta_all = jax.nn.sigmoid(small)
    ii = lax.broadcasted_iota(jnp.int32, (cs, cs), 0)
    jj = lax.broadcasted_iota(jnp.int32, (cs, cs), 1)
    incl = ii >= jj
    strict = ii > jj
    tril = jnp.where(incl, 1.0, 0.0).astype(BF16)
    e128 = jnp.where(lax.broadcasted_iota(jnp.int32, (LANES, LANES), 0)
                     == lax.broadcasted_iota(jnp.int32, (LANES, LANES), 1), 1.0, 0.0).astype(BF16)
    gc_all = [_dot_exact_lhs(tril, g_all[b]) for b in range(bsz)]
    gcT_all = [_dot_exact_lhs(e128, gc_all[b], NT_DIMS) for b in range(bsz)]

    chains = [(b, h) for b in range(bsz) for h in range(nh)]

    def heads(a, base):
        return jnp.stack([a[b][:, base + h * hd:base + (h + 1) * hd] for b, h in chains])

    def lane_bcast(cols, base):
        return jnp.stack([jnp.broadcast_to(cols[b][:, base + h:base + h + 1], (cs, hd))
                          for b, h in chains])

    n_qk = nh * hd
    q = heads(y, 0)
    k = heads(y, n_qk)
    v = heads(y, 2 * n_qk)
    q = q * lax.rsqrt(jnp.sum(q * q, axis=-1, keepdims=True) + NORM_EPS) * (hd ** -0.5)
    k = k * lax.rsqrt(jnp.sum(k * k, axis=-1, keepdims=True) + NORM_EPS)
    gc = lane_bcast(gc_all, SMALL_A)
    beta = lane_bcast([beta_all[b] for b in range(bsz)], SMALL_B)
    gc_row = jnp.stack([gcT_all[b][SMALL_A + h:SMALL_A + h + 1, :] for b, h in chains])

    decay = jnp.exp(jnp.where(incl, gc[:, :, :cs] - gc_row, MASK_VALUE))
    kk = _bmm(k, k, NT_DIMS)
    m_full = jnp.where(strict, -(beta[:, :, :cs] * kk * decay), 0.0)
    blk_diff = ii ^ jj
    inv_m1 = jnp.where(blk_diff < 2, m_full, 0.0)
    size = 2
    while size < cs:
        m_off = jnp.where((blk_diff >= size) & (blk_diff < 2 * size), m_full, 0.0)
        t_mat = m_off + _bmm(inv_m1, m_off, NN_DIMS)
        inv_m1 = inv_m1 + t_mat + _bmm(t_mat, inv_m1, NN_DIMS)
        size *= 2
    e_gc = jnp.exp(gc)
    rhs = jnp.concatenate([v * beta, k * (beta * e_gc)], axis=2)
    sol = rhs + _bmm(inv_m1, rhs, NN_DIMS)
    u, w = sol[:, :, :hd], sol[:, :, hd:]
    qk = _bmm(q, k, NT_DIMS) * decay
    q_dec = q * e_gc
    gc_last = gc[:, cs - 1:cs, :]
    k_dec = k * jnp.exp(gc_last - gc)
    state = state_ref[...]
    v_new = u - _bmm(w, state, NN_DIMS)
    o = _bmm(q_dec, state, NN_DIMS) + _bmm(qk, v_new, NN_DIMS)
    state_ref[...] = state * jnp.exp(gc_last) + _bmm(k_dec, v_new, TN_DIMS)

    o = o * lax.rsqrt(jnp.mean(o * o, axis=-1, keepdims=True) + NORM_EPS) * nw_ref[...]
    z = heads(z_ref[...].astype(F32), 0)
    o = o * (z * jax.nn.sigmoid(z))
    for b in range(bsz):
        o_ref[b] = jnp.concatenate([o[b * nh + h] for h in range(nh)], axis=1).astype(o_ref.dtype)


def _gated_deltanet(qkv3, zm3, aux3, small_block, alog_pad, dtb_pad, norm_w):
    bsz, seq, qkv_cols = qkv3.shape
    cs = GDN_CHUNK
    z_cols = GDN_HEADS * GDN_HEAD_DIM
    out = pl.pallas_call(
        _gdn_kernel,
        grid=(seq // cs,),
        in_specs=[
            pl.BlockSpec((bsz, cs, qkv_cols), lambda c: (0, c, 0)),
            pl.BlockSpec((bsz, cs, z_cols), lambda c: (0, c, 0)),
            pl.BlockSpec((bsz, cs, LANES), lambda c: (0, c, small_block)),
            pl.BlockSpec((1, LANES), lambda c: (0, 0)),
            pl.BlockSpec((1, LANES), lambda c: (0, 0)),
            pl.BlockSpec((1, GDN_HEAD_DIM), lambda c: (0, 0)),
        ],
        out_specs=pl.BlockSpec((bsz, cs, z_cols), lambda c: (0, c, 0)),
        out_shape=jax.ShapeDtypeStruct((bsz, seq, z_cols), BF16),
        scratch_shapes=[pltpu.VMEM((bsz * GDN_HEADS, GDN_HEAD_DIM, GDN_HEAD_DIM), F32)],
        compiler_params=_params(("arbitrary",)), name="gated_deltanet",
    )(qkv3, zm3, aux3, alog_pad, dtb_pad, norm_w.reshape(1, GDN_HEAD_DIM))
    return out.reshape(bsz * seq, z_cols)


def _merge_kernel(ya_ref, yb_ref, ma_ref, mb_ref, x_ref, mod_ref, wa_ref, wb_ref, wo_ref,
                  nf_ref, x1_ref, h2_ref):
    pa = jnp.dot(ya_ref[...], wa_ref[...], preferred_element_type=F32)
    pb = jnp.dot(yb_ref[...], wb_ref[...], preferred_element_type=F32)
    mixed = (jax.nn.sigmoid(ma_ref[...].astype(F32)) * pa
             + jax.nn.sigmoid(mb_ref[...].astype(F32)) * pb)
    mod = mod_ref[0]
    x1 = x_ref[...] + mod[GT_M:GT_M + 1, :] * jnp.dot(
        mixed.astype(BF16), wo_ref[...], preferred_element_type=F32)
    x1_ref[...] = x1
    h2_ref[...] = _modulated_norm(x1, nf_ref[...], mod, SC_F, SH_F).astype(BF16)


def _merge_project(ya, yb, pc, merge_blk, x2, mod, w_pa, w_pb, w_o, norm_ffn, seq):
    t, d = x2.shape
    tiles_per_batch = seq // ROW_TILE
    row = lambda i: (i, 0)
    const = lambda i: (0, 0)
    return pl.pallas_call(
        _merge_kernel,
        grid=(t // ROW_TILE,),
        in_specs=[pl.BlockSpec((ROW_TILE, d), row),
                  pl.BlockSpec((ROW_TILE, d), row),
                  pl.BlockSpec((ROW_TILE, d), lambda i: (i, merge_blk)),
                  pl.BlockSpec((ROW_TILE, d), lambda i: (i, merge_blk + 1)),
                  pl.BlockSpec((ROW_TILE, d), row),
                  pl.BlockSpec((1, 6, d), lambda i: (i // tiles_per_batch, 0, 0)),
                  pl.BlockSpec((d, d), const),
                  pl.BlockSpec((d, d), const),
                  pl.BlockSpec((d, d), const),
                  pl.BlockSpec((1, d), const)],
        out_specs=[pl.BlockSpec((ROW_TILE, d), row), pl.BlockSpec((ROW_TILE, d), row)],
        out_shape=[jax.ShapeDtypeStruct((t, d), F32), jax.ShapeDtypeStruct((t, d), BF16)],
        compiler_params=_params(("parallel",)), name="merge_project",
    )(ya, yb, pc, pc, x2, mod, w_pa, w_pb, w_o, norm_ffn.reshape(1, d))


def _ffn_kernel(h_ref, halo_ref, res_ref, mod_ref, wg_ref, wv_ref, cwg_ref, cwv_ref,
                cbg_ref, cbv_ref, wd_ref, nfin_ref, o_ref, ug_ref, uv_ref,
                *, tiles_per_batch, last):
    i = pl.program_id(0)
    tm = h_ref.shape[0]
    first_of_batch = (i % tiles_per_batch) == 0
    h = h_ref[...]
    halo = halo_ref[...]

    def conv_half(w_ref, cw_ref, cb_ref, u_ref):
        u_ref[SUBLANES:, :] = jnp.dot(h, w_ref[...], preferred_element_type=F32)
        up_halo = jnp.dot(halo, w_ref[...], preferred_element_type=F32)[SUBLANES:]
        u_ref[0:SUBLANES, :] = jnp.where(first_of_batch, 0.0, up_halo)
        cw = cw_ref[...]
        out = cb_ref[...] + cw[FFN_CONV - 1:FFN_CONV, :] * u_ref[SUBLANES:, :]
        for j in range(FFN_CONV - 1):
            off = SUBLANES - (FFN_CONV - 1) + j
            out = out + cw[j:j + 1, :] * u_ref[off:off + tm, :]
        return out

    gate = conv_half(wg_ref, cwg_ref, cbg_ref, ug_ref)
    val = conv_half(wv_ref, cwv_ref, cbv_ref, uv_ref)
    act = gate * jax.nn.sigmoid(gate) * val
    down = jnp.dot(act.astype(BF16), wd_ref[...], preferred_element_type=F32)
    y = res_ref[...] + mod_ref[0][GT_F:GT_F + 1, :] * down
    if last:
        ms = jnp.mean(y * y, axis=-1, keepdims=True)
        y = y * lax.rsqrt(ms + NORM_EPS) * nfin_ref[...]
    o_ref[...] = y


def _conv_ffn(h2, x1, mod, w_up, conv_w, conv_b, w_down, norm_final, seq):
    t, d = x1.shape
    tm, tf = ROW_TILE, FFN_TF
    n_f = FFN_DIM // tf
    tiles_per_batch = seq // tm
    halo_per_tile = tm // SUBLANES
    row = lambda i: (i, 0)
    res = x1
    for f in range(n_f):
        col = lambda i, c=f: (0, c)
        col_val = lambda i, c=n_f + f: (0, c)
        res = pl.pallas_call(
            functools.partial(_ffn_kernel, tiles_per_batch=tiles_per_batch, last=f == n_f - 1),
            grid=(t // tm,),
            in_specs=[pl.BlockSpec((tm, d), row),
                      pl.BlockSpec((SUBLANES * 2, d),
                                   lambda i: (jnp.maximum(i * (halo_per_tile // 2) - 1, 0), 0)),
                      pl.BlockSpec((tm, d), row),
                      pl.BlockSpec((1, 6, d), lambda i: (i // tiles_per_batch, 0, 0)),
                      pl.BlockSpec((d, tf), col),
                      pl.BlockSpec((d, tf), col_val),
                      pl.BlockSpec((FFN_CONV, tf), col),
                      pl.BlockSpec((FFN_CONV, tf), col_val),
                      pl.BlockSpec((1, tf), col),
                      pl.BlockSpec((1, tf), col_val),
                      pl.BlockSpec((tf, d), lambda i, c=f: (c, 0)),
                      pl.BlockSpec((1, d), lambda i: (0, 0))],
            out_specs=pl.BlockSpec((tm, d), row),
            out_shape=jax.ShapeDtypeStruct((t, d), F32),
            scratch_shapes=[pltpu.VMEM((tm + SUBLANES, tf), F32),
                            pltpu.VMEM((tm + SUBLANES, tf), F32)],
            compiler_params=_params(("parallel",)), name="conv_ffn",
        )(h2, h2, res, mod, w_up, w_up, conv_w, conv_w, conv_b, conv_b, w_down,
          norm_final.reshape(1, d))
    return res


def _overlap_t(seq):
    n_cmp = (seq - CMP_BLK) // CMP_STRIDE + 1
    n_blk = seq // SEL_BLK
    c_start = np.arange(n_cmp) * CMP_STRIDE
    s_start = np.arange(n_blk) * SEL_BLK
    ov = np.clip(np.minimum(c_start[:, None] + CMP_BLK, s_start[None, :] + SEL_BLK)
                 - np.maximum(c_start[:, None], s_start[None, :]), 0, None) / CMP_BLK
    n_half = seq // CMP_STRIDE
    out = np.zeros((n_blk, n_half), np.float32)
    out[:, :n_cmp] = ov.T
    return jnp.asarray(out, BF16)


def kernel(x, c, w_ada, b_ada, norm_mix, w_in, nsa_pos_k, nsa_pos_v, nsa_ck_w1, nsa_ck_b1, nsa_ck_w2, nsa_ck_b2, nsa_cv_w1, nsa_cv_b1, nsa_cv_w2, nsa_cv_b2, gdn_conv, gdn_a_log, gdn_dt_bias, gdn_norm, w_proj_nsa, w_proj_gdn, w_out, norm_ffn, ffn_up, ffn_conv, ffn_conv_b, ffn_down, norm_final):
    bsz, seq, d = x.shape
    depth = w_ada.shape[0]
    n_grp, dk = NSA_KV_HEADS, NSA_HEAD_DIM
    q_cols = NSA_HEADS * dk
    gqkv_cols = 3 * GDN_HEADS * GDN_HEAD_DIM
    z_cols = GDN_HEADS * GDN_HEAD_DIM
    o_kv = q_cols
    o_gl = o_kv + 6 * n_grp * dk
    o_gq = o_gl + 3 * NSA_HEADS
    o_a = o_gq + gqkv_cols
    o_z = o_a + 2 * GDN_HEADS
    overlap_t = _overlap_t(seq)
    x2 = x.reshape(bsz * seq, d)
    assert depth == 1, "kernel supports the problem's DEPTH == 1"
    for l in range(depth):
        mod = _adaln_mod(c, w_ada[l], b_ada[l])
        w = w_in[l]
        kv_w = w[:, o_kv:o_gl].reshape(d, 6, n_grp, dk)
        zero = jnp.zeros((d, n_grp, dk), F32)
        pair = lambda a, b: jnp.concatenate([a, b], axis=2).reshape(d, n_grp * 2 * dk)
        w_att = jnp.concatenate([w[:, :o_kv] * (dk ** -0.5 * LOG2_E), pair(kv_w[:, 2], zero),
                                 pair(kv_w[:, 4], zero), pair(kv_w[:, 3], kv_w[:, 5])],
                                axis=1).astype(BF16)
        gate_w = jnp.pad(w[:, o_gl:o_gq].reshape(d, n_grp, 3 * NSA_REP),
                         ((0, 0), (0, 0), (0, NSA_AUX - 3 * NSA_REP))).reshape(d, SMALL_A)
        w_small = jnp.concatenate([gate_w, w[:, o_a:o_z],
                                   jnp.zeros((d, LANES - SMALL_B - GDN_HEADS), F32)],
                                  axis=1).astype(BF16)
        w_aux = jnp.concatenate([pair(kv_w[:, 0], kv_w[:, 1]).astype(BF16), w_small], axis=1)
        small_block = n_grp
        w_zm = w[:, o_z:].astype(BF16)

        p_att, h = _norm_mod_matmul(x2, norm_mix[l], mod, w_att, 1280, BF16, seq)
        qkv = _matmul_conv_silu(h, w[:, o_gq:o_a].astype(BF16), gdn_conv[l], 1536, seq)
        zm = _matmul(h, w_zm, 1536, BF16)
        aux = _matmul(h, w_aux, w_aux.shape[1], F32)
        aux3 = aux.reshape(bsz, seq, aux.shape[1])

        w1k = nsa_ck_w1[l].reshape(2, CMP_STRIDE, dk, CMP_HIDDEN)
        w1v = nsa_cv_w1[l].reshape(2, CMP_STRIDE, dk, CMP_HIDDEN)
        zw = jnp.zeros((CMP_STRIDE, dk, 2 * CMP_HIDDEN), F32)
        wc = jnp.concatenate(
            [jnp.concatenate([w1k[0], w1k[1], zw], axis=2),
             jnp.concatenate([zw, w1v[0], w1v[1]], axis=2)], axis=1).astype(BF16)
        pos = jnp.stack([nsa_pos_k[l], nsa_pos_v[l]]).reshape(2, 1, CMP_BLK * dk)
        pos = jnp.broadcast_to(pos, (2, 2 * SUBLANES, CMP_BLK * dk)).astype(BF16)
        cw1 = jnp.stack([nsa_ck_w1[l], nsa_cv_w1[l]]).astype(BF16)
        cb1 = jnp.stack([nsa_ck_b1[l], nsa_cv_b1[l]]).reshape(2, 1, CMP_HIDDEN)
        cw2 = jnp.stack([nsa_ck_w2[l], nsa_cv_w2[l]]).astype(BF16)
        cb2 = jnp.stack([nsa_ck_b2[l], nsa_cv_b2[l]]).reshape(2, 1, dk)
        eye = jnp.eye(max(NSA_REP * dk, NSA_TQ), dtype=BF16)
        kc, vcT = _compress(aux3, wc, pos, cw1, cb1, cw2, cb2, eye, 0)

        blk_in_tile = np.arange(NSA_TK) // SEL_BLK
        aux_s = np.zeros((NSA_TK, LANES), np.float32)
        aux_s[np.arange(NSA_TK), dk + blk_in_tile] = 1.0
        y_a = _nsa_attn(p_att.reshape(bsz, seq, p_att.shape[1]), aux3, small_block, kc, vcT,
                        overlap_t, eye, jnp.asarray(aux_s, BF16)).reshape(bsz * seq, q_cols)

        alog_pad = jnp.zeros((1, LANES), F32).at[0, SMALL_A:SMALL_A + GDN_HEADS].set(gdn_a_log[l])
        dtb_pad = jnp.zeros((1, LANES), F32).at[0, SMALL_A:SMALL_A + GDN_HEADS].set(gdn_dt_bias[l])
        y_b = _gated_deltanet(qkv.reshape(bsz, seq, gqkv_cols), zm.reshape(bsz, seq, zm.shape[1]),
                              aux3, small_block, alog_pad, dtb_pad, gdn_norm[l])

        x1, h2 = _merge_project(y_a, y_b, zm, z_cols // d, x2, mod, w_proj_nsa[l].astype(BF16),
                                w_proj_gdn[l].astype(BF16), w_out[l].astype(BF16),
                                norm_ffn[l], seq)
        out = _conv_ffn(h2, x1, mod, ffn_up[l].astype(BF16), ffn_conv[l],
                        ffn_conv_b[l].reshape(1, 2 * FFN_DIM), ffn_down[l].astype(BF16),
                        norm_final, seq)
    return out.reshape(bsz, seq, d)
```

```python
import functools

import numpy as np
import jax
import jax.numpy as jnp
from jax import lax
from jax.experimental import pallas as pl
from jax.experimental.pallas import tpu as pltpu

F32 = jnp.float32
BF16 = jnp.bfloat16
HIGHEST = lax.Precision.HIGHEST

NSA_HEAD_DIM = 64
NSA_HEADS = 16
NSA_KV_HEADS = 4
NSA_REP = NSA_HEADS // NSA_KV_HEADS
CMP_BLK = 32
CMP_STRIDE = 16
CMP_HIDDEN = 256
SEL_BLK = 64
N_SEL = 16
WINDOW = 512
GDN_HEAD_DIM = 128
GDN_HEADS = 8
GDN_CONV = 4
GDN_CHUNK = 64
FFN_DIM = 2816
FFN_CONV = 3
NORM_EPS = 1e-6
MASK_VALUE = -1e30

LANES = 128
SUBLANES = 8
VMEM_LIMIT = 56 * 1024 * 1024

NSA_TQ = 256
NSA_TILES_PER_STEP = 4
NSA_TK = 512
NSA_AUX = 16
LOG2_E = 1.4426950408889634
PROJ_TILE = 1024
ROW_TILE = 512
FFN_TF = 1408

SH_M, SC_M, GT_M, SH_F, SC_F, GT_F = range(6)
SMALL_A = NSA_KV_HEADS * NSA_AUX
SMALL_B = SMALL_A + GDN_HEADS


def _params(sem):
    return pltpu.CompilerParams(dimension_semantics=sem, vmem_limit_bytes=VMEM_LIMIT)


def _split3(a):
    hi = a.astype(BF16)
    r1 = a - hi.astype(F32)
    mid = r1.astype(BF16)
    lo = (r1 - mid.astype(F32)).astype(BF16)
    return hi, mid, lo


def _dot_exact_lhs(a_bf16, b, dims=None):
    if dims is None:
        dims = (((a_bf16.ndim - 1,), (0,)), ((), ()))
    out = None
    for part in _split3(b):
        term = lax.dot_general(a_bf16, part, dimension_numbers=dims, preferred_element_type=F32)
        out = term if out is None else out + term
    return out


NT_DIMS = (((1,), (1,)), ((), ()))
TN_DIMS = (((0,), (0,)), ((), ()))
NN_DIMS = (((1,), (0,)), ((), ()))


def _mod_kernel(c_ref, w_ref, b_ref, o_ref):
    o_ref[...] = jnp.dot(c_ref[...], w_ref[...], preferred_element_type=F32,
                         precision=HIGHEST) + b_ref[...]


def _adaln_mod(c, w_ada, b_ada):
    bsz, d = c.shape
    n = w_ada.shape[1]
    c8 = jnp.zeros((SUBLANES, d), F32).at[:bsz].set(c)
    out = pl.pallas_call(
        _mod_kernel,
        grid=(n // d,),
        in_specs=[pl.BlockSpec((SUBLANES, d), lambda j: (0, 0)),
                  pl.BlockSpec((d, d), lambda j: (0, j)),
                  pl.BlockSpec((1, d), lambda j: (0, j))],
        out_specs=pl.BlockSpec((SUBLANES, d), lambda j: (0, j)),
        out_shape=jax.ShapeDtypeStruct((SUBLANES, n), F32),
        compiler_params=_params(("arbitrary",)), name="adaln_mod",
    )(c8, w_ada, b_ada.reshape(1, n))
    return out[:bsz].reshape(bsz, n // d, d)


def _modulated_norm(x, norm_w, mod, sc_row, sh_row):
    ms = jnp.mean(x * x, axis=-1, keepdims=True)
    y = x * lax.rsqrt(ms + NORM_EPS) * norm_w
    return y * (1.0 + mod[sc_row:sc_row + 1, :]) + mod[sh_row:sh_row + 1, :]


def _nmm_kernel(x_ref, nw_ref, mod_ref, w_ref, o_ref, h_ref):
    @pl.when(pl.program_id(1) == 0)
    def _():
        h = _modulated_norm(x_ref[...], nw_ref[...], mod_ref[0], SC_M, SH_M)
        h_ref[...] = h.astype(BF16)

    o_ref[...] = jnp.dot(h_ref[...], w_ref[...],
                         preferred_element_type=F32).astype(o_ref.dtype)


def _norm_mod_matmul(x2, norm_w, mod, w, tn, out_dtype, seq):
    t, d = x2.shape
    n = w.shape[1]
    tiles_per_batch = seq // PROJ_TILE
    return pl.pallas_call(
        _nmm_kernel,
        grid=(t // PROJ_TILE, n // tn),
        in_specs=[pl.BlockSpec((PROJ_TILE, d), lambda i, j: (i, 0)),
                  pl.BlockSpec((1, d), lambda i, j: (0, 0)),
                  pl.BlockSpec((1, 6, d), lambda i, j: (i // tiles_per_batch, 0, 0)),
                  pl.BlockSpec((d, tn), lambda i, j: (0, j))],
        out_specs=[pl.BlockSpec((PROJ_TILE, tn), lambda i, j: (i, j)),
                   pl.BlockSpec((PROJ_TILE, d), lambda i, j: (i, 0))],
        out_shape=[jax.ShapeDtypeStruct((t, n), out_dtype), jax.ShapeDtypeStruct((t, d), BF16)],
        compiler_params=_params(("parallel", "arbitrary")), name="norm_mod_proj",
    )(x2, norm_w.reshape(1, d), mod, w)


def _mm_kernel(h_ref, w_ref, o_ref):
    o_ref[...] = jnp.dot(h_ref[...], w_ref[...], preferred_element_type=F32).astype(o_ref.dtype)


def _matmul(h, w, tn, out_dtype):
    t, d = h.shape
    n = w.shape[1]
    return pl.pallas_call(
        _mm_kernel,
        grid=(n // tn, t // PROJ_TILE),
        in_specs=[pl.BlockSpec((PROJ_TILE, d), lambda j, i: (i, 0)),
                  pl.BlockSpec((d, tn), lambda j, i: (0, j))],
        out_specs=pl.BlockSpec((PROJ_TILE, tn), lambda j, i: (i, j)),
        out_shape=jax.ShapeDtypeStruct((t, n), out_dtype),
        compiler_params=_params(("parallel", "parallel")), name="proj",
    )(h, w)


def _mm_conv_kernel(h_ref, w_ref, cw_ref, o_ref, u_ref, *, tiles_per_batch):
    tm = h_ref.shape[0]
    width = cw_ref.shape[0]

    @pl.when(pl.program_id(1) % tiles_per_batch == 0)
    def _():
        u_ref[0:SUBLANES, :] = jnp.zeros((SUBLANES, u_ref.shape[1]), F32)

    cw = cw_ref[...]
    n_sub = 4
    sub = tm // n_sub

    def matmul(s):
        u_ref[SUBLANES + s * sub:SUBLANES + (s + 1) * sub, :] = jnp.dot(
            h_ref[s * sub:(s + 1) * sub, :], w_ref[...], preferred_element_type=F32)

    def conv(s):
        base = SUBLANES + s * sub
        y = cw[width - 1:width, :] * u_ref[base:base + sub, :]
        for i in range(width - 1):
            off = base - (width - 1) + i
            y = y + cw[i:i + 1, :] * u_ref[off:off + sub, :]
        o_ref[s * sub:(s + 1) * sub, :] = y * jax.nn.sigmoid(y)

    matmul(0)
    for s in range(1, n_sub):
        matmul(s)
        conv(s - 1)
    conv(n_sub - 1)
    u_ref[0:SUBLANES, :] = u_ref[tm:tm + SUBLANES, :]


def _matmul_conv_silu(h, w, conv_w, tn, seq):
    t, d = h.shape
    n = w.shape[1]
    return pl.pallas_call(
        functools.partial(_mm_conv_kernel, tiles_per_batch=seq // PROJ_TILE),
        grid=(n // tn, t // PROJ_TILE),
        in_specs=[pl.BlockSpec((PROJ_TILE, d), lambda j, i: (i, 0)),
                  pl.BlockSpec((d, tn), lambda j, i: (0, j)),
                  pl.BlockSpec((conv_w.shape[0], tn), lambda j, i: (0, j))],
        out_specs=pl.BlockSpec((PROJ_TILE, tn), lambda j, i: (i, j)),
        out_shape=jax.ShapeDtypeStruct((t, n), F32),
        scratch_shapes=[pltpu.VMEM((PROJ_TILE + SUBLANES, tn), F32)],
        compiler_params=_params(("parallel", "arbitrary")), name="proj_conv_silu",
    )(h, w, conv_w)


def _compress_kernel(t_ref, wc_ref, pos_ref, w1_ref, b1_ref, w2_ref, b2_ref, eye_ref,
                     kc_ref, vcT_ref):
    n_half = t_ref.shape[1] // CMP_STRIDE
    acc = None
    for l in range(CMP_STRIDE):
        tok = t_ref[0, pl.ds(l, n_half, stride=CMP_STRIDE), :].astype(BF16)
        term = jnp.dot(tok, wc_ref[l], preferred_element_type=F32)
        acc = term if acc is None else acc + term
    outs = []
    for kind in range(2):
        top = acc[:, 2 * kind * CMP_HIDDEN:(2 * kind + 1) * CMP_HIDDEN]
        bot = acc[:, (2 * kind + 1) * CMP_HIDDEN:(2 * kind + 2) * CMP_HIDDEN]
        bot_next = pltpu.roll(bot, n_half - 1, 0)
        pos_term = jnp.dot(pos_ref[kind], w1_ref[kind], preferred_element_type=F32)[0:1]
        hid = jax.nn.gelu(top + bot_next + pos_term + b1_ref[kind])
        outs.append(jnp.dot(hid.astype(BF16), w2_ref[kind], preferred_element_type=F32)
                    + b2_ref[kind])
    kc_ref[0, 0] = outs[0].astype(BF16)
    dk = NSA_HEAD_DIM
    vcT_ref[0, 0] = lax.dot_general(eye_ref[:dk, :dk], outs[1].astype(BF16), NT_DIMS,
                                    preferred_element_type=F32).astype(BF16)


def _compress(pc3, wc, pos, w1, b1, w2, b2, eye, first_col_block):
    bsz, seq, _ = pc3.shape
    n_grp, dk = NSA_KV_HEADS, NSA_HEAD_DIM
    n_half = seq // CMP_STRIDE
    const3 = lambda b, g: (0, 0, 0)
    return pl.pallas_call(
        _compress_kernel,
        grid=(bsz, n_grp),
        in_specs=[pl.BlockSpec((1, seq, LANES), lambda b, g: (b, 0, first_col_block + g)),
                  pl.BlockSpec(wc.shape, const3),
                  pl.BlockSpec(pos.shape, const3),
                  pl.BlockSpec(w1.shape, const3),
                  pl.BlockSpec(b1.shape, const3),
                  pl.BlockSpec(w2.shape, const3),
                  pl.BlockSpec(b2.shape, const3),
                  pl.BlockSpec(eye.shape, lambda b, g: (0, 0))],
        out_specs=[pl.BlockSpec((1, 1, n_half, dk), lambda b, g: (b, g, 0, 0)),
                   pl.BlockSpec((1, 1, dk, n_half), lambda b, g: (b, g, 0, 0))],
        out_shape=[jax.ShapeDtypeStruct((bsz, n_grp, n_half, dk), BF16),
                   jax.ShapeDtypeStruct((bsz, n_grp, dk, n_half), BF16)],
        compiler_params=_params(("parallel", "parallel")), name="nsa_compress",
    )(pc3, wc, pos, w1, b1, w2, b2, eye)


def _nsa_attn_kernel(q_ref, ks_ref, kw_ref, v_ref, kc_ref, vcT_ref, ov_ref, gl_ref, eye_ref,
                     aux_ref, o_ref, *scratch):
    tq = NSA_TQ
    first_tile = pl.program_id(2) * NSA_TILES_PER_STEP

    def one_tile(i, carry):
        rows_i = pl.ds(pl.multiple_of(i * tq, tq), tq)
        _nsa_query_tile(first_tile + i, q_ref.at[:, rows_i, :], ks_ref, kw_ref, v_ref, kc_ref,
                        vcT_ref, ov_ref, gl_ref.at[:, rows_i, :], eye_ref, aux_ref,
                        o_ref.at[:, rows_i, :], *scratch)
        return carry

    lax.fori_loop(0, NSA_TILES_PER_STEP, one_tile, 0)


def _nsa_query_tile(qt, q_ref, ks_ref, kw_ref, v_ref, kc_ref, vcT_ref, ov_ref, gl_ref, eye_ref,
                    aux_ref, o_ref,
                    ksa_ref, kwa_ref, vsT_ref, vwT_ref, sel_ref, s_ref, m_ref, acc_ref,
                    sw_ref, mw_ref, accw_ref, part_ref, gate_ref):
    tq, tk = NSA_TQ, NSA_TK
    rows = NSA_REP * tq
    dk = NSA_HEAD_DIM
    grp = pl.program_id(1)
    qs = qt * tq
    seq = ks_ref.shape[1]
    eye = eye_ref[...]
    tq_per_tk = tk // tq

    @pl.when(qt == 0)
    def _():
        lane = lax.broadcasted_iota(jnp.int32, (tk, LANES), 1)
        lane_flag = jnp.where(lane == dk, 1.0, 0.0).astype(BF16)
        ones_rows = jnp.ones((NSA_AUX, tk), BF16)

        def build(j, carry):
            r0 = pl.multiple_of(j * tk, tk)
            ksa_ref[pl.ds(r0, tk), :] = ks_ref[0, pl.ds(r0, tk), :] + aux_ref[...]
            kwa_ref[pl.ds(r0, tk), :] = kw_ref[0, pl.ds(r0, tk), :] + lane_flag
            vT = lax.dot_general(eye[:LANES, :LANES], v_ref[0, pl.ds(r0, tk), :], NT_DIMS,
                                 preferred_element_type=F32).astype(BF16)
            vsT_ref[j, 0:dk, :] = vT[:dk]
            vsT_ref[j, dk:, :] = ones_rows
            for i in range(tq_per_tk):
                vwT_ref[j * tq_per_tk + i, 0:dk, :] = vT[dk:, i * tq:(i + 1) * tq]
                vwT_ref[j * tq_per_tk + i, dk:, :] = ones_rows[:, :tq]
            return carry

        lax.fori_loop(0, seq // tk, build, 0)

    q4 = lax.dot_general(eye[:NSA_REP * dk, :NSA_REP * dk], q_ref[0], NT_DIMS,
                         preferred_element_type=F32).astype(BF16)
    qT = jnp.concatenate([q4[r * dk:(r + 1) * dk, :] for r in range(NSA_REP)], axis=1)
    t_q = qs + lax.broadcasted_iota(jnp.int32, (1, tq), 1)
    t_row = jnp.concatenate([t_q] * NSA_REP, axis=1)
    zpad = jnp.zeros((LANES - dk - NSA_AUX, rows), BF16)

    def per_head(a):
        return jnp.concatenate([a] * NSA_REP, axis=1)

    gate_ref[...] = _dot_exact_lhs(eye[:LANES, :LANES], jax.nn.sigmoid(gl_ref[0]), NT_DIMS)
    gates = gate_ref[pl.ds(pl.multiple_of(grp * NSA_AUX, NSA_AUX), NSA_AUX), :]

    n_cmp = kc_ref.shape[2]
    s_c = jnp.dot(kc_ref[0, 0], qT, preferred_element_type=F32)
    c_end = lax.broadcasted_iota(jnp.int32, (n_cmp, 1), 0) * CMP_STRIDE + (CMP_BLK - 1)
    s_c = s_c + per_head(jnp.where(c_end <= t_q, 0.0, MASK_VALUE))
    mx_c = jnp.max(s_c, axis=0, keepdims=True)
    p_c = jnp.exp2(s_c - mx_c)
    l_c = jnp.sum(p_c, axis=0, keepdims=True)
    p_c = p_c * jnp.where(t_row >= CMP_BLK - 1, 1.0 / l_c, 0.0)
    o_c = jnp.dot(vcT_ref[0, 0], p_c.astype(BF16), preferred_element_type=F32)
    for r in range(NSA_REP):
        part_ref[r * dk:(r + 1) * dk, :] = gates[3 * r:3 * r + 1] * o_c[:, r * tq:(r + 1) * tq]

    p_sum = p_c[:, 0:tq]
    for r in range(1, NSA_REP):
        p_sum = p_sum + p_c[:, r * tq:(r + 1) * tq]
    ov = ov_ref[...]
    imp = _dot_exact_lhs(ov, p_sum)
    n_blk = ov.shape[0]
    j_blk = lax.broadcasted_iota(jnp.int32, (n_blk, 1), 0)
    cur = t_q // SEL_BLK
    forced = (j_blk == 0) | (j_blk == cur) | (j_blk == cur - 1)
    valid = j_blk * SEL_BLK <= t_q
    n_rank = N_SEL - 3
    cand = jnp.where(forced, -jnp.inf, jnp.where(valid, imp, -1.0))

    n_wt = WINDOW // tq + 1
    row0 = lax.broadcasted_iota(jnp.int32, (NSA_AUX, rows), 0) == 0
    a_k = lax.broadcasted_iota(jnp.int32, (tq, 1), 0)
    b_q = lax.broadcasted_iota(jnp.int32, (1, tq), 1)
    w_tiles = [jnp.maximum(qt - (n_wt - 1) + i, 0) for i in range(n_wt)]

    def win_scores(i):
        before_start = qt - (n_wt - 1) + i < 0
        flag = jnp.where(row0 & before_start, MASK_VALUE, 0.0).astype(BF16)
        rhs_w = jnp.concatenate([qT, flag, zpad], axis=0)
        k_tile = kwa_ref[pl.ds(pl.multiple_of(w_tiles[i] * tq, tq), tq), :]
        s = jnp.dot(k_tile, rhs_w, preferred_element_type=F32)
        if i == 0:
            s = s + per_head(jnp.where(a_k > b_q, 0.0, MASK_VALUE))
        if i == n_wt - 1:
            s = s + per_head(jnp.where(a_k <= b_q, 0.0, MASK_VALUE))
        sw_ref[i * tq:(i + 1) * tq, :] = s

    def win_max():
        mw_ref[...] = jnp.max(sw_ref[...], axis=0, keepdims=True)

    def win_pv(i):
        p = jnp.exp2(sw_ref[i * tq:(i + 1) * tq, :] - mw_ref[...]).astype(BF16)
        pv = jnp.dot(vwT_ref[w_tiles[i]], p, preferred_element_type=F32)
        if i == 0:
            accw_ref[...] = pv
        else:
            accw_ref[...] = accw_ref[...] + pv

    win_steps = ([functools.partial(win_scores, i) for i in range(n_wt)] + [win_max]
                 + [functools.partial(win_pv, i) for i in range(n_wt)])

    work = cand
    maxima = []
    for r in range(n_rank):
        mx = jnp.max(work, axis=0, keepdims=True)
        maxima.append(mx)
        work = jnp.where(work == mx, -jnp.inf, work)
        if r < len(win_steps):
            win_steps[r]()
    for step in win_steps[n_rank:]:
        step()

    ones_lhs = jnp.ones((SUBLANES, n_blk), BF16)

    def count(mask):
        return jnp.dot(ones_lhs, jnp.where(mask, 1.0, 0.0).astype(BF16),
                       preferred_element_type=F32)[0:1]

    thr = maxima[-1]
    for mx in reversed(maxima[:-1]):
        thr = jnp.where(count(cand >= mx) >= n_rank, mx, thr)
    ties = cand == thr
    below = (lax.broadcasted_iota(jnp.int32, (n_blk, n_blk), 1)
             < lax.broadcasted_iota(jnp.int32, (n_blk, n_blk), 0))
    tie_rank = jnp.dot(jnp.where(below, 1.0, 0.0).astype(BF16),
                       jnp.where(ties, 1.0, 0.0).astype(BF16), preferred_element_type=F32)
    picked = (cand > thr) | (ties & (tie_rank < n_rank - count(cand > thr)))
    sel_ref[0:n_blk, :] = jnp.where(forced | (picked & (cand >= 0.0)), 0.0, MASK_VALUE)
    sel_ref[n_blk:, :] = jnp.zeros((NSA_AUX, tq), F32)

    acc_w = accw_ref[...]
    o_w = acc_w[:dk] * (1.0 / acc_w[dk:dk + 1])
    for r in range(NSA_REP):
        part_ref[r * dk:(r + 1) * dk, :] = (part_ref[r * dk:(r + 1) * dk, :]
                                            + gates[3 * r + 2:3 * r + 3] * o_w[:, r * tq:(r + 1) * tq])

    blk_per_tile = tk // SEL_BLK
    n_full = qs // tk

    def scores(kt, slot, diagonal=False):
        sb = sel_ref[pl.ds(pl.multiple_of(kt * blk_per_tile, blk_per_tile), NSA_AUX), :]
        rhs = jnp.concatenate([qT, per_head(sb).astype(BF16), zpad], axis=0)
        k_tile = ksa_ref[pl.ds(pl.multiple_of(kt * tk, tk), tk), :]
        s = jnp.dot(k_tile, rhs, preferred_element_type=F32)
        if diagonal:
            kpos = kt * tk + lax.broadcasted_iota(jnp.int32, (tk, 1), 0)
            s = s + per_head(jnp.where(kpos <= t_q, 0.0, MASK_VALUE))
        s_ref[slot] = s

    def softmax_pv(kt, slot):
        m_i = m_ref[...]
        m_new = jnp.maximum(m_i, jnp.max(s_ref[slot], axis=0, keepdims=True))
        alpha = jnp.exp2(m_i - m_new)
        p = jnp.exp2(s_ref[slot] - m_new).astype(BF16)
        pv = jnp.dot(vsT_ref[kt], p, preferred_element_type=F32)
        acc_ref[...] = alpha * acc_ref[...] + pv
        m_ref[...] = m_new

    m_ref[...] = jnp.full((1, rows), MASK_VALUE, F32)
    acc_ref[...] = jnp.zeros((dk + NSA_AUX, rows), F32)
    scores(n_full, 0, diagonal=True)

    n_pairs = n_full // 2

    def slot0_tile(i):
        return jnp.where(i == 0, n_full, 2 * i - 1)

    def pair_step(i, carry):
        scores(2 * i, 1)
        softmax_pv(slot0_tile(i), 0)
        scores(2 * i + 1, 0)
        softmax_pv(2 * i, 1)
        return carry

    def quad_step(i, carry):
        pair_step(2 * i, carry)
        return pair_step(2 * i + 1, carry)

    n_quads = n_pairs // 2
    lax.fori_loop(0, n_quads, quad_step, 0)
    lax.fori_loop(2 * n_quads, n_pairs, pair_step, 0)

    @pl.when(n_full % 2 == 1)
    def _():
        scores(2 * n_pairs, 1)
        softmax_pv(slot0_tile(n_pairs), 0)
        softmax_pv(2 * n_pairs, 1)

    @pl.when(n_full % 2 == 0)
    def _():
        softmax_pv(slot0_tile(n_pairs), 0)

    acc_s = acc_ref[...]
    o_s = acc_s[:dk] * (1.0 / acc_s[dk:dk + 1])
    gates = gate_ref[pl.ds(pl.multiple_of(grp * NSA_AUX, NSA_AUX), NSA_AUX), :]
    out_t = jnp.concatenate(
        [part_ref[r * dk:(r + 1) * dk, :] + gates[3 * r + 1:3 * r + 2] * o_s[:, r * tq:(r + 1) * tq]
         for r in range(NSA_REP)], axis=0).astype(BF16)
    o_ref[0] = lax.dot_general(eye[:tq, :tq], out_t, NT_DIMS,
                               preferred_element_type=F32).astype(o_ref.dtype)


def _nsa_attn(p_att3, aux3, small_block, kc, vcT, overlap_t, eye, aux_s):
    bsz, seq, _ = p_att3.shape
    n_grp, dk = NSA_KV_HEADS, NSA_HEAD_DIM
    tq, tk = NSA_TQ, NSA_TK
    rows = NSA_REP * tq
    q_w = NSA_REP * dk
    n_cmp = kc.shape[2]
    n_blk = seq // SEL_BLK
    v_rows = dk + NSA_AUX
    n_wt = WINDOW // tq + 1
    first = NSA_HEADS * dk // LANES
    kv_spec = lambda off: pl.BlockSpec((1, seq, LANES), lambda b, g, t: (b, 0, first + off + g))
    step_rows = NSA_TILES_PER_STEP * tq
    return pl.pallas_call(
        _nsa_attn_kernel,
        grid=(bsz, n_grp, seq // step_rows),
        in_specs=[
            pl.BlockSpec((1, step_rows, q_w), lambda b, g, t: (b, t, g)),
            kv_spec(0), kv_spec(n_grp), kv_spec(2 * n_grp),
            pl.BlockSpec((1, 1, n_cmp, dk), lambda b, g, t: (b, g, 0, 0)),
            pl.BlockSpec((1, 1, dk, n_cmp), lambda b, g, t: (b, g, 0, 0)),
            pl.BlockSpec((n_blk, n_cmp), lambda b, g, t: (0, 0)),
            pl.BlockSpec((1, step_rows, LANES), lambda b, g, t: (b, t, small_block)),
            pl.BlockSpec(eye.shape, lambda b, g, t: (0, 0)),
            pl.BlockSpec((tk, LANES), lambda b, g, t: (0, 0)),
        ],
        out_specs=pl.BlockSpec((1, step_rows, q_w), lambda b, g, t: (b, t, g)),
        out_shape=jax.ShapeDtypeStruct((bsz, seq, NSA_HEADS * dk), BF16),
        scratch_shapes=[pltpu.VMEM((seq, LANES), BF16),
                        pltpu.VMEM((seq, LANES), BF16),
                        pltpu.VMEM((seq // tk, v_rows, tk), BF16),
                        pltpu.VMEM((seq // tq, v_rows, tq), BF16),
                        pltpu.VMEM((n_blk + NSA_AUX, tq), F32),
                        pltpu.VMEM((2, tk, rows), F32),
                        pltpu.VMEM((1, rows), F32),
                        pltpu.VMEM((v_rows, rows), F32),
                        pltpu.VMEM((n_wt * tq, rows), F32),
                        pltpu.VMEM((1, rows), F32),
                        pltpu.VMEM((v_rows, rows), F32),
                        pltpu.VMEM((q_w, tq), F32),
                        pltpu.VMEM((LANES, tq), F32)],
        compiler_params=_params(("parallel", "parallel", "arbitrary")), name="nsa_attention",
    )(p_att3, p_att3, p_att3, p_att3, kc, vcT, overlap_t, aux3, eye, aux_s)


def _bmm(a, b, dims):
    a = a.astype(BF16)
    b = b.astype(BF16)
    return jnp.stack([lax.dot_general(a[n], b[n], dims, preferred_element_type=F32)
                      for n in range(a.shape[0])])


def _gdn_kernel(y_ref, z_ref, small_ref, alog_ref, dtb_ref, nw_ref, o_ref, state_ref):
    bsz = y_ref.shape[0]
    cs = GDN_CHUNK
    hd = GDN_HEAD_DIM
    nh = GDN_HEADS
    c = pl.program_id(0)

    @pl.when(c == 0)
    def _():
        state_ref[...] = jnp.zeros_like(state_ref)

    y = y_ref[...]

    small = small_ref[...]
    sp_in = small + dtb_ref[...]
    softplus = jnp.maximum(sp_in, 0.0) + jnp.log1p(jnp.exp(-jnp.abs(sp_in)))
    g_all = -jnp.exp(alog_ref[...]) * softplus
    beta_all = jax.nn.sigmoid(small)
    ii = lax.broadcasted_iota(jnp.int32, (cs, cs), 0)
    jj = lax.broadcasted_iota(jnp.int32, (cs, cs), 1)
    incl = ii >= jj
    strict = ii > jj
    tril = jnp.where(incl, 1.0, 0.0).astype(BF16)
    e128 = jnp.where(lax.broadcasted_iota(jnp.int32, (LANES, LANES), 0)
                     == lax.broadcasted_iota(jnp.int32, (LANES, LANES), 1), 1.0, 0.0).astype(BF16)
    gc_all = [_dot_exact_lhs(tril, g_all[b]) for b in range(bsz)]
    gcT_all = [_dot_exact_lhs(e128, gc_all[b], NT_DIMS) for b in range(bsz)]

    chains = [(b, h) for b in range(bsz) for h in range(nh)]

    def heads(a, base):
        return jnp.stack([a[b][:, base + h * hd:base + (h + 1) * hd] for b, h in chains])

    def lane_bcast(cols, base):
        return jnp.stack([jnp.broadcast_to(cols[b][:, base + h:base + h + 1], (cs, hd))
                          for b, h in chains])

    n_qk = nh * hd
    q = heads(y, 0)
    k = heads(y, n_qk)
    v = heads(y, 2 * n_qk)
    q = q * lax.rsqrt(jnp.sum(q * q, axis=-1, keepdims=True) + NORM_EPS) * (hd ** -0.5)
    k = k * lax.rsqrt(jnp.sum(k * k, axis=-1, keepdims=True) + NORM_EPS)
    gc = lane_bcast(gc_all, SMALL_A)
    beta = lane_bcast([beta_all[b] for b in range(bsz)], SMALL_B)
    gc_row = jnp.stack([gcT_all[b][SMALL_A + h:SMALL_A + h + 1, :] for b, h in chains])

    decay = jnp.exp(jnp.where(incl, gc[:, :, :cs] - gc_row, MASK_VALUE))
    kk = _bmm(k, k, NT_DIMS)
    m_full = jnp.where(strict, -(beta[:, :, :cs] * kk * decay), 0.0)
    blk_diff = ii ^ jj
    inv_m1 = jnp.where(blk_diff < 2, m_full, 0.0)
    size = 2
    while size < cs:
        m_off = jnp.where((blk_diff >= size) & (blk_diff < 2 * size), m_full, 0.0)
        t_mat = m_off + _bmm(inv_m1, m_off, NN_DIMS)
        inv_m1 = inv_m1 + t_mat + _bmm(t_mat, inv_m1, NN_DIMS)
        size *= 2
    e_gc = jnp.exp(gc)
    rhs = jnp.concatenate([v * beta, k * (beta * e_gc)], axis=2)
    sol = rhs + _bmm(inv_m1, rhs, NN_DIMS)
    u, w = sol[:, :, :hd], sol[:, :, hd:]
    qk = _bmm(q, k, NT_DIMS) * decay
    q_dec = q * e_gc
    gc_last = gc[:, cs - 1:cs, :]
    k_dec = k * jnp.exp(gc_last - gc)
    state = state_ref[...]
    v_new = u - _bmm(w, state, NN_DIMS)
    o = _bmm(q_dec, state, NN_DIMS) + _bmm(qk, v_new, NN_DIMS)
    state_ref[...] = state * jnp.exp(gc_last) + _bmm(k_dec, v_new, TN_DIMS)

    o = o * lax.rsqrt(jnp.mean(o * o, axis=-1, keepdims=True) + NORM_EPS) * nw_ref[...]
    z = heads(z_ref[...].astype(F32), 0)
    o = o * (z * jax.nn.sigmoid(z))
    for b in range(bsz):
        o_ref[b] = jnp.concatenate([o[b * nh + h] for h in range(nh)], axis=1).astype(o_ref.dtype)


def _gated_deltanet(qkv3, zm3, aux3, small_block, alog_pad, dtb_pad, norm_w):
    bsz, seq, qkv_cols = qkv3.shape
    cs = GDN_CHUNK
    z_cols = GDN_HEADS * GDN_HEAD_DIM
    out = pl.pallas_call(
        _gdn_kernel,
        grid=(seq // cs,),
        in_specs=[
            pl.BlockSpec((bsz, cs, qkv_cols), lambda c: (0, c, 0)),
            pl.BlockSpec((bsz, cs, z_cols), lambda c: (0, c, 0)),
            pl.BlockSpec((bsz, cs, LANES), lambda c: (0, c, small_block)),
            pl.BlockSpec((1, LANES), lambda c: (0, 0)),
            pl.BlockSpec((1, LANES), lambda c: (0, 0)),
            pl.BlockSpec((1, GDN_HEAD_DIM), lambda c: (0, 0)),
        ],
        out_specs=pl.BlockSpec((bsz, cs, z_cols), lambda c: (0, c, 0)),
        out_shape=jax.ShapeDtypeStruct((bsz, seq, z_cols), BF16),
        scratch_shapes=[pltpu.VMEM((bsz * GDN_HEADS, GDN_HEAD_DIM, GDN_HEAD_DIM), F32)],
        compiler_params=_params(("arbitrary",)), name="gated_deltanet",
    )(qkv3, zm3, aux3, alog_pad, dtb_pad, norm_w.reshape(1, GDN_HEAD_DIM))
    return out.reshape(bsz * seq, z_cols)


def _merge_kernel(ya_ref, yb_ref, ma_ref, mb_ref, x_ref, mod_ref, wa_ref, wb_ref, wo_ref,
                  nf_ref, x1_ref, h2_ref):
    pa = jnp.dot(ya_ref[...], wa_ref[...], preferred_element_type=F32)
    pb = jnp.dot(yb_ref[...], wb_ref[...], preferred_element_type=F32)
    mixed = (jax.nn.sigmoid(ma_ref[...].astype(F32)) * pa
             + jax.nn.sigmoid(mb_ref[...].astype(F32)) * pb)
    mod = mod_ref[0]
    x1 = x_ref[...] + mod[GT_M:GT_M + 1, :] * jnp.dot(
        mixed.astype(BF16), wo_ref[...], preferred_element_type=F32)
    x1_ref[...] = x1
    h2_ref[...] = _modulated_norm(x1, nf_ref[...], mod, SC_F, SH_F).astype(BF16)


def _merge_project(ya, yb, pc, merge_blk, x2, mod, w_pa, w_pb, w_o, norm_ffn, seq):
    t, d = x2.shape
    tiles_per_batch = seq // ROW_TILE
    row = lambda i: (i, 0)
    const = lambda i: (0, 0)
    return pl.pallas_call(
        _merge_kernel,
        grid=(t // ROW_TILE,),
        in_specs=[pl.BlockSpec((ROW_TILE, d), row),
                  pl.BlockSpec((ROW_TILE, d), row),
                  pl.BlockSpec((ROW_TILE, d), lambda i: (i, merge_blk)),
                  pl.BlockSpec((ROW_TILE, d), lambda i: (i, merge_blk + 1)),
                  pl.BlockSpec((ROW_TILE, d), row),
                  pl.BlockSpec((1, 6, d), lambda i: (i // tiles_per_batch, 0, 0)),
                  pl.BlockSpec((d, d), const),
                  pl.BlockSpec((d, d), const),
                  pl.BlockSpec((d, d), const),
                  pl.BlockSpec((1, d), const)],
        out_specs=[pl.BlockSpec((ROW_TILE, d), row), pl.BlockSpec((ROW_TILE, d), row)],
        out_shape=[jax.ShapeDtypeStruct((t, d), F32), jax.ShapeDtypeStruct((t, d), BF16)],
        compiler_params=_params(("parallel",)), name="merge_project",
    )(ya, yb, pc, pc, x2, mod, w_pa, w_pb, w_o, norm_ffn.reshape(1, d))


def _ffn_kernel(h_ref, halo_ref, res_ref, mod_ref, wg_ref, wv_ref, cwg_ref, cwv_ref,
                cbg_ref, cbv_ref, wd_ref, nfin_ref, o_ref, ug_ref, uv_ref,
                *, tiles_per_batch, last):
    i = pl.program_id(0)
    tm = h_ref.shape[0]
    first_of_batch = (i % tiles_per_batch) == 0
    h = h_ref[...]
    halo = halo_ref[...]

    def conv_half(w_ref, cw_ref, cb_ref, u_ref):
        u_ref[SUBLANES:, :] = jnp.dot(h, w_ref[...], preferred_element_type=F32)
        up_halo = jnp.dot(halo, w_ref[...], preferred_element_type=F32)[SUBLANES:]
        u_ref[0:SUBLANES, :] = jnp.where(first_of_batch, 0.0, up_halo)
        cw = cw_ref[...]
        out = cb_ref[...] + cw[FFN_CONV - 1:FFN_CONV, :] * u_ref[SUBLANES:, :]
        for j in range(FFN_CONV - 1):
            off = SUBLANES - (FFN_CONV - 1) + j
            out = out + cw[j:j + 1, :] * u_ref[off:off + tm, :]
        return out

    gate = conv_half(wg_ref, cwg_ref, cbg_ref, ug_ref)
    val = conv_half(wv_ref, cwv_ref, cbv_ref, uv_ref)
    act = gate * jax.nn.sigmoid(gate) * val
    down = jnp.dot(act.astype(BF16), wd_ref[...], preferred_element_type=F32)
    y = res_ref[...] + mod_ref[0][GT_F:GT_F + 1, :] * down
    if last:
        ms = jnp.mean(y * y, axis=-1, keepdims=True)
        y = y * lax.rsqrt(ms + NORM_EPS) * nfin_ref[...]
    o_ref[...] = y


def _conv_ffn(h2, x1, mod, w_up, conv_w, conv_b, w_down, norm_final, seq):
    t, d = x1.shape
    tm, tf = ROW_TILE, FFN_TF
    n_f = FFN_DIM // tf
    tiles_per_batch = seq // tm
    halo_per_tile = tm // SUBLANES
    row = lambda i: (i, 0)
    res = x1
    for f in range(n_f):
        col = lambda i, c=f: (0, c)
        col_val = lambda i, c=n_f + f: (0, c)
        res = pl.pallas_call(
            functools.partial(_ffn_kernel, tiles_per_batch=tiles_per_batch, last=f == n_f - 1),
            grid=(t // tm,),
            in_specs=[pl.BlockSpec((tm, d), row),
                      pl.BlockSpec((SUBLANES * 2, d),
                                   lambda i: (jnp.maximum(i * (halo_per_tile // 2) - 1, 0), 0)),
                      pl.BlockSpec((tm, d), row),
                      pl.BlockSpec((1, 6, d), lambda i: (i // tiles_per_batch, 0, 0)),
                      pl.BlockSpec((d, tf), col),
                      pl.BlockSpec((d, tf), col_val),
                      pl.BlockSpec((FFN_CONV, tf), col),
                      pl.BlockSpec((FFN_CONV, tf), col_val),
                      pl.BlockSpec((1, tf), col),
                      pl.BlockSpec((1, tf), col_val),
                      pl.BlockSpec((tf, d), lambda i, c=f: (c, 0)),
                      pl.BlockSpec((1, d), lambda i: (0, 0))],
            out_specs=pl.BlockSpec((tm, d), row),
            out_shape=jax.ShapeDtypeStruct((t, d), F32),
            scratch_shapes=[pltpu.VMEM((tm + SUBLANES, tf), F32),
                            pltpu.VMEM((tm + SUBLANES, tf), F32)],
            compiler_params=_params(("parallel",)), name="conv_ffn",
        )(h2, h2, res, mod, w_up, w_up, conv_w, conv_w, conv_b, conv_b, w_down,
          norm_final.reshape(1, d))
    return res


def _overlap_t(seq):
    n_cmp = (seq - CMP_BLK) // CMP_STRIDE + 1
    n_blk = seq // SEL_BLK
    c_start = np.arange(n_cmp) * CMP_STRIDE
    s_start = np.arange(n_blk) * SEL_BLK
    ov = np.clip(np.minimum(c_start[:, None] + CMP_BLK, s_start[None, :] + SEL_BLK)
                 - np.maximum(c_start[:, None], s_start[None, :]), 0, None) / CMP_BLK
    n_half = seq // CMP_STRIDE
    out = np.zeros((n_blk, n_half), np.float32)
    out[:, :n_cmp] = ov.T
    return jnp.asarray(out, BF16)


def kernel(x, c, w_ada, b_ada, norm_mix, w_in, nsa_pos_k, nsa_pos_v, nsa_ck_w1, nsa_ck_b1, nsa_ck_w2, nsa_ck_b2, nsa_cv_w1, nsa_cv_b1, nsa_cv_w2, nsa_cv_b2, gdn_conv, gdn_a_log, gdn_dt_bias, gdn_norm, w_proj_nsa, w_proj_gdn, w_out, norm_ffn, ffn_up, ffn_conv, ffn_conv_b, ffn_down, norm_final):
    bsz, seq, d = x.shape
    depth = w_ada.shape[0]
    n_grp, dk = NSA_KV_HEADS, NSA_HEAD_DIM
    q_cols = NSA_HEADS * dk
    gqkv_cols = 3 * GDN_HEADS * GDN_HEAD_DIM
    z_cols = GDN_HEADS * GDN_HEAD_DIM
    o_kv = q_cols
    o_gl = o_kv + 6 * n_grp * dk
    o_gq = o_gl + 3 * NSA_HEADS
    o_a = o_gq + gqkv_cols
    o_z = o_a + 2 * GDN_HEADS
    overlap_t = _overlap_t(seq)
    x2 = x.reshape(bsz * seq, d)
    assert depth == 1, "kernel supports the problem's DEPTH == 1"
    for l in range(depth):
        mod = _adaln_mod(c, w_ada[l], b_ada[l])
        w = w_in[l]
        kv_w = w[:, o_kv:o_gl].reshape(d, 6, n_grp, dk)
        zero = jnp.zeros((d, n_grp, dk), F32)
        pair = lambda a, b: jnp.concatenate([a, b], axis=2).reshape(d, n_grp * 2 * dk)
        w_att = jnp.concatenate([w[:, :o_kv] * (dk ** -0.5 * LOG2_E), pair(kv_w[:, 2], zero),
                                 pair(kv_w[:, 4], zero), pair(kv_w[:, 3], kv_w[:, 5])],
                                axis=1).astype(BF16)
        gate_w = jnp.pad(w[:, o_gl:o_gq].reshape(d, n_grp, 3 * NSA_REP),
                         ((0, 0), (0, 0), (0, NSA_AUX - 3 * NSA_REP))).reshape(d, SMALL_A)
        w_small = jnp.concatenate([gate_w, w[:, o_a:o_z],
                                   jnp.zeros((d, LANES - SMALL_B - GDN_HEADS), F32)],
                                  axis=1).astype(BF16)
        w_aux = jnp.concatenate([pair(kv_w[:, 0], kv_w[:, 1]).astype(BF16), w_small], axis=1)
        small_block = n_grp
        w_zm = w[:, o_z:].astype(BF16)

        p_att, h = _norm_mod_matmul(x2, norm_mix[l], mod, w_att, 1280, BF16, seq)
        qkv = _matmul_conv_silu(h, w[:, o_gq:o_a].astype(BF16), gdn_conv[l], 1536, seq)
        zm = _matmul(h, w_zm, 1536, BF16)
        aux = _matmul(h, w_aux, w_aux.shape[1], F32)
        aux3 = aux.reshape(bsz, seq, aux.shape[1])

        w1k = nsa_ck_w1[l].reshape(2, CMP_STRIDE, dk, CMP_HIDDEN)
        w1v = nsa_cv_w1[l].reshape(2, CMP_STRIDE, dk, CMP_HIDDEN)
        zw = jnp.zeros((CMP_STRIDE, dk, 2 * CMP_HIDDEN), F32)
        wc = jnp.concatenate(
            [jnp.concatenate([w1k[0], w1k[1], zw], axis=2),
             jnp.concatenate([zw, w1v[0], w1v[1]], axis=2)], axis=1).astype(BF16)
        pos = jnp.stack([nsa_pos_k[l], nsa_pos_v[l]]).reshape(2, 1, CMP_BLK * dk)
        pos = jnp.broadcast_to(pos, (2, 2 * SUBLANES, CMP_BLK * dk)).astype(BF16)
        cw1 = jnp.stack([nsa_ck_w1[l], nsa_cv_w1[l]]).astype(BF16)
        cb1 = jnp.stack([nsa_ck_b1[l], nsa_cv_b1[l]]).reshape(2, 1, CMP_HIDDEN)
        cw2 = jnp.stack([nsa_ck_w2[l], nsa_cv_w2[l]]).astype(BF16)
        cb2 = jnp.stack([nsa_ck_b2[l], nsa_cv_b2[l]]).reshape(2, 1, dk)
        eye = jnp.eye(max(NSA_REP * dk, NSA_TQ), dtype=BF16)
        kc, vcT = _compress(aux3, wc, pos, cw1, cb1, cw2, cb2, eye, 0)

        blk_in_tile = np.arange(NSA_TK) // SEL_BLK
        aux_s = np.zeros((NSA_TK, LANES), np.float32)
        aux_s[np.arange(NSA_TK), dk + blk_in_tile] = 1.0
        y_a = _nsa_attn(p_att.reshape(bsz, seq, p_att.shape[1]), aux3, small_block, kc, vcT,
                        overlap_t, eye, jnp.asarray(aux_s, BF16)).reshape(bsz * seq, q_cols)

        alog_pad = jnp.zeros((1, LANES), F32).at[0, SMALL_A:SMALL_A + GDN_HEADS].set(gdn_a_log[l])
        dtb_pad = jnp.zeros((1, LANES), F32).at[0, SMALL_A:SMALL_A + GDN_HEADS].set(gdn_dt_bias[l])
        y_b = _gated_deltanet(qkv.reshape(bsz, seq, gqkv_cols), zm.reshape(bsz, seq, zm.shape[1]),
                              aux3, small_block, alog_pad, dtb_pad, gdn_norm[l])

        x1, h2 = _merge_project(y_a, y_b, zm, z_cols // d, x2, mod, w_proj_nsa[l].astype(BF16),
                                w_proj_gdn[l].astype(BF16), w_out[l].astype(BF16),
                                norm_ffn[l], seq)
        out = _conv_ffn(h2, x1, mod, ffn_up[l].astype(BF16), ffn_conv[l],
                        ffn_conv_b[l].reshape(1, 2 * FFN_DIM), ffn_down[l].astype(BF16),
                        norm_final, seq)
    return out.reshape(bsz, seq, d)
```

```python
import functools

import numpy as np
import jax
import jax.numpy as jnp
from jax import lax
from jax.experimental import pallas as pl
from jax.experimental.pallas import tpu as pltpu

F32 = jnp.float32
BF16 = jnp.bfloat16
HIGHEST = lax.Precision.HIGHEST

NSA_HEAD_DIM = 64
NSA_HEADS = 16
NSA_KV_HEADS = 4
NSA_REP = NSA_HEADS // NSA_KV_HEADS
CMP_BLK = 32
CMP_STRIDE = 16
CMP_HIDDEN = 256
SEL_BLK = 64
N_SEL = 16
WINDOW = 512
GDN_HEAD_DIM = 128
GDN_HEADS = 8
GDN_CONV = 4
GDN_CHUNK = 64
FFN_DIM = 2816
FFN_CONV = 3
NORM_EPS = 1e-6
MASK_VALUE = -1e30

LANES = 128
SUBLANES = 8
VMEM_LIMIT = 56 * 1024 * 1024

NSA_TQ = 256
NSA_TILES_PER_STEP = 4
NSA_TK = 512
NSA_AUX = 16
LOG2_E = 1.4426950408889634
PROJ_TILE = 1024
ROW_TILE = 512
FFN_TF = 1408

SH_M, SC_M, GT_M, SH_F, SC_F, GT_F = range(6)
SMALL_A = NSA_KV_HEADS * NSA_AUX
SMALL_B = SMALL_A + GDN_HEADS


def _params(sem):
    return pltpu.CompilerParams(dimension_semantics=sem, vmem_limit_bytes=VMEM_LIMIT)


def _split3(a):
    hi = a.astype(BF16)
    r1 = a - hi.astype(F32)
    mid = r1.astype(BF16)
    lo = (r1 - mid.astype(F32)).astype(BF16)
    return hi, mid, lo


def _dot_exact_lhs(a_bf16, b, dims=None):
    if dims is None:
        dims = (((a_bf16.ndim - 1,), (0,)), ((), ()))
    out = None
    for part in _split3(b):
        term = lax.dot_general(a_bf16, part, dimension_numbers=dims, preferred_element_type=F32)
        out = term if out is None else out + term
    return out


NT_DIMS = (((1,), (1,)), ((), ()))
TN_DIMS = (((0,), (0,)), ((), ()))
NN_DIMS = (((1,), (0,)), ((), ()))


def _mod_kernel(c_ref, w_ref, b_ref, o_ref):
    o_ref[...] = jnp.dot(c_ref[...], w_ref[...], preferred_element_type=F32,
                         precision=HIGHEST) + b_ref[...]


def _adaln_mod(c, w_ada, b_ada):
    bsz, d = c.shape
    n = w_ada.shape[1]
    c8 = jnp.zeros((SUBLANES, d), F32).at[:bsz].set(c)
    out = pl.pallas_call(
        _mod_kernel,
        grid=(n // d,),
        in_specs=[pl.BlockSpec((SUBLANES, d), lambda j: (0, 0)),
                  pl.BlockSpec((d, d), lambda j: (0, j)),
                  pl.BlockSpec((1, d), lambda j: (0, j))],
        out_specs=pl.BlockSpec((SUBLANES, d), lambda j: (0, j)),
        out_shape=jax.ShapeDtypeStruct((SUBLANES, n), F32),
        compiler_params=_params(("arbitrary",)), name="adaln_mod",
    )(c8, w_ada, b_ada.reshape(1, n))
    return out[:bsz].reshape(bsz, n // d, d)


def _modulated_norm(x, norm_w, mod, sc_row, sh_row):
    ms = jnp.mean(x * x, axis=-1, keepdims=True)
    y = x * lax.rsqrt(ms + NORM_EPS) * norm_w
    return y * (1.0 + mod[sc_row:sc_row + 1, :]) + mod[sh_row:sh_row + 1, :]


def _nmm_kernel(x_ref, nw_ref, mod_ref, w_ref, o_ref, h_ref):
    @pl.when(pl.program_id(1) == 0)
    def _():
        h = _modulated_norm(x_ref[...], nw_ref[...], mod_ref[0], SC_M, SH_M)
        h_ref[...] = h.astype(BF16)

    o_ref[...] = jnp.dot(h_ref[...], w_ref[...],
                         preferred_element_type=F32).astype(o_ref.dtype)


def _norm_mod_matmul(x2, norm_w, mod, w, tn, out_dtype, seq):
    t, d = x2.shape
    n = w.shape[1]
    tiles_per_batch = seq // PROJ_TILE
    return pl.pallas_call(
        _nmm_kernel,
        grid=(t // PROJ_TILE, n // tn),
        in_specs=[pl.BlockSpec((PROJ_TILE, d), lambda i, j: (i, 0)),
                  pl.BlockSpec((1, d), lambda i, j: (0, 0)),
                  pl.BlockSpec((1, 6, d), lambda i, j: (i // tiles_per_batch, 0, 0)),
                  pl.BlockSpec((d, tn), lambda i, j: (0, j))],
        out_specs=[pl.BlockSpec((PROJ_TILE, tn), lambda i, j: (i, j)),
                   pl.BlockSpec((PROJ_TILE, d), lambda i, j: (i, 0))],
        out_shape=[jax.ShapeDtypeStruct((t, n), out_dtype), jax.ShapeDtypeStruct((t, d), BF16)],
        compiler_params=_params(("parallel", "arbitrary")), name="norm_mod_proj",
    )(x2, norm_w.reshape(1, d), mod, w)


def _mm_kernel(h_ref, w_ref, o_ref):
    o_ref[...] = jnp.dot(h_ref[...], w_ref[...], preferred_element_type=F32).astype(o_ref.dtype)


def _matmul(h, w, tn, out_dtype):
    t, d = h.shape
    n = w.shape[1]
    return pl.pallas_call(
        _mm_kernel,
        grid=(n // tn, t // PROJ_TILE),
        in_specs=[pl.BlockSpec((PROJ_TILE, d), lambda j, i: (i, 0)),
                  pl.BlockSpec((d, tn), lambda j, i: (0, j))],
        out_specs=pl.BlockSpec((PROJ_TILE, tn), lambda j, i: (i, j)),
        out_shape=jax.ShapeDtypeStruct((t, n), out_dtype),
        compiler_params=_params(("parallel", "parallel")), name="proj",
    )(h, w)


def _mm_conv_kernel(h_ref, w_ref, cw_ref, o_ref, u_ref, *, tiles_per_batch):
    tm = h_ref.shape[0]
    width = cw_ref.shape[0]

    @pl.when(pl.program_id(1) % tiles_per_batch == 0)
    def _():
        u_ref[0:SUBLANES, :] = jnp.zeros((SUBLANES, u_ref.shape[1]), F32)

    cw = cw_ref[...]
    n_sub = 4
    sub = tm // n_sub

    def matmul(s):
        u_ref[SUBLANES + s * sub:SUBLANES + (s + 1) * sub, :] = jnp.dot(
            h_ref[s * sub:(s + 1) * sub, :], w_ref[...], preferred_element_type=F32)

    def conv(s):
        base = SUBLANES + s * sub
        y = cw[width - 1:width, :] * u_ref[base:base + sub, :]
        for i in range(width - 1):
            off = base - (width - 1) + i
            y = y + cw[i:i + 1, :] * u_ref[off:off + sub, :]
        o_ref[s * sub:(s + 1) * sub, :] = y * jax.nn.sigmoid(y)

    matmul(0)
    for s in range(1, n_sub):
        matmul(s)
        conv(s - 1)
    conv(n_sub - 1)
    u_ref[0:SUBLANES, :] = u_ref[tm:tm + SUBLANES, :]


def _matmul_conv_silu(h, w, conv_w, tn, seq):
    t, d = h.shape
    n = w.shape[1]
    return pl.pallas_call(
        functools.partial(_mm_conv_kernel, tiles_per_batch=seq // PROJ_TILE),
        grid=(n // tn, t // PROJ_TILE),
        in_specs=[pl.BlockSpec((PROJ_TILE, d), lambda j, i: (i, 0)),
                  pl.BlockSpec((d, tn), lambda j, i: (0, j)),
                  pl.BlockSpec((conv_w.shape[0], tn), lambda j, i: (0, j))],
        out_specs=pl.BlockSpec((PROJ_TILE, tn), lambda j, i: (i, j)),
        out_shape=jax.ShapeDtypeStruct((t, n), F32),
        scratch_shapes=[pltpu.VMEM((PROJ_TILE + SUBLANES, tn), F32)],
        compiler_params=_params(("parallel", "arbitrary")), name="proj_conv_silu",
    )(h, w, conv_w)


def _compress_kernel(t_ref, wc_ref, pos_ref, w1_ref, b1_ref, w2_ref, b2_ref, eye_ref,
                     kc_ref, vcT_ref):
    n_half = t_ref.shape[1] // CMP_STRIDE
    acc = None
    for l in range(CMP_STRIDE):
        tok = t_ref[0, pl.ds(l, n_half, stride=CMP_STRIDE), :].astype(BF16)
        term = jnp.dot(tok, wc_ref[l], preferred_element_type=F32)
        acc = term if acc is None else acc + term
    outs = []
    for kind in range(2):
        top = acc[:, 2 * kind * CMP_HIDDEN:(2 * kind + 1) * CMP_HIDDEN]
        bot = acc[:, (2 * kind + 1) * CMP_HIDDEN:(2 * kind + 2) * CMP_HIDDEN]
        bot_next = pltpu.roll(bot, n_half - 1, 0)
        pos_term = jnp.dot(pos_ref[kind], w1_ref[kind], preferred_element_type=F32)[0:1]
        hid = jax.nn.gelu(top + bot_next + pos_term + b1_ref[kind])
        outs.append(jnp.dot(hid.astype(BF16), w2_ref[kind], preferred_element_type=F32)
                    + b2_ref[kind])
    kc_ref[0, 0] = outs[0].astype(BF16)
    dk = NSA_HEAD_DIM
    vcT_ref[0, 0] = lax.dot_general(eye_ref[:dk, :dk], outs[1].astype(BF16), NT_DIMS,
                                    preferred_element_type=F32).astype(BF16)


def _compress(pc3, wc, pos, w1, b1, w2, b2, eye, first_col_block):
    bsz, seq, _ = pc3.shape
    n_grp, dk = NSA_KV_HEADS, NSA_HEAD_DIM
    n_half = seq // CMP_STRIDE
    const3 = lambda b, g: (0, 0, 0)
    return pl.pallas_call(
        _compress_kernel,
        grid=(bsz, n_grp),
        in_specs=[pl.BlockSpec((1, seq, LANES), lambda b, g: (b, 0, first_col_block + g)),
                  pl.BlockSpec(wc.shape, const3),
                  pl.BlockSpec(pos.shape, const3),
                  pl.BlockSpec(w1.shape, const3),
                  pl.BlockSpec(b1.shape, const3),
                  pl.BlockSpec(w2.shape, const3),
                  pl.BlockSpec(b2.shape, const3),
                  pl.BlockSpec(eye.shape, lambda b, g: (0, 0))],
        out_specs=[pl.BlockSpec((1, 1, n_half, dk), lambda b, g: (b, g, 0, 0)),
                   pl.BlockSpec((1, 1, dk, n_half), lambda b, g: (b, g, 0, 0))],
        out_shape=[jax.ShapeDtypeStruct((bsz, n_grp, n_half, dk), BF16),
                   jax.ShapeDtypeStruct((bsz, n_grp, dk, n_half), BF16)],
        compiler_params=_params(("parallel", "parallel")), name="nsa_compress",
    )(pc3, wc, pos, w1, b1, w2, b2, eye)


def _nsa_attn_kernel(q_ref, ks_ref, kw_ref, v_ref, kc_ref, vcT_ref, ov_ref, gl_ref, eye_ref,
                     aux_ref, o_ref, *scratch):
    tq = NSA_TQ
    first_tile = pl.program_id(2) * NSA_TILES_PER_STEP

    def one_tile(i, carry):
        rows_i = pl.ds(pl.multiple_of(i * tq, tq), tq)
        _nsa_query_tile(first_tile + i, q_ref.at[:, rows_i, :], ks_ref, kw_ref, v_ref, kc_ref,
                        vcT_ref, ov_ref, gl_ref.at[:, rows_i, :], eye_ref, aux_ref,
                        o_ref.at[:, rows_i, :], *scratch)
        return carry

    lax.fori_loop(0, NSA_TILES_PER_STEP, one_tile, 0)


def _nsa_query_tile(qt, q_ref, ks_ref, kw_ref, v_ref, kc_ref, vcT_ref, ov_ref, gl_ref, eye_ref,
                    aux_ref, o_ref,
                    ksa_ref, kwa_ref, vsT_ref, vwT_ref, sel_ref, s_ref, m_ref, acc_ref,
                    sw_ref, mw_ref, accw_ref, part_ref, gate_ref):
    tq, tk = NSA_TQ, NSA_TK
    rows = NSA_REP * tq
    dk = NSA_HEAD_DIM
    grp = pl.program_id(1)
    qs = qt * tq
    seq = ks_ref.shape[1]
    eye = eye_ref[...]
    tq_per_tk = tk // tq

    @pl.when(qt == 0)
    def _():
        lane = lax.broadcasted_iota(jnp.int32, (tk, LANES), 1)
        lane_flag = jnp.where(lane == dk, 1.0, 0.0).astype(BF16)
        ones_rows = jnp.ones((NSA_AUX, tk), BF16)

        def build(j, carry):
            r0 = pl.multiple_of(j * tk, tk)
            ksa_ref[pl.ds(r0, tk), :] = ks_ref[0, pl.ds(r0, tk), :] + aux_ref[...]
            kwa_ref[pl.ds(r0, tk), :] = kw_ref[0, pl.ds(r0, tk), :] + lane_flag
            vT = lax.dot_general(eye[:LANES, :LANES], v_ref[0, pl.ds(r0, tk), :], NT_DIMS,
                                 preferred_element_type=F32).astype(BF16)
            vsT_ref[j, 0:dk, :] = vT[:dk]
            vsT_ref[j, dk:, :] = ones_rows
            for i in range(tq_per_tk):
                vwT_ref[j * tq_per_tk + i, 0:dk, :] = vT[dk:, i * tq:(i + 1) * tq]
                vwT_ref[j * tq_per_tk + i, dk:, :] = ones_rows[:, :tq]
            return carry

        lax.fori_loop(0, seq // tk, build, 0)

    q4 = lax.dot_general(eye[:NSA_REP * dk, :NSA_REP * dk], q_ref[0], NT_DIMS,
                         preferred_element_type=F32).astype(BF16)
    qT = jnp.concatenate([q4[r * dk:(r + 1) * dk, :] for r in range(NSA_REP)], axis=1)
    t_q = qs + lax.broadcasted_iota(jnp.int32, (1, tq), 1)
    t_row = jnp.concatenate([t_q] * NSA_REP, axis=1)
    zpad = jnp.zeros((LANES - dk - NSA_AUX, rows), BF16)

    def per_head(a):
        return jnp.concatenate([a] * NSA_REP, axis=1)

    gate_ref[...] = _dot_exact_lhs(eye[:LANES, :LANES], jax.nn.sigmoid(gl_ref[0]), NT_DIMS)
    gates = gate_ref[pl.ds(pl.multiple_of(grp * NSA_AUX, NSA_AUX), NSA_AUX), :]

    n_cmp = kc_ref.shape[2]
    s_c = jnp.dot(kc_ref[0, 0], qT, preferred_element_type=F32)
    c_end = lax.broadcasted_iota(jnp.int32, (n_cmp, 1), 0) * CMP_STRIDE + (CMP_BLK - 1)
    s_c = s_c + per_head(jnp.where(c_end <= t_q, 0.0, MASK_VALUE))
    mx_c = jnp.max(s_c, axis=0, keepdims=True)
    p_c = jnp.exp2(s_c - mx_c)
    l_c = jnp.sum(p_c, axis=0, keepdims=True)
    p_c = p_c * jnp.where(t_row >= CMP_BLK - 1, 1.0 / l_c, 0.0)
    o_c = jnp.dot(vcT_ref[0, 0], p_c.astype(BF16), preferred_element_type=F32)
    for r in range(NSA_REP):
        part_ref[r * dk:(r + 1) * dk, :] = gates[3 * r:3 * r + 1] * o_c[:, r * tq:(r + 1) * tq]

    p_sum = p_c[:, 0:tq]
    for r in range(1, NSA_REP):
        p_sum = p_sum + p_c[:, r * tq:(r + 1) * tq]
    ov = ov_ref[...]
    imp = _dot_exact_lhs(ov, p_sum)
    n_blk = ov.shape[0]
    j_blk = lax.broadcasted_iota(jnp.int32, (n_blk, 1), 0)
    cur = t_q // SEL_BLK
    forced = (j_blk == 0) | (j_blk == cur) | (j_blk == cur - 1)
    valid = j_blk * SEL_BLK <= t_q
    n_rank = N_SEL - 3
    cand = jnp.where(forced, -jnp.inf, jnp.where(valid, imp, -1.0))

    n_wt = WINDOW // tq + 1
    row0 = lax.broadcasted_iota(jnp.int32, (NSA_AUX, rows), 0) == 0
    a_k = lax.broadcasted_iota(jnp.int32, (tq, 1), 0)
    b_q = lax.broadcasted_iota(jnp.int32, (1, tq), 1)
    w_tiles = [jnp.maximum(qt - (n_wt - 1) + i, 0) for i in range(n_wt)]

    def win_scores(i):
        before_start = qt - (n_wt - 1) + i < 0
        flag = jnp.where(row0 & before_start, MASK_VALUE, 0.0).astype(BF16)
        rhs_w = jnp.concatenate([qT, flag, zpad], axis=0)
        k_tile = kwa_ref[pl.ds(pl.multiple_of(w_tiles[i] * tq, tq), tq), :]
        s = jnp.dot(k_tile, rhs_w, preferred_element_type=F32)
        if i == 0:
            s = s + per_head(jnp.where(a_k > b_q, 0.0, MASK_VALUE))
        if i == n_wt - 1:
            s = s + per_head(jnp.where(a_k <= b_q, 0.0, MASK_VALUE))
        sw_ref[i * tq:(i + 1) * tq, :] = s

    def win_max():
        mw_ref[...] = jnp.max(sw_ref[...], axis=0, keepdims=True)

    def win_pv(i):
        p = jnp.exp2(sw_ref[i * tq:(i + 1) * tq, :] - mw_ref[...]).astype(BF16)
        pv = jnp.dot(vwT_ref[w_tiles[i]], p, preferred_element_type=F32)
        if i == 0:
            accw_ref[...] = pv
        else:
            accw_ref[...] = accw_ref[...] + pv

    win_steps = ([functools.partial(win_scores, i) for i in range(n_wt)] + [win_max]
                 + [functools.partial(win_pv, i) for i in range(n_wt)])

    work = cand
    maxima = []
    for r in range(n_rank):
        mx = jnp.max(work, axis=0, keepdims=True)
        maxima.append(mx)
        work = jnp.where(work == mx, -jnp.inf, work)
        if r < len(win_steps):
            win_steps[r]()
    for step in win_steps[n_rank:]:
        step()

    ones_lhs = jnp.ones((SUBLANES, n_blk), BF16)

    def count(mask):
        return jnp.dot(ones_lhs, jnp.where(mask, 1.0, 0.0).astype(BF16),
                       preferred_element_type=F32)[0:1]

    thr = maxima[-1]
    for mx in reversed(maxima[:-1]):
        thr = jnp.where(count(cand >= mx) >= n_rank, mx, thr)
    ties = cand == thr
    below = (lax.broadcasted_iota(jnp.int32, (n_blk, n_blk), 1)
             < lax.broadcasted_iota(jnp.int32, (n_blk, n_blk), 0))
    tie_rank = jnp.dot(jnp.where(below, 1.0, 0.0).astype(BF16),
                       jnp.where(ties, 1.0, 0.0).astype(BF16), preferred_element_type=F32)
    picked = (cand > thr) | (ties & (tie_rank < n_rank - count(cand > thr)))
    sel_ref[0:n_blk, :] = jnp.where(forced | (picked & (cand >= 0.0)), 0.0, MASK_VALUE)
    sel_ref[n_blk:, :] = jnp.zeros((NSA_AUX, tq), F32)

    acc_w = accw_ref[...]
    o_w = acc_w[:dk] * (1.0 / acc_w[dk:dk + 1])
    for r in range(NSA_REP):
        part_ref[r * dk:(r + 1) * dk, :] = (part_ref[r * dk:(r + 1) * dk, :]
                                            + gates[3 * r + 2:3 * r + 3] * o_w[:, r * tq:(r + 1) * tq])

    blk_per_tile = tk // SEL_BLK
    n_full = qs // tk

    def scores(kt, slot, diagonal=False):
        sb = sel_ref[pl.ds(pl.multiple_of(kt * blk_per_tile, blk_per_tile), NSA_AUX), :]
        rhs = jnp.concatenate([qT, per_head(sb).astype(BF16), zpad], axis=0)
        k_tile = ksa_ref[pl.ds(pl.multiple_of(kt * tk, tk), tk), :]
        s = jnp.dot(k_tile, rhs, preferred_element_type=F32)
        if diagonal:
            kpos = kt * tk + lax.broadcasted_iota(jnp.int32, (tk, 1), 0)
            s = s + per_head(jnp.where(kpos <= t_q, 0.0, MASK_VALUE))
        s_ref[slot] = s

    def softmax_pv(kt, slot):
        m_i = m_ref[...]
        m_new = jnp.maximum(m_i, jnp.max(s_ref[slot], axis=0, keepdims=True))
        alpha = jnp.exp2(m_i - m_new)
        p = jnp.exp2(s_ref[slot] - m_new).astype(BF16)
        pv = jnp.dot(vsT_ref[kt], p, preferred_element_type=F32)
        acc_ref[...] = alpha * acc_ref[...] + pv
        m_ref[...] = m_new

    m_ref[...] = jnp.full((1, rows), MASK_VALUE, F32)
    acc_ref[...] = jnp.zeros((dk + NSA_AUX, rows), F32)
    scores(n_full, 0, diagonal=True)

    n_pairs = n_full // 2

    def slot0_tile(i):
        return jnp.where(i == 0, n_full, 2 * i - 1)

    def pair_step(i, carry):
        scores(2 * i, 1)
        softmax_pv(slot0_tile(i), 0)
        scores(2 * i + 1, 0)
        softmax_pv(2 * i, 1)
        return carry

    def quad_step(i, carry):
        pair_step(2 * i, carry)
        return pair_step(2 * i + 1, carry)

    n_quads = n_pairs // 2
    lax.fori_loop(0, n_quads, quad_step, 0)
    lax.fori_loop(2 * n_quads, n_pairs, pair_step, 0)

    @pl.when(n_full % 2 == 1)
    def _():
        scores(2 * n_pairs, 1)
        softmax_pv(slot0_tile(n_pairs), 0)
        softmax_pv(2 * n_pairs, 1)

    @pl.when(n_full % 2 == 0)
    def _():
        softmax_pv(slot0_tile(n_pairs), 0)

    acc_s = acc_ref[...]
    o_s = acc_s[:dk] * (1.0 / acc_s[dk:dk + 1])
    gates = gate_ref[pl.ds(pl.multiple_of(grp * NSA_AUX, NSA_AUX), NSA_AUX), :]
    out_t = jnp.concatenate(
        [part_ref[r * dk:(r + 1) * dk, :] + gates[3 * r + 1:3 * r + 2] * o_s[:, r * tq:(r + 1) * tq]
         for r in range(NSA_REP)], axis=0).astype(BF16)
    o_ref[0] = lax.dot_general(eye[:tq, :tq], out_t, NT_DIMS,
                               preferred_element_type=F32).astype(o_ref.dtype)


def _nsa_attn(p_att3, aux3, small_block, kc, vcT, overlap_t, eye, aux_s):
    bsz, seq, _ = p_att3.shape
    n_grp, dk = NSA_KV_HEADS, NSA_HEAD_DIM
    tq, tk = NSA_TQ, NSA_TK
    rows = NSA_REP * tq
    q_w = NSA_REP * dk
    n_cmp = kc.shape[2]
    n_blk = seq // SEL_BLK
    v_rows = dk + NSA_AUX
    n_wt = WINDOW // tq + 1
    first = NSA_HEADS * dk // LANES
    kv_spec = lambda off: pl.BlockSpec((1, seq, LANES), lambda b, g, t: (b, 0, first + off + g))
    step_rows = NSA_TILES_PER_STEP * tq
    return pl.pallas_call(
        _nsa_attn_kernel,
        grid=(bsz, n_grp, seq // step_rows),
        in_specs=[
            pl.BlockSpec((1, step_rows, q_w), lambda b, g, t: (b, t, g)),
            kv_spec(0), kv_spec(n_grp), kv_spec(2 * n_grp),
            pl.BlockSpec((1, 1, n_cmp, dk), lambda b, g, t: (b, g, 0, 0)),
            pl.BlockSpec((1, 1, dk, n_cmp), lambda b, g, t: (b, g, 0, 0)),
            pl.BlockSpec((n_blk, n_cmp), lambda b, g, t: (0, 0)),
            pl.BlockSpec((1, step_rows, LANES), lambda b, g, t: (b, t, small_block)),
            pl.BlockSpec(eye.shape, lambda b, g, t: (0, 0)),
            pl.BlockSpec((tk, LANES), lambda b, g, t: (0, 0)),
        ],
        out_specs=pl.BlockSpec((1, step_rows, q_w), lambda b, g, t: (b, t, g)),
        out_shape=jax.ShapeDtypeStruct((bsz, seq, NSA_HEADS * dk), BF16),
        scratch_shapes=[pltpu.VMEM((seq, LANES), BF16),
                        pltpu.VMEM((seq, LANES), BF16),
                        pltpu.VMEM((seq // tk, v_rows, tk), BF16),
                        pltpu.VMEM((seq // tq, v_rows, tq), BF16),
                        pltpu.VMEM((n_blk + NSA_AUX, tq), F32),
                        pltpu.VMEM((2, tk, rows), F32),
                        pltpu.VMEM((1, rows), F32),
                        pltpu.VMEM((v_rows, rows), F32),
                        pltpu.VMEM((n_wt * tq, rows), F32),
                        pltpu.VMEM((1, rows), F32),
                        pltpu.VMEM((v_rows, rows), F32),
                        pltpu.VMEM((q_w, tq), F32),
                        pltpu.VMEM((LANES, tq), F32)],
        compiler_params=_params(("parallel", "parallel", "arbitrary")), name="nsa_attention",
    )(p_att3, p_att3, p_att3, p_att3, kc, vcT, overlap_t, aux3, eye, aux_s)


def _bmm(a, b, dims):
    a = a.astype(BF16)
    b = b.astype(BF16)
    return jnp.stack([lax.dot_general(a[n], b[n], dims, preferred_element_type=F32)
                      for n in range(a.shape[0])])


def _gdn_kernel(y_ref, z_ref, small_ref, alog_ref, dtb_ref, nw_ref, o_ref, state_ref):
    bsz = y_ref.shape[0]
    cs = GDN_CHUNK
    hd = GDN_HEAD_DIM
    nh = GDN_HEADS
    c = pl.program_id(0)

    @pl.when(c == 0)
    def _():
        state_ref[...] = jnp.zeros_like(state_ref)

    y = y_ref[...]

    small = small_ref[...]
    sp_in = small + dtb_ref[...]
    softplus = jnp.maximum(sp_in, 0.0) + jnp.log1p(jnp.exp(-jnp.abs(sp_in)))
    g_all = -jnp.exp(alog_ref[...]) * softplus
    beta_all = jax.nn.sigmoid(small)
    ii = lax.broadcasted_iota(jnp.int32, (cs, cs), 0)
    jj = lax.broadcasted_iota(jnp.int32, (cs, cs), 1)
    incl = ii >= jj
    strict = ii > jj
    tril = jnp.where(incl, 1.0, 0.0).astype(BF16)
    e128 = jnp.where(lax.broadcasted_iota(jnp.int32, (LANES, LANES), 0)
                     == lax.broadcasted_iota(jnp.int32, (LANES, LANES), 1), 1.0, 0.0).astype(BF16)
    gc_all = [_dot_exact_lhs(tril, g_all[b]) for b in range(bsz)]
    gcT_all = [_dot_exact_lhs(e128, gc_all[b], NT_DIMS) for b in range(bsz)]

    chains = [(b, h) for b in range(bsz) for h in range(nh)]

    def heads(a, base):
        return jnp.stack([a[b][:, base + h * hd:base + (h + 1) * hd] for b, h in chains])

    def lane_bcast(cols, base):
        return jnp.stack([jnp.broadcast_to(cols[b][:, base + h:base + h + 1], (cs, hd))
                          for b, h in chains])

    n_qk = nh * hd
    q = heads(y, 0)
    k = heads(y, n_qk)
    v = heads(y, 2 * n_qk)
    q = q * lax.rsqrt(jnp.sum(q * q, axis=-1, keepdims=True) + NORM_EPS) * (hd ** -0.5)
    k = k * lax.rsqrt(jnp.sum(k * k, axis=-1, keepdims=True) + NORM_EPS)
    gc = lane_bcast(gc_all, SMALL_A)
    beta = lane_bcast([beta_all[b] for b in range(bsz)], SMALL_B)
    gc_row = jnp.stack([gcT_all[b][SMALL_A + h:SMALL_A + h + 1, :] for b, h in chains])

    decay = jnp.exp(jnp.where(incl, gc[:, :, :cs] - gc_row, MASK_VALUE))
    kk = _bmm(k, k, NT_DIMS)
    m_full = jnp.where(strict, -(beta[:, :, :cs] * kk * decay), 0.0)
    blk_diff = ii ^ jj
    inv_m1 = jnp.where(blk_diff < 2, m_full, 0.0)
    size = 2
    while size < cs:
        m_off = jnp.where((blk_diff >= size) & (blk_diff < 2 * size), m_full, 0.0)
        t_mat = m_off + _bmm(inv_m1, m_off, NN_DIMS)
        inv_m1 = inv_m1 + t_mat + _bmm(t_mat, inv_m1, NN_DIMS)
        size *= 2
    e_gc = jnp.exp(gc)
    rhs = jnp.concatenate([v * beta, k * (beta * e_gc)], axis=2)
    sol = rhs + _bmm(inv_m1, rhs, NN_DIMS)
    u, w = sol[:, :, :hd], sol[:, :, hd:]
    qk = _bmm(q, k, NT_DIMS) * decay
    q_dec = q * e_gc
    gc_last = gc[:, cs - 1:cs, :]
    k_dec = k * jnp.exp(gc_last - gc)
    state = state_ref[...]
    v_new = u - _bmm(w, state, NN_DIMS)
    o = _bmm(q_dec, state, NN_DIMS) + _bmm(qk, v_new, NN_DIMS)
    state_ref[...] = state * jnp.exp(gc_last) + _bmm(k_dec, v_new, TN_DIMS)

    o = o * lax.rsqrt(jnp.mean(o * o, axis=-1, keepdims=True) + NORM_EPS) * nw_ref[...]
    z = heads(z_ref[...].astype(F32), 0)
    o = o * (z * jax.nn.sigmoid(z))
    for b in range(bsz):
        o_ref[b] = jnp.concatenate([o[b * nh + h] for h in range(nh)], axis=1).astype(o_ref.dtype)


def _gated_deltanet(qkv3, zm3, aux3, small_block, alog_pad, dtb_pad, norm_w):
    bsz, seq, qkv_cols = qkv3.shape
    cs = GDN_CHUNK
    z_cols = GDN_HEADS * GDN_HEAD_DIM
    out = pl.pallas_call(
        _gdn_kernel,
        grid=(seq // cs,),
        in_specs=[
            pl.BlockSpec((bsz, cs, qkv_cols), lambda c: (0, c, 0)),
            pl.BlockSpec((bsz, cs, z_cols), lambda c: (0, c, 0)),
            pl.BlockSpec((bsz, cs, LANES), lambda c: (0, c, small_block)),
            pl.BlockSpec((1, LANES), lambda c: (0, 0)),
            pl.BlockSpec((1, LANES), lambda c: (0, 0)),
            pl.BlockSpec((1, GDN_HEAD_DIM), lambda c: (0, 0)),
        ],
        out_specs=pl.BlockSpec((bsz, cs, z_cols), lambda c: (0, c, 0)),
        out_shape=jax.ShapeDtypeStruct((bsz, seq, z_cols), BF16),
        scratch_shapes=[pltpu.VMEM((bsz * GDN_HEADS, GDN_HEAD_DIM, GDN_HEAD_DIM), F32)],
        compiler_params=_params(("arbitrary",)), name="gated_deltanet",
    )(qkv3, zm3, aux3, alog_pad, dtb_pad, norm_w.reshape(1, GDN_HEAD_DIM))
    return out.reshape(bsz * seq, z_cols)


def _merge_kernel(ya_ref, yb_ref, ma_ref, mb_ref, x_ref, mod_ref, wa_ref, wb_ref, wo_ref,
                  nf_ref, x1_ref, h2_ref):
    pa = jnp.dot(ya_ref[...], wa_ref[...], preferred_element_type=F32)
    pb = jnp.dot(yb_ref[...], wb_ref[...], preferred_element_type=F32)
    mixed = (jax.nn.sigmoid(ma_ref[...].astype(F32)) * pa
             + jax.nn.sigmoid(mb_ref[...].astype(F32)) * pb)
    mod = mod_ref[0]
    x1 = x_ref[...] + mod[GT_M:GT_M + 1, :] * jnp.dot(
        mixed.astype(BF16), wo_ref[...], preferred_element_type=F32)
    x1_ref[...] = x1
    h2_ref[...] = _modulated_norm(x1, nf_ref[...], mod, SC_F, SH_F).astype(BF16)


def _merge_project(ya, yb, pc, merge_blk, x2, mod, w_pa, w_pb, w_o, norm_ffn, seq):
    t, d = x2.shape
    tiles_per_batch = seq // ROW_TILE
    row = lambda i: (i, 0)
    const = lambda i: (0, 0)
    return pl.pallas_call(
        _merge_kernel,
        grid=(t // ROW_TILE,),
        in_specs=[pl.BlockSpec((ROW_TILE, d), row),
                  pl.BlockSpec((ROW_TILE, d), row),
                  pl.BlockSpec((ROW_TILE, d), lambda i: (i, merge_blk)),
                  pl.BlockSpec((ROW_TILE, d), lambda i: (i, merge_blk + 1)),
                  pl.BlockSpec((ROW_TILE, d), row),
                  pl.BlockSpec((1, 6, d), lambda i: (i // tiles_per_batch, 0, 0)),
                  pl.BlockSpec((d, d), const),
                  pl.BlockSpec((d, d), const),
                  pl.BlockSpec((d, d), const),
                  pl.BlockSpec((1, d), const)],
        out_specs=[pl.BlockSpec((ROW_TILE, d), row), pl.BlockSpec((ROW_TILE, d), row)],
        out_shape=[jax.ShapeDtypeStruct((t, d), F32), jax.ShapeDtypeStruct((t, d), BF16)],
        compiler_params=_params(("parallel",)), name="merge_project",
    )(ya, yb, pc, pc, x2, mod, w_pa, w_pb, w_o, norm_ffn.reshape(1, d))


def _ffn_kernel(h_ref, halo_ref, res_ref, mod_ref, wg_ref, wv_ref, cwg_ref, cwv_ref,
                cbg_ref, cbv_ref, wd_ref, nfin_ref, o_ref, ug_ref, uv_ref,
                *, tiles_per_batch, last):
    i = pl.program_id(0)
    tm = h_ref.shape[0]
    first_of_batch = (i % tiles_per_batch) == 0
    lhs = jnp.concatenate([halo_ref[...], h_ref[...]], axis=0)

    def conv_half(w_ref, cw_ref, cb_ref, u_ref):
        u_all = jnp.dot(lhs, w_ref[...], preferred_element_type=F32)
        u_ref[SUBLANES:, :] = u_all[2 * SUBLANES:]
        u_ref[0:SUBLANES, :] = jnp.where(first_of_batch, 0.0, u_all[SUBLANES:2 * SUBLANES])
        cw = cw_ref[...]
        out = cb_ref[...] + cw[FFN_CONV - 1:FFN_CONV, :] * u_ref[SUBLANES:, :]
        for j in range(FFN_CONV - 1):
            off = SUBLANES - (FFN_CONV - 1) + j
            out = out + cw[j:j + 1, :] * u_ref[off:off + tm, :]
        return out

    gate = conv_half(wg_ref, cwg_ref, cbg_ref, ug_ref)
    val = conv_half(wv_ref, cwv_ref, cbv_ref, uv_ref)
    act = gate * jax.nn.sigmoid(gate) * val
    down = jnp.dot(act.astype(BF16), wd_ref[...], preferred_element_type=F32)
    y = res_ref[...] + mod_ref[0][GT_F:GT_F + 1, :] * down
    if last:
        ms = jnp.mean(y * y, axis=-1, keepdims=True)
        y = y * lax.rsqrt(ms + NORM_EPS) * nfin_ref[...]
    o_ref[...] = y


def _conv_ffn(h2, x1, mod, w_up, conv_w, conv_b, w_down, norm_final, seq):
    t, d = x1.shape
    tm, tf = ROW_TILE, FFN_TF
    n_f = FFN_DIM // tf
    tiles_per_batch = seq // tm
    halo_per_tile = tm // SUBLANES
    row = lambda i: (i, 0)
    res = x1
    for f in range(n_f):
        col = lambda i, c=f: (0, c)
        col_val = lambda i, c=n_f + f: (0, c)
        res = pl.pallas_call(
            functools.partial(_ffn_kernel, tiles_per_batch=tiles_per_batch, last=f == n_f - 1),
            grid=(t // tm,),
            in_specs=[pl.BlockSpec((tm, d), row),
                      pl.BlockSpec((SUBLANES * 2, d),
                                   lambda i: (jnp.maximum(i * (halo_per_tile // 2) - 1, 0), 0)),
                      pl.BlockSpec((tm, d), row),
                      pl.BlockSpec((1, 6, d), lambda i: (i // tiles_per_batch, 0, 0)),
                      pl.BlockSpec((d, tf), col),
                      pl.BlockSpec((d, tf), col_val),
                      pl.BlockSpec((FFN_CONV, tf), col),
                      pl.BlockSpec((FFN_CONV, tf), col_val),
                      pl.BlockSpec((1, tf), col),
                      pl.BlockSpec((1, tf), col_val),
                      pl.BlockSpec((tf, d), lambda i, c=f: (c, 0)),
                      pl.BlockSpec((1, d), lambda i: (0, 0))],
            out_specs=pl.BlockSpec((tm, d), row),
            out_shape=jax.ShapeDtypeStruct((t, d), F32),
            scratch_shapes=[pltpu.VMEM((tm + SUBLANES, tf), F32),
                            pltpu.VMEM((tm + SUBLANES, tf), F32)],
            compiler_params=_params(("parallel",)), name="conv_ffn",
        )(h2, h2, res, mod, w_up, w_up, conv_w, conv_w, conv_b, conv_b, w_down,
          norm_final.reshape(1, d))
    return res


def _overlap_t(seq):
    n_cmp = (seq - CMP_BLK) // CMP_STRIDE + 1
    n_blk = seq // SEL_BLK
    c_start = np.arange(n_cmp) * CMP_STRIDE
    s_start = np.arange(n_blk) * SEL_BLK
    ov = np.clip(np.minimum(c_start[:, None] + CMP_BLK, s_start[None, :] + SEL_BLK)
                 - np.maximum(c_start[:, None], s_start[None, :]), 0, None) / CMP_BLK
    n_half = seq // CMP_STRIDE
    out = np.zeros((n_blk, n_half), np.float32)
    out[:, :n_cmp] = ov.T
    return jnp.asarray(out, BF16)


def kernel(x, c, w_ada, b_ada, norm_mix, w_in, nsa_pos_k, nsa_pos_v, nsa_ck_w1, nsa_ck_b1, nsa_ck_w2, nsa_ck_b2, nsa_cv_w1, nsa_cv_b1, nsa_cv_w2, nsa_cv_b2, gdn_conv, gdn_a_log, gdn_dt_bias, gdn_norm, w_proj_nsa, w_proj_gdn, w_out, norm_ffn, ffn_up, ffn_conv, ffn_conv_b, ffn_down, norm_final):
    bsz, seq, d = x.shape
    depth = w_ada.shape[0]
    n_grp, dk = NSA_KV_HEADS, NSA_HEAD_DIM
    q_cols = NSA_HEADS * dk
    gqkv_cols = 3 * GDN_HEADS * GDN_HEAD_DIM
    z_cols = GDN_HEADS * GDN_HEAD_DIM
    o_kv = q_cols
    o_gl = o_kv + 6 * n_grp * dk
    o_gq = o_gl + 3 * NSA_HEADS
    o_a = o_gq + gqkv_cols
    o_z = o_a + 2 * GDN_HEADS
    overlap_t = _overlap_t(seq)
    x2 = x.reshape(bsz * seq, d)
    assert depth == 1, "kernel supports the problem's DEPTH == 1"
    for l in range(depth):
        mod = _adaln_mod(c, w_ada[l], b_ada[l])
        w = w_in[l]
        kv_w = w[:, o_kv:o_gl].reshape(d, 6, n_grp, dk)
        zero = jnp.zeros((d, n_grp, dk), F32)
        pair = lambda a, b: jnp.concatenate([a, b], axis=2).reshape(d, n_grp * 2 * dk)
        w_att = jnp.concatenate([w[:, :o_kv] * (dk ** -0.5 * LOG2_E), pair(kv_w[:, 2], zero),
                                 pair(kv_w[:, 4], zero), pair(kv_w[:, 3], kv_w[:, 5])],
                                axis=1).astype(BF16)
        gate_w = jnp.pad(w[:, o_gl:o_gq].reshape(d, n_grp, 3 * NSA_REP),
                         ((0, 0), (0, 0), (0, NSA_AUX - 3 * NSA_REP))).reshape(d, SMALL_A)
        w_small = jnp.concatenate([gate_w, w[:, o_a:o_z],
                                   jnp.zeros((d, LANES - SMALL_B - GDN_HEADS), F32)],
                                  axis=1).astype(BF16)
        w_aux = jnp.concatenate([pair(kv_w[:, 0], kv_w[:, 1]).astype(BF16), w_small], axis=1)
        small_block = n_grp
        w_zm = w[:, o_z:].astype(BF16)

        p_att, h = _norm_mod_matmul(x2, norm_mix[l], mod, w_att, 1280, BF16, seq)
        qkv = _matmul_conv_silu(h, w[:, o_gq:o_a].astype(BF16), gdn_conv[l], 1536, seq)
        zm = _matmul(h, w_zm, 1536, BF16)
        aux = _matmul(h, w_aux, w_aux.shape[1], F32)
        aux3 = aux.reshape(bsz, seq, aux.shape[1])

        w1k = nsa_ck_w1[l].reshape(2, CMP_STRIDE, dk, CMP_HIDDEN)
        w1v = nsa_cv_w1[l].reshape(2, CMP_STRIDE, dk, CMP_HIDDEN)
        zw = jnp.zeros((CMP_STRIDE, dk, 2 * CMP_HIDDEN), F32)
        wc = jnp.concatenate(
            [jnp.concatenate([w1k[0], w1k[1], zw], axis=2),
             jnp.concatenate([zw, w1v[0], w1v[1]], axis=2)], axis=1).astype(BF16)
        pos = jnp.stack([nsa_pos_k[l], nsa_pos_v[l]]).reshape(2, 1, CMP_BLK * dk)
        pos = jnp.broadcast_to(pos, (2, 2 * SUBLANES, CMP_BLK * dk)).astype(BF16)
        cw1 = jnp.stack([nsa_ck_w1[l], nsa_cv_w1[l]]).astype(BF16)
        cb1 = jnp.stack([nsa_ck_b1[l], nsa_cv_b1[l]]).reshape(2, 1, CMP_HIDDEN)
        cw2 = jnp.stack([nsa_ck_w2[l], nsa_cv_w2[l]]).astype(BF16)
        cb2 = jnp.stack([nsa_ck_b2[l], nsa_cv_b2[l]]).reshape(2, 1, dk)
        eye = jnp.eye(max(NSA_REP * dk, NSA_TQ), dtype=BF16)
        kc, vcT = _compress(aux3, wc, pos, cw1, cb1, cw2, cb2, eye, 0)

        blk_in_tile = np.arange(NSA_TK) // SEL_BLK
        aux_s = np.zeros((NSA_TK, LANES), np.float32)
        aux_s[np.arange(NSA_TK), dk + blk_in_tile] = 1.0
        y_a = _nsa_attn(p_att.reshape(bsz, seq, p_att.shape[1]), aux3, small_block, kc, vcT,
                        overlap_t, eye, jnp.asarray(aux_s, BF16)).reshape(bsz * seq, q_cols)

        alog_pad = jnp.zeros((1, LANES), F32).at[0, SMALL_A:SMALL_A + GDN_HEADS].set(gdn_a_log[l])
        dtb_pad = jnp.zeros((1, LANES), F32).at[0, SMALL_A:SMALL_A + GDN_HEADS].set(gdn_dt_bias[l])
        y_b = _gated_deltanet(qkv.reshape(bsz, seq, gqkv_cols), zm.reshape(bsz, seq, zm.shape[1]),
                              aux3, small_block, alog_pad, dtb_pad, gdn_norm[l])

        x1, h2 = _merge_project(y_a, y_b, zm, z_cols // d, x2, mod, w_proj_nsa[l].astype(BF16),
                                w_proj_gdn[l].astype(BF16), w_out[l].astype(BF16),
                                norm_ffn[l], seq)
        out = _conv_ffn(h2, x1, mod, ffn_up[l].astype(BF16), ffn_conv[l],
                        ffn_conv_b[l].reshape(1, 2 * FFN_DIM), ffn_down[l].astype(BF16),
                        norm_final, seq)
    return out.reshape(bsz, seq, d)
```

```python
import functools

import numpy as np
import jax
import jax.numpy as jnp
from jax import lax
from jax.experimental import pallas as pl
from jax.experimental.pallas import tpu as pltpu

F32 = jnp.float32
BF16 = jnp.bfloat16
HIGHEST = lax.Precision.HIGHEST

NSA_HEAD_DIM = 64
NSA_HEADS = 16
NSA_KV_HEADS = 4
NSA_REP = NSA_HEADS // NSA_KV_HEADS
CMP_BLK = 32
CMP_STRIDE = 16
CMP_HIDDEN = 256
SEL_BLK = 64
N_SEL = 16
WINDOW = 512
GDN_HEAD_DIM = 128
GDN_HEADS = 8
GDN_CONV = 4
GDN_CHUNK = 64
FFN_DIM = 2816
FFN_CONV = 3
NORM_EPS = 1e-6
MASK_VALUE = -1e30

LANES = 128
SUBLANES = 8
VMEM_LIMIT = 56 * 1024 * 1024

NSA_TQ = 256
NSA_TILES_PER_STEP = 4
NSA_TK = 512
NSA_AUX = 16
LOG2_E = 1.4426950408889634
GDN_CHUNKS_PER_STEP = 4
PROJ_TILE = 1024
PROJ_COLS_ATT = 1280
PROJ_COLS_GDN = 1536
ROW_TILE = 512
FFN_TF = 1408

SH_M, SC_M, GT_M, SH_F, SC_F, GT_F = range(6)
SMALL_A = NSA_KV_HEADS * NSA_AUX
SMALL_B = SMALL_A + GDN_HEADS


def _params(sem):
    return pltpu.CompilerParams(dimension_semantics=sem, vmem_limit_bytes=VMEM_LIMIT)


def _split3(a):
    hi = a.astype(BF16)
    r1 = a - hi.astype(F32)
    mid = r1.astype(BF16)
    lo = (r1 - mid.astype(F32)).astype(BF16)
    return hi, mid, lo


def _dot_exact_lhs(a_bf16, b, dims=None):
    if dims is None:
        dims = (((a_bf16.ndim - 1,), (0,)), ((), ()))
    out = None
    for part in _split3(b):
        term = lax.dot_general(a_bf16, part, dimension_numbers=dims, preferred_element_type=F32)
        out = term if out is None else out + term
    return out


NT_DIMS = (((1,), (1,)), ((), ()))
TN_DIMS = (((0,), (0,)), ((), ()))
NN_DIMS = (((1,), (0,)), ((), ()))


def _mod_kernel(c_ref, w_ref, b_ref, o_ref):
    o_ref[...] = jnp.dot(c_ref[...], w_ref[...], preferred_element_type=F32,
                         precision=HIGHEST) + b_ref[...]


def _adaln_mod(c, w_ada, b_ada):
    bsz, d = c.shape
    n = w_ada.shape[1]
    c8 = jnp.zeros((SUBLANES, d), F32).at[:bsz].set(c)
    out = pl.pallas_call(
        _mod_kernel,
        grid=(n // d,),
        in_specs=[pl.BlockSpec((SUBLANES, d), lambda j: (0, 0)),
                  pl.BlockSpec((d, d), lambda j: (0, j)),
                  pl.BlockSpec((1, d), lambda j: (0, j))],
        out_specs=pl.BlockSpec((SUBLANES, d), lambda j: (0, j)),
        out_shape=jax.ShapeDtypeStruct((SUBLANES, n), F32),
        compiler_params=_params(("arbitrary",)), name="adaln_mod",
    )(c8, w_ada, b_ada.reshape(1, n))
    return out[:bsz].reshape(bsz, n // d, d)


def _modulated_norm(x, norm_w, mod, sc_row, sh_row):
    ms = jnp.mean(x * x, axis=-1, keepdims=True)
    y = x * lax.rsqrt(ms + NORM_EPS) * norm_w
    return y * (1.0 + mod[sc_row:sc_row + 1, :]) + mod[sh_row:sh_row + 1, :]


def _nmm_kernel(x_ref, nw_ref, mod_ref, w_ref, o_ref, h_ref):
    @pl.when(pl.program_id(1) == 0)
    def _():
        h = _modulated_norm(x_ref[...], nw_ref[...], mod_ref[0], SC_M, SH_M)
        h_ref[...] = h.astype(BF16)

    o_ref[...] = jnp.dot(h_ref[...], w_ref[...],
                         preferred_element_type=F32).astype(o_ref.dtype)


def _norm_mod_matmul(x2, norm_w, mod, w, tn, out_dtype, seq):
    t, d = x2.shape
    n = w.shape[1]
    tiles_per_batch = seq // PROJ_TILE
    return pl.pallas_call(
        _nmm_kernel,
        grid=(t // PROJ_TILE, n // tn),
        in_specs=[pl.BlockSpec((PROJ_TILE, d), lambda i, j: (i, 0)),
                  pl.BlockSpec((1, d), lambda i, j: (0, 0)),
                  pl.BlockSpec((1, 6, d), lambda i, j: (i // tiles_per_batch, 0, 0)),
                  pl.BlockSpec((d, tn), lambda i, j: (0, j))],
        out_specs=[pl.BlockSpec((PROJ_TILE, tn), lambda i, j: (i, j)),
                   pl.BlockSpec((PROJ_TILE, d), lambda i, j: (i, 0))],
        out_shape=[jax.ShapeDtypeStruct((t, n), out_dtype), jax.ShapeDtypeStruct((t, d), BF16)],
        compiler_params=_params(("parallel", "arbitrary")), name="norm_mod_proj",
    )(x2, norm_w.reshape(1, d), mod, w)


def _mm_kernel(h_ref, w_ref, o_ref):
    o_ref[...] = jnp.dot(h_ref[...], w_ref[...], preferred_element_type=F32).astype(o_ref.dtype)


def _matmul(h, w, tn, out_dtype):
    t, d = h.shape
    n = w.shape[1]
    return pl.pallas_call(
        _mm_kernel,
        grid=(n // tn, t // PROJ_TILE),
        in_specs=[pl.BlockSpec((PROJ_TILE, d), lambda j, i: (i, 0)),
                  pl.BlockSpec((d, tn), lambda j, i: (0, j))],
        out_specs=pl.BlockSpec((PROJ_TILE, tn), lambda j, i: (i, j)),
        out_shape=jax.ShapeDtypeStruct((t, n), out_dtype),
        compiler_params=_params(("parallel", "parallel")), name="proj",
    )(h, w)


def _mm_conv_kernel(h_ref, w_ref, cw_ref, o_ref, u_ref, *, tiles_per_batch):
    tm = h_ref.shape[0]
    width = cw_ref.shape[0]

    @pl.when(pl.program_id(1) % tiles_per_batch == 0)
    def _():
        u_ref[0:SUBLANES, :] = jnp.zeros((SUBLANES, u_ref.shape[1]), F32)

    cw = cw_ref[...]
    n_sub = 4
    sub = tm // n_sub

    def matmul(s):
        u_ref[SUBLANES + s * sub:SUBLANES + (s + 1) * sub, :] = jnp.dot(
            h_ref[s * sub:(s + 1) * sub, :], w_ref[...], preferred_element_type=F32)

    def conv(s):
        base = SUBLANES + s * sub
        y = cw[width - 1:width, :] * u_ref[base:base + sub, :]
        for i in range(width - 1):
            off = base - (width - 1) + i
            y = y + cw[i:i + 1, :] * u_ref[off:off + sub, :]
        o_ref[s * sub:(s + 1) * sub, :] = y * jax.nn.sigmoid(y)

    matmul(0)
    for s in range(1, n_sub):
        matmul(s)
        conv(s - 1)
    conv(n_sub - 1)
    u_ref[0:SUBLANES, :] = u_ref[tm:tm + SUBLANES, :]


def _matmul_conv_silu(h, w, conv_w, tn, seq):
    t, d = h.shape
    n = w.shape[1]
    return pl.pallas_call(
        functools.partial(_mm_conv_kernel, tiles_per_batch=seq // PROJ_TILE),
        grid=(n // tn, t // PROJ_TILE),
        in_specs=[pl.BlockSpec((PROJ_TILE, d), lambda j, i: (i, 0)),
                  pl.BlockSpec((d, tn), lambda j, i: (0, j)),
                  pl.BlockSpec((conv_w.shape[0], tn), lambda j, i: (0, j))],
        out_specs=pl.BlockSpec((PROJ_TILE, tn), lambda j, i: (i, j)),
        out_shape=jax.ShapeDtypeStruct((t, n), F32),
        scratch_shapes=[pltpu.VMEM((PROJ_TILE + SUBLANES, tn), F32)],
        compiler_params=_params(("parallel", "arbitrary")), name="proj_conv_silu",
    )(h, w, conv_w)


def _compress_kernel(t_ref, wc_ref, pos_ref, w1_ref, b1_ref, w2_ref, b2_ref, eye_ref,
                     kc_ref, vcT_ref):
    n_half = t_ref.shape[1] // CMP_STRIDE
    acc = None
    for l in range(CMP_STRIDE):
        tok = t_ref[0, pl.ds(l, n_half, stride=CMP_STRIDE), :].astype(BF16)
        term = jnp.dot(tok, wc_ref[l], preferred_element_type=F32)
        acc = term if acc is None else acc + term
    outs = []
    for kind in range(2):
        top = acc[:, 2 * kind * CMP_HIDDEN:(2 * kind + 1) * CMP_HIDDEN]
        bot = acc[:, (2 * kind + 1) * CMP_HIDDEN:(2 * kind + 2) * CMP_HIDDEN]
        bot_next = pltpu.roll(bot, n_half - 1, 0)
        pos_term = jnp.dot(pos_ref[kind], w1_ref[kind], preferred_element_type=F32)[0:1]
        hid = jax.nn.gelu(top + bot_next + pos_term + b1_ref[kind])
        outs.append(jnp.dot(hid.astype(BF16), w2_ref[kind], preferred_element_type=F32)
                    + b2_ref[kind])
    kc_ref[0, 0] = outs[0].astype(BF16)
    dk = NSA_HEAD_DIM
    vcT_ref[0, 0] = lax.dot_general(eye_ref[:dk, :dk], outs[1].astype(BF16), NT_DIMS,
                                    preferred_element_type=F32).astype(BF16)


def _compress(pc3, wc, pos, w1, b1, w2, b2, eye, first_col_block):
    bsz, seq, _ = pc3.shape
    n_grp, dk = NSA_KV_HEADS, NSA_HEAD_DIM
    n_half = seq // CMP_STRIDE
    const3 = lambda b, g: (0, 0, 0)
    return pl.pallas_call(
        _compress_kernel,
        grid=(bsz, n_grp),
        in_specs=[pl.BlockSpec((1, seq, LANES), lambda b, g: (b, 0, first_col_block + g)),
                  pl.BlockSpec(wc.shape, const3),
                  pl.BlockSpec(pos.shape, const3),
                  pl.BlockSpec(w1.shape, const3),
                  pl.BlockSpec(b1.shape, const3),
                  pl.BlockSpec(w2.shape, const3),
                  pl.BlockSpec(b2.shape, const3),
                  pl.BlockSpec(eye.shape, lambda b, g: (0, 0))],
        out_specs=[pl.BlockSpec((1, 1, n_half, dk), lambda b, g: (b, g, 0, 0)),
                   pl.BlockSpec((1, 1, dk, n_half), lambda b, g: (b, g, 0, 0))],
        out_shape=[jax.ShapeDtypeStruct((bsz, n_grp, n_half, dk), BF16),
                   jax.ShapeDtypeStruct((bsz, n_grp, dk, n_half), BF16)],
        compiler_params=_params(("parallel", "parallel")), name="nsa_compress",
    )(pc3, wc, pos, w1, b1, w2, b2, eye)


def _nsa_attn_kernel(q_ref, ks_ref, kw_ref, v_ref, kc_ref, vcT_ref, ov_ref, gl_ref, eye_ref,
                     aux_ref, o_ref, *scratch):
    tq = NSA_TQ
    first_tile = pl.program_id(2) * NSA_TILES_PER_STEP

    def one_tile(i, carry):
        rows_i = pl.ds(pl.multiple_of(i * tq, tq), tq)
        _nsa_query_tile(first_tile + i, q_ref.at[:, rows_i, :], ks_ref, kw_ref, v_ref, kc_ref,
                        vcT_ref, ov_ref, gl_ref.at[:, rows_i, :], eye_ref, aux_ref,
                        o_ref.at[:, rows_i, :], *scratch)
        return carry

    lax.fori_loop(0, NSA_TILES_PER_STEP, one_tile, 0)


def _nsa_query_tile(qt, q_ref, ks_ref, kw_ref, v_ref, kc_ref, vcT_ref, ov_ref, gl_ref, eye_ref,
                    aux_ref, o_ref,
                    ksa_ref, kwa_ref, vsT_ref, vwT_ref, sel_ref, s_ref, m_ref, acc_ref,
                    sw_ref, mw_ref, accw_ref, part_ref, gate_ref):
    tq, tk = NSA_TQ, NSA_TK
    rows = NSA_REP * tq
    dk = NSA_HEAD_DIM
    grp = pl.program_id(1)
    qs = qt * tq
    seq = ks_ref.shape[1]
    eye = eye_ref[...]
    tq_per_tk = tk // tq

    @pl.when(qt == 0)
    def _():
        lane = lax.broadcasted_iota(jnp.int32, (tk, LANES), 1)
        lane_flag = jnp.where(lane == dk, 1.0, 0.0).astype(BF16)
        ones_rows = jnp.ones((NSA_AUX, tk), BF16)

        def build(j, carry):
            r0 = pl.multiple_of(j * tk, tk)
            ksa_ref[pl.ds(r0, tk), :] = ks_ref[0, pl.ds(r0, tk), :] + aux_ref[...]
            kwa_ref[pl.ds(r0, tk), :] = kw_ref[0, pl.ds(r0, tk), :] + lane_flag
            vT = lax.dot_general(eye[:LANES, :LANES], v_ref[0, pl.ds(r0, tk), :], NT_DIMS,
                                 preferred_element_type=F32).astype(BF16)
            vsT_ref[j, 0:dk, :] = vT[:dk]
            vsT_ref[j, dk:, :] = ones_rows
            for i in range(tq_per_tk):
                vwT_ref[j * tq_per_tk + i, 0:dk, :] = vT[dk:, i * tq:(i + 1) * tq]
                vwT_ref[j * tq_per_tk + i, dk:, :] = ones_rows[:, :tq]
            return carry

        lax.fori_loop(0, seq // tk, build, 0)

    q4 = lax.dot_general(eye[:NSA_REP * dk, :NSA_REP * dk], q_ref[0], NT_DIMS,
                         preferred_element_type=F32).astype(BF16)
    qT = jnp.concatenate([q4[r * dk:(r + 1) * dk, :] for r in range(NSA_REP)], axis=1)
    t_q = qs + lax.broadcasted_iota(jnp.int32, (1, tq), 1)
    t_row = jnp.concatenate([t_q] * NSA_REP, axis=1)
    zpad = jnp.zeros((LANES - dk - NSA_AUX, rows), BF16)

    def per_head(a):
        return jnp.concatenate([a] * NSA_REP, axis=1)

    gate_ref[...] = _dot_exact_lhs(eye[:LANES, :LANES], jax.nn.sigmoid(gl_ref[0]), NT_DIMS)
    gates = gate_ref[pl.ds(pl.multiple_of(grp * NSA_AUX, NSA_AUX), NSA_AUX), :]

    n_cmp = kc_ref.shape[2]
    s_c = jnp.dot(kc_ref[0, 0], qT, preferred_element_type=F32)
    c_end = lax.broadcasted_iota(jnp.int32, (n_cmp, 1), 0) * CMP_STRIDE + (CMP_BLK - 1)
    s_c = s_c + per_head(jnp.where(c_end <= t_q, 0.0, MASK_VALUE))
    mx_c = jnp.max(s_c, axis=0, keepdims=True)
    p_c = jnp.exp2(s_c - mx_c)
    l_c = jnp.sum(p_c, axis=0, keepdims=True)
    p_c = p_c * jnp.where(t_row >= CMP_BLK - 1, 1.0 / l_c, 0.0)
    o_c = jnp.dot(vcT_ref[0, 0], p_c.astype(BF16), preferred_element_type=F32)
    for r in range(NSA_REP):
        part_ref[r * dk:(r + 1) * dk, :] = gates[3 * r:3 * r + 1] * o_c[:, r * tq:(r + 1) * tq]

    p_sum = p_c[:, 0:tq]
    for r in range(1, NSA_REP):
        p_sum = p_sum + p_c[:, r * tq:(r + 1) * tq]
    ov = ov_ref[...]
    imp = _dot_exact_lhs(ov, p_sum)
    n_blk = ov.shape[0]
    j_blk = lax.broadcasted_iota(jnp.int32, (n_blk, 1), 0)
    cur = t_q // SEL_BLK
    forced = (j_blk == 0) | (j_blk == cur) | (j_blk == cur - 1)
    valid = j_blk * SEL_BLK <= t_q
    n_rank = N_SEL - 3
    cand = jnp.where(forced, -jnp.inf, jnp.where(valid, imp, -1.0))

    n_wt = WINDOW // tq + 1
    row0 = lax.broadcasted_iota(jnp.int32, (NSA_AUX, rows), 0) == 0
    a_k = lax.broadcasted_iota(jnp.int32, (tq, 1), 0)
    b_q = lax.broadcasted_iota(jnp.int32, (1, tq), 1)
    w_tiles = [jnp.maximum(qt - (n_wt - 1) + i, 0) for i in range(n_wt)]

    def win_scores(i):
        before_start = qt - (n_wt - 1) + i < 0
        flag = jnp.where(row0 & before_start, MASK_VALUE, 0.0).astype(BF16)
        rhs_w = jnp.concatenate([qT, flag, zpad], axis=0)
        k_tile = kwa_ref[pl.ds(pl.multiple_of(w_tiles[i] * tq, tq), tq), :]
        s = jnp.dot(k_tile, rhs_w, preferred_element_type=F32)
        if i == 0:
            s = s + per_head(jnp.where(a_k > b_q, 0.0, MASK_VALUE))
        if i == n_wt - 1:
            s = s + per_head(jnp.where(a_k <= b_q, 0.0, MASK_VALUE))
        sw_ref[i * tq:(i + 1) * tq, :] = s

    def win_max():
        mw_ref[...] = jnp.max(sw_ref[...], axis=0, keepdims=True)

    def win_pv(i):
        p = jnp.exp2(sw_ref[i * tq:(i + 1) * tq, :] - mw_ref[...]).astype(BF16)
        pv = jnp.dot(vwT_ref[w_tiles[i]], p, preferred_element_type=F32)
        if i == 0:
            accw_ref[...] = pv
        else:
            accw_ref[...] = accw_ref[...] + pv

    win_steps = ([functools.partial(win_scores, i) for i in range(n_wt)] + [win_max]
                 + [functools.partial(win_pv, i) for i in range(n_wt)])

    work = cand
    maxima = []
    for r in range(n_rank):
        mx = jnp.max(work, axis=0, keepdims=True)
        maxima.append(mx)
        work = jnp.where(work == mx, -jnp.inf, work)
        if r < len(win_steps):
            win_steps[r]()
    for step in win_steps[n_rank:]:
        step()

    ones_lhs = jnp.ones((SUBLANES, n_blk), BF16)

    def count(mask):
        return jnp.dot(ones_lhs, jnp.where(mask, 1.0, 0.0).astype(BF16),
                       preferred_element_type=F32)[0:1]

    thr = maxima[-1]
    for mx in reversed(maxima[:-1]):
        thr = jnp.where(count(cand >= mx) >= n_rank, mx, thr)
    ties = cand == thr
    below = (lax.broadcasted_iota(jnp.int32, (n_blk, n_blk), 1)
             < lax.broadcasted_iota(jnp.int32, (n_blk, n_blk), 0))
    tie_rank = jnp.dot(jnp.where(below, 1.0, 0.0).astype(BF16),
                       jnp.where(ties, 1.0, 0.0).astype(BF16), preferred_element_type=F32)
    picked = (cand > thr) | (ties & (tie_rank < n_rank - count(cand > thr)))
    sel_ref[0:n_blk, :] = jnp.where(forced | (picked & (cand >= 0.0)), 0.0, MASK_VALUE)
    sel_ref[n_blk:, :] = jnp.zeros((NSA_AUX, tq), F32)

    acc_w = accw_ref[...]
    o_w = acc_w[:dk] * (1.0 / acc_w[dk:dk + 1])
    for r in range(NSA_REP):
        part_ref[r * dk:(r + 1) * dk, :] = (part_ref[r * dk:(r + 1) * dk, :]
                                            + gates[3 * r + 2:3 * r + 3] * o_w[:, r * tq:(r + 1) * tq])

    blk_per_tile = tk // SEL_BLK
    n_full = qs // tk

    def scores(kt, slot, diagonal=False):
        sb = sel_ref[pl.ds(pl.multiple_of(kt * blk_per_tile, blk_per_tile), NSA_AUX), :]
        rhs = jnp.concatenate([qT, per_head(sb).astype(BF16), zpad], axis=0)
        k_tile = ksa_ref[pl.ds(pl.multiple_of(kt * tk, tk), tk), :]
        s = jnp.dot(k_tile, rhs, preferred_element_type=F32)
        if diagonal:
            kpos = kt * tk + lax.broadcasted_iota(jnp.int32, (tk, 1), 0)
            s = s + per_head(jnp.where(kpos <= t_q, 0.0, MASK_VALUE))
        s_ref[slot] = s

    def softmax_pv(kt, slot):
        m_i = m_ref[...]
        m_new = jnp.maximum(m_i, jnp.max(s_ref[slot], axis=0, keepdims=True))
        alpha = jnp.exp2(m_i - m_new)
        p = jnp.exp2(s_ref[slot] - m_new).astype(BF16)
        pv = jnp.dot(vsT_ref[kt], p, preferred_element_type=F32)
        acc_ref[...] = alpha * acc_ref[...] + pv
        m_ref[...] = m_new

    m_ref[...] = jnp.full((1, rows), MASK_VALUE, F32)
    acc_ref[...] = jnp.zeros((dk + NSA_AUX, rows), F32)
    scores(n_full, 0, diagonal=True)

    n_pairs = n_full // 2

    def slot0_tile(i):
        return jnp.where(i == 0, n_full, 2 * i - 1)

    def pair_step(i, carry):
        scores(2 * i, 1)
        softmax_pv(slot0_tile(i), 0)
        scores(2 * i + 1, 0)
        softmax_pv(2 * i, 1)
        return carry

    def quad_step(i, carry):
        pair_step(2 * i, carry)
        return pair_step(2 * i + 1, carry)

    n_quads = n_pairs // 2
    lax.fori_loop(0, n_quads, quad_step, 0)
    lax.fori_loop(2 * n_quads, n_pairs, pair_step, 0)

    @pl.when(n_full % 2 == 1)
    def _():
        scores(2 * n_pairs, 1)
        softmax_pv(slot0_tile(n_pairs), 0)
        softmax_pv(2 * n_pairs, 1)

    @pl.when(n_full % 2 == 0)
    def _():
        softmax_pv(slot0_tile(n_pairs), 0)

    acc_s = acc_ref[...]
    o_s = acc_s[:dk] * (1.0 / acc_s[dk:dk + 1])
    gates = gate_ref[pl.ds(pl.multiple_of(grp * NSA_AUX, NSA_AUX), NSA_AUX), :]
    out_t = jnp.concatenate(
        [part_ref[r * dk:(r + 1) * dk, :] + gates[3 * r + 1:3 * r + 2] * o_s[:, r * tq:(r + 1) * tq]
         for r in range(NSA_REP)], axis=0).astype(BF16)
    o_ref[0] = lax.dot_general(eye[:tq, :tq], out_t, NT_DIMS,
                               preferred_element_type=F32).astype(o_ref.dtype)


def _nsa_attn(p_att3, aux3, small_block, kc, vcT, overlap_t, eye, aux_s):
    bsz, seq, _ = p_att3.shape
    n_grp, dk = NSA_KV_HEADS, NSA_HEAD_DIM
    tq, tk = NSA_TQ, NSA_TK
    rows = NSA_REP * tq
    q_w = NSA_REP * dk
    n_cmp = kc.shape[2]
    n_blk = seq // SEL_BLK
    v_rows = dk + NSA_AUX
    n_wt = WINDOW // tq + 1
    first = NSA_HEADS * dk // LANES
    kv_spec = lambda off: pl.BlockSpec((1, seq, LANES), lambda b, g, t: (b, 0, first + off + g))
    step_rows = NSA_TILES_PER_STEP * tq
    return pl.pallas_call(
        _nsa_attn_kernel,
        grid=(bsz, n_grp, seq // step_rows),
        in_specs=[
            pl.BlockSpec((1, step_rows, q_w), lambda b, g, t: (b, t, g)),
            kv_spec(0), kv_spec(n_grp), kv_spec(2 * n_grp),
            pl.BlockSpec((1, 1, n_cmp, dk), lambda b, g, t: (b, g, 0, 0)),
            pl.BlockSpec((1, 1, dk, n_cmp), lambda b, g, t: (b, g, 0, 0)),
            pl.BlockSpec((n_blk, n_cmp), lambda b, g, t: (0, 0)),
            pl.BlockSpec((1, step_rows, LANES), lambda b, g, t: (b, t, small_block)),
            pl.BlockSpec(eye.shape, lambda b, g, t: (0, 0)),
            pl.BlockSpec((tk, LANES), lambda b, g, t: (0, 0)),
        ],
        out_specs=pl.BlockSpec((1, step_rows, q_w), lambda b, g, t: (b, t, g)),
        out_shape=jax.ShapeDtypeStruct((bsz, seq, NSA_HEADS * dk), BF16),
        scratch_shapes=[pltpu.VMEM((seq, LANES), BF16),
                        pltpu.VMEM((seq, LANES), BF16),
                        pltpu.VMEM((seq // tk, v_rows, tk), BF16),
                        pltpu.VMEM((seq // tq, v_rows, tq), BF16),
                        pltpu.VMEM((n_blk + NSA_AUX, tq), F32),
                        pltpu.VMEM((2, tk, rows), F32),
                        pltpu.VMEM((1, rows), F32),
                        pltpu.VMEM((v_rows, rows), F32),
                        pltpu.VMEM((n_wt * tq, rows), F32),
                        pltpu.VMEM((1, rows), F32),
                        pltpu.VMEM((v_rows, rows), F32),
                        pltpu.VMEM((q_w, tq), F32),
                        pltpu.VMEM((LANES, tq), F32)],
        compiler_params=_params(("parallel", "parallel", "arbitrary")), name="nsa_attention",
    )(p_att3, p_att3, p_att3, p_att3, kc, vcT, overlap_t, aux3, eye, aux_s)


def _bmm(a, b, dims):
    a = a.astype(BF16)
    b = b.astype(BF16)
    return jnp.stack([lax.dot_general(a[n], b[n], dims, preferred_element_type=F32)
                      for n in range(a.shape[0])])


def _gdn_kernel(y_ref, z_ref, small_ref, alog_ref, dtb_ref, nw_ref, o_ref, state_ref):
    cs = GDN_CHUNK
    first_chunk = pl.program_id(0) * GDN_CHUNKS_PER_STEP

    def one_chunk(i, carry):
        rows_i = pl.ds(pl.multiple_of(i * cs, cs), cs)
        _gdn_chunk(first_chunk + i, y_ref.at[:, rows_i, :], z_ref.at[:, rows_i, :],
                   small_ref.at[:, rows_i, :], alog_ref, dtb_ref, nw_ref,
                   o_ref.at[:, rows_i, :], state_ref)
        return carry

    lax.fori_loop(0, GDN_CHUNKS_PER_STEP, one_chunk, 0)


def _gdn_chunk(c, y_ref, z_ref, small_ref, alog_ref, dtb_ref, nw_ref, o_ref, state_ref):
    bsz = y_ref.shape[0]
    cs = GDN_CHUNK
    hd = GDN_HEAD_DIM
    nh = GDN_HEADS

    @pl.when(c == 0)
    def _():
        state_ref[...] = jnp.zeros_like(state_ref)

    y = y_ref[...]

    small = small_ref[...]
    sp_in = small + dtb_ref[...]
    softplus = jnp.maximum(sp_in, 0.0) + jnp.log1p(jnp.exp(-jnp.abs(sp_in)))
    g_all = -jnp.exp(alog_ref[...]) * softplus
    beta_all = jax.nn.sigmoid(small)
    ii = lax.broadcasted_iota(jnp.int32, (cs, cs), 0)
    jj = lax.broadcasted_iota(jnp.int32, (cs, cs), 1)
    incl = ii >= jj
    strict = ii > jj
    tril = jnp.where(incl, 1.0, 0.0).astype(BF16)
    e128 = jnp.where(lax.broadcasted_iota(jnp.int32, (LANES, LANES), 0)
                     == lax.broadcasted_iota(jnp.int32, (LANES, LANES), 1), 1.0, 0.0).astype(BF16)
    gc_all = [_dot_exact_lhs(tril, g_all[b]) for b in range(bsz)]
    gcT_all = [_dot_exact_lhs(e128, gc_all[b], NT_DIMS) for b in range(bsz)]

    chains = [(b, h) for b in range(bsz) for h in range(nh)]

    def heads(a, base):
        return jnp.stack([a[b][:, base + h * hd:base + (h + 1) * hd] for b, h in chains])

    def lane_bcast(cols, base):
        return jnp.stack([jnp.broadcast_to(cols[b][:, base + h:base + h + 1], (cs, hd))
                          for b, h in chains])

    n_qk = nh * hd
    q = heads(y, 0)
    k = heads(y, n_qk)
    v = heads(y, 2 * n_qk)
    q = q * lax.rsqrt(jnp.sum(q * q, axis=-1, keepdims=True) + NORM_EPS) * (hd ** -0.5)
    k = k * lax.rsqrt(jnp.sum(k * k, axis=-1, keepdims=True) + NORM_EPS)
    gc = lane_bcast(gc_all, SMALL_A)
    beta = lane_bcast([beta_all[b] for b in range(bsz)], SMALL_B)
    gc_row = jnp.stack([gcT_all[b][SMALL_A + h:SMALL_A + h + 1, :] for b, h in chains])

    decay = jnp.exp(jnp.where(incl, gc[:, :, :cs] - gc_row, MASK_VALUE))
    kk = _bmm(k, k, NT_DIMS)
    m_full = jnp.where(strict, -(beta[:, :, :cs] * kk * decay), 0.0)
    blk_diff = ii ^ jj
    inv_m1 = jnp.where(blk_diff < 2, m_full, 0.0)
    size = 2
    while size < cs:
        m_off = jnp.where((blk_diff >= size) & (blk_diff < 2 * size), m_full, 0.0)
        t_mat = m_off + _bmm(inv_m1, m_off, NN_DIMS)
        inv_m1 = inv_m1 + t_mat + _bmm(t_mat, inv_m1, NN_DIMS)
        size *= 2
    e_gc = jnp.exp(gc)
    rhs = jnp.concatenate([v * beta, k * (beta * e_gc)], axis=2)
    sol = rhs + _bmm(inv_m1, rhs, NN_DIMS)
    u, w = sol[:, :, :hd], sol[:, :, hd:]
    qk = _bmm(q, k, NT_DIMS) * decay
    q_dec = q * e_gc
    gc_last = gc[:, cs - 1:cs, :]
    k_dec = k * jnp.exp(gc_last - gc)
    state = state_ref[...]
    v_new = u - _bmm(w, state, NN_DIMS)
    o = _bmm(q_dec, state, NN_DIMS) + _bmm(qk, v_new, NN_DIMS)
    state_ref[...] = state * jnp.exp(gc_last) + _bmm(k_dec, v_new, TN_DIMS)

    o = o * lax.rsqrt(jnp.mean(o * o, axis=-1, keepdims=True) + NORM_EPS) * nw_ref[...]
    z = heads(z_ref[...].astype(F32), 0)
    o = o * (z * jax.nn.sigmoid(z))
    for b in range(bsz):
        o_ref[b] = jnp.concatenate([o[b * nh + h] for h in range(nh)], axis=1).astype(o_ref.dtype)


def _gated_deltanet(qkv3, zm3, aux3, small_block, alog_pad, dtb_pad, norm_w):
    bsz, seq, qkv_cols = qkv3.shape
    cs = GDN_CHUNK
    z_cols = GDN_HEADS * GDN_HEAD_DIM
    step_rows = GDN_CHUNKS_PER_STEP * cs
    out = pl.pallas_call(
        _gdn_kernel,
        grid=(seq // step_rows,),
        in_specs=[
            pl.BlockSpec((bsz, step_rows, qkv_cols), lambda c: (0, c, 0)),
            pl.BlockSpec((bsz, step_rows, z_cols), lambda c: (0, c, 0)),
            pl.BlockSpec((bsz, step_rows, LANES), lambda c: (0, c, small_block)),
            pl.BlockSpec((1, LANES), lambda c: (0, 0)),
            pl.BlockSpec((1, LANES), lambda c: (0, 0)),
            pl.BlockSpec((1, GDN_HEAD_DIM), lambda c: (0, 0)),
        ],
        out_specs=pl.BlockSpec((bsz, step_rows, z_cols), lambda c: (0, c, 0)),
        out_shape=jax.ShapeDtypeStruct((bsz, seq, z_cols), BF16),
        scratch_shapes=[pltpu.VMEM((bsz * GDN_HEADS, GDN_HEAD_DIM, GDN_HEAD_DIM), F32)],
        compiler_params=_params(("arbitrary",)), name="gated_deltanet",
    )(qkv3, zm3, aux3, alog_pad, dtb_pad, norm_w.reshape(1, GDN_HEAD_DIM))
    return out.reshape(bsz * seq, z_cols)


def _merge_kernel(ya_ref, yb_ref, ma_ref, mb_ref, x_ref, mod_ref, wa_ref, wb_ref, wo_ref,
                  nf_ref, x1_ref, h2_ref):
    pa = jnp.dot(ya_ref[...], wa_ref[...], preferred_element_type=F32)
    pb = jnp.dot(yb_ref[...], wb_ref[...], preferred_element_type=F32)
    mixed = (jax.nn.sigmoid(ma_ref[...].astype(F32)) * pa
             + jax.nn.sigmoid(mb_ref[...].astype(F32)) * pb)
    mod = mod_ref[0]
    x1 = x_ref[...] + mod[GT_M:GT_M + 1, :] * jnp.dot(
        mixed.astype(BF16), wo_ref[...], preferred_element_type=F32)
    x1_ref[...] = x1
    h2_ref[...] = _modulated_norm(x1, nf_ref[...], mod, SC_F, SH_F).astype(BF16)


def _merge_project(ya, yb, pc, merge_blk, x2, mod, w_pa, w_pb, w_o, norm_ffn, seq):
    t, d = x2.shape
    tiles_per_batch = seq // ROW_TILE
    row = lambda i: (i, 0)
    const = lambda i: (0, 0)
    return pl.pallas_call(
        _merge_kernel,
        grid=(t // ROW_TILE,),
        in_specs=[pl.BlockSpec((ROW_TILE, d), row),
                  pl.BlockSpec((ROW_TILE, d), row),
                  pl.BlockSpec((ROW_TILE, d), lambda i: (i, merge_blk)),
                  pl.BlockSpec((ROW_TILE, d), lambda i: (i, merge_blk + 1)),
                  pl.BlockSpec((ROW_TILE, d), row),
                  pl.BlockSpec((1, 6, d), lambda i: (i // tiles_per_batch, 0, 0)),
                  pl.BlockSpec((d, d), const),
                  pl.BlockSpec((d, d), const),
                  pl.BlockSpec((d, d), const),
                  pl.BlockSpec((1, d), const)],
        out_specs=[pl.BlockSpec((ROW_TILE, d), row), pl.BlockSpec((ROW_TILE, d), row)],
        out_shape=[jax.ShapeDtypeStruct((t, d), F32), jax.ShapeDtypeStruct((t, d), BF16)],
        compiler_params=_params(("parallel",)), name="merge_project",
    )(ya, yb, pc, pc, x2, mod, w_pa, w_pb, w_o, norm_ffn.reshape(1, d))


def _ffn_kernel(h_ref, halo_ref, res_ref, mod_ref, wg_ref, wv_ref, cwg_ref, cwv_ref,
                cbg_ref, cbv_ref, wd_ref, nfin_ref, o_ref, ug_ref, uv_ref,
                *, tiles_per_batch, last):
    i = pl.program_id(0)
    tm = h_ref.shape[0]
    first_of_batch = (i % tiles_per_batch) == 0
    lhs = jnp.concatenate([halo_ref[...], h_ref[...]], axis=0)

    def conv_half(w_ref, cw_ref, cb_ref, u_ref):
        u_all = jnp.dot(lhs, w_ref[...], preferred_element_type=F32)
        u_ref[SUBLANES:, :] = u_all[2 * SUBLANES:]
        u_ref[0:SUBLANES, :] = jnp.where(first_of_batch, 0.0, u_all[SUBLANES:2 * SUBLANES])
        cw = cw_ref[...]
        out = cb_ref[...] + cw[FFN_CONV - 1:FFN_CONV, :] * u_ref[SUBLANES:, :]
        for j in range(FFN_CONV - 1):
            off = SUBLANES - (FFN_CONV - 1) + j
            out = out + cw[j:j + 1, :] * u_ref[off:off + tm, :]
        return out

    gate = conv_half(wg_ref, cwg_ref, cbg_ref, ug_ref)
    val = conv_half(wv_ref, cwv_ref, cbv_ref, uv_ref)
    act = gate * jax.nn.sigmoid(gate) * val
    down = jnp.dot(act.astype(BF16), wd_ref[...], preferred_element_type=F32)
    y = res_ref[...] + mod_ref[0][GT_F:GT_F + 1, :] * down
    if last:
        ms = jnp.mean(y * y, axis=-1, keepdims=True)
        y = y * lax.rsqrt(ms + NORM_EPS) * nfin_ref[...]
    o_ref[...] = y


def _conv_ffn(h2, x1, mod, w_up, conv_w, conv_b, w_down, norm_final, seq):
    t, d = x1.shape
    tm, tf = ROW_TILE, FFN_TF
    n_f = FFN_DIM // tf
    tiles_per_batch = seq // tm
    halo_per_tile = tm // SUBLANES
    row = lambda i: (i, 0)
    res = x1
    for f in range(n_f):
        col = lambda i, c=f: (0, c)
        col_val = lambda i, c=n_f + f: (0, c)
        res = pl.pallas_call(
            functools.partial(_ffn_kernel, tiles_per_batch=tiles_per_batch, last=f == n_f - 1),
            grid=(t // tm,),
            in_specs=[pl.BlockSpec((tm, d), row),
                      pl.BlockSpec((SUBLANES * 2, d),
                                   lambda i: (jnp.maximum(i * (halo_per_tile // 2) - 1, 0), 0)),
                      pl.BlockSpec((tm, d), row),
                      pl.BlockSpec((1, 6, d), lambda i: (i // tiles_per_batch, 0, 0)),
                      pl.BlockSpec((d, tf), col),
                      pl.BlockSpec((d, tf), col_val),
                      pl.BlockSpec((FFN_CONV, tf), col),
                      pl.BlockSpec((FFN_CONV, tf), col_val),
                      pl.BlockSpec((1, tf), col),
                      pl.BlockSpec((1, tf), col_val),
                      pl.BlockSpec((tf, d), lambda i, c=f: (c, 0)),
                      pl.BlockSpec((1, d), lambda i: (0, 0))],
            out_specs=pl.BlockSpec((tm, d), row),
            out_shape=jax.ShapeDtypeStruct((t, d), F32),
            scratch_shapes=[pltpu.VMEM((tm + SUBLANES, tf), F32),
                            pltpu.VMEM((tm + SUBLANES, tf), F32)],
            compiler_params=_params(("parallel",)), name="conv_ffn",
        )(h2, h2, res, mod, w_up, w_up, conv_w, conv_w, conv_b, conv_b, w_down,
          norm_final.reshape(1, d))
    return res


def _overlap_t(seq):
    n_cmp = (seq - CMP_BLK) // CMP_STRIDE + 1
    n_blk = seq // SEL_BLK
    c_start = np.arange(n_cmp) * CMP_STRIDE
    s_start = np.arange(n_blk) * SEL_BLK
    ov = np.clip(np.minimum(c_start[:, None] + CMP_BLK, s_start[None, :] + SEL_BLK)
                 - np.maximum(c_start[:, None], s_start[None, :]), 0, None) / CMP_BLK
    n_half = seq // CMP_STRIDE
    out = np.zeros((n_blk, n_half), np.float32)
    out[:, :n_cmp] = ov.T
    return jnp.asarray(out, BF16)


def kernel(x, c, w_ada, b_ada, norm_mix, w_in, nsa_pos_k, nsa_pos_v, nsa_ck_w1, nsa_ck_b1, nsa_ck_w2, nsa_ck_b2, nsa_cv_w1, nsa_cv_b1, nsa_cv_w2, nsa_cv_b2, gdn_conv, gdn_a_log, gdn_dt_bias, gdn_norm, w_proj_nsa, w_proj_gdn, w_out, norm_ffn, ffn_up, ffn_conv, ffn_conv_b, ffn_down, norm_final):
    bsz, seq, d = x.shape
    depth = w_ada.shape[0]
    n_grp, dk = NSA_KV_HEADS, NSA_HEAD_DIM
    q_cols = NSA_HEADS * dk
    gqkv_cols = 3 * GDN_HEADS * GDN_HEAD_DIM
    z_cols = GDN_HEADS * GDN_HEAD_DIM
    o_kv = q_cols
    o_gl = o_kv + 6 * n_grp * dk
    o_gq = o_gl + 3 * NSA_HEADS
    o_a = o_gq + gqkv_cols
    o_z = o_a + 2 * GDN_HEADS
    overlap_t = _overlap_t(seq)
    x2 = x.reshape(bsz * seq, d)
    assert depth == 1, "kernel supports the problem's DEPTH == 1"
    for l in range(depth):
        mod = _adaln_mod(c, w_ada[l], b_ada[l])
        w = w_in[l]
        kv_w = w[:, o_kv:o_gl].reshape(d, 6, n_grp, dk)
        zero = jnp.zeros((d, n_grp, dk), F32)
        pair = lambda a, b: jnp.concatenate([a, b], axis=2).reshape(d, n_grp * 2 * dk)
        w_att = jnp.concatenate([w[:, :o_kv] * (dk ** -0.5 * LOG2_E), pair(kv_w[:, 2], zero),
                                 pair(kv_w[:, 4], zero), pair(kv_w[:, 3], kv_w[:, 5])],
                                axis=1).astype(BF16)
        gate_w = jnp.pad(w[:, o_gl:o_gq].reshape(d, n_grp, 3 * NSA_REP),
                         ((0, 0), (0, 0), (0, NSA_AUX - 3 * NSA_REP))).reshape(d, SMALL_A)
        w_small = jnp.concatenate([gate_w, w[:, o_a:o_z],
                                   jnp.zeros((d, LANES - SMALL_B - GDN_HEADS), F32)],
                                  axis=1).astype(BF16)
        w_aux = jnp.concatenate([pair(kv_w[:, 0], kv_w[:, 1]).astype(BF16), w_small], axis=1)
        small_block = n_grp
        w_zm = w[:, o_z:].astype(BF16)

        p_att, h = _norm_mod_matmul(x2, norm_mix[l], mod, w_att, PROJ_COLS_ATT, BF16, seq)
        qkv = _matmul_conv_silu(h, w[:, o_gq:o_a].astype(BF16), gdn_conv[l], PROJ_COLS_GDN, seq)
        zm = _matmul(h, w_zm, PROJ_COLS_GDN, BF16)
        aux = _matmul(h, w_aux, w_aux.shape[1], F32)
        aux3 = aux.reshape(bsz, seq, aux.shape[1])

        w1k = nsa_ck_w1[l].reshape(2, CMP_STRIDE, dk, CMP_HIDDEN)
        w1v = nsa_cv_w1[l].reshape(2, CMP_STRIDE, dk, CMP_HIDDEN)
        zw = jnp.zeros((CMP_STRIDE, dk, 2 * CMP_HIDDEN), F32)
        wc = jnp.concatenate(
            [jnp.concatenate([w1k[0], w1k[1], zw], axis=2),
             jnp.concatenate([zw, w1v[0], w1v[1]], axis=2)], axis=1).astype(BF16)
        pos = jnp.stack([nsa_pos_k[l], nsa_pos_v[l]]).reshape(2, 1, CMP_BLK * dk)
        pos = jnp.broadcast_to(pos, (2, 2 * SUBLANES, CMP_BLK * dk)).astype(BF16)
        cw1 = jnp.stack([nsa_ck_w1[l], nsa_cv_w1[l]]).astype(BF16)
        cb1 = jnp.stack([nsa_ck_b1[l], nsa_cv_b1[l]]).reshape(2, 1, CMP_HIDDEN)
        cw2 = jnp.stack([nsa_ck_w2[l], nsa_cv_w2[l]]).astype(BF16)
        cb2 = jnp.stack([nsa_ck_b2[l], nsa_cv_b2[l]]).reshape(2, 1, dk)
        eye = jnp.eye(max(NSA_REP * dk, NSA_TQ), dtype=BF16)
        kc, vcT = _compress(aux3, wc, pos, cw1, cb1, cw2, cb2, eye, 0)

        blk_in_tile = np.arange(NSA_TK) // SEL_BLK
        aux_s = np.zeros((NSA_TK, LANES), np.float32)
        aux_s[np.arange(NSA_TK), dk + blk_in_tile] = 1.0
        y_a = _nsa_attn(p_att.reshape(bsz, seq, p_att.shape[1]), aux3, small_block, kc, vcT,
                        overlap_t, eye, jnp.asarray(aux_s, BF16)).reshape(bsz * seq, q_cols)

        alog_pad = jnp.zeros((1, LANES), F32).at[0, SMALL_A:SMALL_A + GDN_HEADS].set(gdn_a_log[l])
        dtb_pad = jnp.zeros((1, LANES), F32).at[0, SMALL_A:SMALL_A + GDN_HEADS].set(gdn_dt_bias[l])
        y_b = _gated_deltanet(qkv.reshape(bsz, seq, gqkv_cols), zm.reshape(bsz, seq, zm.shape[1]),
                              aux3, small_block, alog_pad, dtb_pad, gdn_norm[l])

        x1, h2 = _merge_project(y_a, y_b, zm, z_cols // d, x2, mod, w_proj_nsa[l].astype(BF16),
                                w_proj_gdn[l].astype(BF16), w_out[l].astype(BF16),
                                norm_ffn[l], seq)
        out = _conv_ffn(h2, x1, mod, ffn_up[l].astype(BF16), ffn_conv[l],
                        ffn_conv_b[l].reshape(1, 2 * FFN_DIM), ffn_down[l].astype(BF16),
                        norm_final, seq)
    return out.reshape(bsz, seq, d)
```

```python
import functools

import numpy as np
import jax
import jax.numpy as jnp
from jax import lax
from jax.experimental import pallas as pl
from jax.experimental.pallas import tpu as pltpu

F32 = jnp.float32
BF16 = jnp.bfloat16
HIGHEST = lax.Precision.HIGHEST

NSA_HEAD_DIM = 64
NSA_HEADS = 16
NSA_KV_HEADS = 4
NSA_REP = NSA_HEADS // NSA_KV_HEADS
CMP_BLK = 32
CMP_STRIDE = 16
CMP_HIDDEN = 256
SEL_BLK = 64
N_SEL = 16
WINDOW = 512
GDN_HEAD_DIM = 128
GDN_HEADS = 8
GDN_CONV = 4
GDN_CHUNK = 64
FFN_DIM = 2816
FFN_CONV = 3
NORM_EPS = 1e-6
MASK_VALUE = -1e30

LANES = 128
SUBLANES = 8
VMEM_LIMIT = 56 * 1024 * 1024

NSA_TQ = 256
NSA_TILES_PER_STEP = 4
NSA_TK = 512
PV_CHUNK = 256
NSA_AUX = 16
LOG2_E = 1.4426950408889634
PROJ_TILE = 1024
ROW_TILE = 512
FFN_TF = 1408

SH_M, SC_M, GT_M, SH_F, SC_F, GT_F = range(6)
SMALL_A = NSA_KV_HEADS * NSA_AUX
SMALL_B = SMALL_A + GDN_HEADS


def _params(sem):
    return pltpu.CompilerParams(dimension_semantics=sem, vmem_limit_bytes=VMEM_LIMIT)


def _split3(a):
    hi = a.astype(BF16)
    r1 = a - hi.astype(F32)
    mid = r1.astype(BF16)
    lo = (r1 - mid.astype(F32)).astype(BF16)
    return hi, mid, lo


def _dot_exact_lhs(a_bf16, b, dims=None):
    if dims is None:
        dims = (((a_bf16.ndim - 1,), (0,)), ((), ()))
    out = None
    for part in _split3(b):
        term = lax.dot_general(a_bf16, part, dimension_numbers=dims, preferred_element_type=F32)
        out = term if out is None else out + term
    return out


NT_DIMS = (((1,), (1,)), ((), ()))
TN_DIMS = (((0,), (0,)), ((), ()))
NN_DIMS = (((1,), (0,)), ((), ()))


def _mod_kernel(c_ref, w_ref, b_ref, o_ref):
    o_ref[...] = jnp.dot(c_ref[...], w_ref[...], preferred_element_type=F32,
                         precision=HIGHEST) + b_ref[...]


def _adaln_mod(c, w_ada, b_ada):
    bsz, d = c.shape
    n = w_ada.shape[1]
    c8 = jnp.zeros((SUBLANES, d), F32).at[:bsz].set(c)
    out = pl.pallas_call(
        _mod_kernel,
        grid=(n // d,),
        in_specs=[pl.BlockSpec((SUBLANES, d), lambda j: (0, 0)),
                  pl.BlockSpec((d, d), lambda j: (0, j)),
                  pl.BlockSpec((1, d), lambda j: (0, j))],
        out_specs=pl.BlockSpec((SUBLANES, d), lambda j: (0, j)),
        out_shape=jax.ShapeDtypeStruct((SUBLANES, n), F32),
        compiler_params=_params(("arbitrary",)), name="adaln_mod",
    )(c8, w_ada, b_ada.reshape(1, n))
    return out[:bsz].reshape(bsz, n // d, d)


def _modulated_norm(x, norm_w, mod, sc_row, sh_row):
    ms = jnp.mean(x * x, axis=-1, keepdims=True)
    y = x * lax.rsqrt(ms + NORM_EPS) * norm_w
    return y * (1.0 + mod[sc_row:sc_row + 1, :]) + mod[sh_row:sh_row + 1, :]


def _nmm_kernel(x_ref, nw_ref, mod_ref, w_ref, o_ref, h_ref):
    @pl.when(pl.program_id(1) == 0)
    def _():
        h = _modulated_norm(x_ref[...], nw_ref[...], mod_ref[0], SC_M, SH_M)
        h_ref[...] = h.astype(BF16)

    o_ref[...] = jnp.dot(h_ref[...], w_ref[...],
                         preferred_element_type=F32).astype(o_ref.dtype)


def _norm_mod_matmul(x2, norm_w, mod, w, tn, out_dtype, seq):
    t, d = x2.shape
    n = w.shape[1]
    tiles_per_batch = seq // PROJ_TILE
    return pl.pallas_call(
        _nmm_kernel,
        grid=(t // PROJ_TILE, n // tn),
        in_specs=[pl.BlockSpec((PROJ_TILE, d), lambda i, j: (i, 0)),
                  pl.BlockSpec((1, d), lambda i, j: (0, 0)),
                  pl.BlockSpec((1, 6, d), lambda i, j: (i // tiles_per_batch, 0, 0)),
                  pl.BlockSpec((d, tn), lambda i, j: (0, j))],
        out_specs=[pl.BlockSpec((PROJ_TILE, tn), lambda i, j: (i, j)),
                   pl.BlockSpec((PROJ_TILE, d), lambda i, j: (i, 0))],
        out_shape=[jax.ShapeDtypeStruct((t, n), out_dtype), jax.ShapeDtypeStruct((t, d), BF16)],
        compiler_params=_params(("parallel", "arbitrary")), name="norm_mod_proj",
    )(x2, norm_w.reshape(1, d), mod, w)


def _mm_kernel(h_ref, w_ref, o_ref):
    o_ref[...] = jnp.dot(h_ref[...], w_ref[...], preferred_element_type=F32).astype(o_ref.dtype)


def _matmul(h, w, tn, out_dtype):
    t, d = h.shape
    n = w.shape[1]
    return pl.pallas_call(
        _mm_kernel,
        grid=(n // tn, t // PROJ_TILE),
        in_specs=[pl.BlockSpec((PROJ_TILE, d), lambda j, i: (i, 0)),
                  pl.BlockSpec((d, tn), lambda j, i: (0, j))],
        out_specs=pl.BlockSpec((PROJ_TILE, tn), lambda j, i: (i, j)),
        out_shape=jax.ShapeDtypeStruct((t, n), out_dtype),
        compiler_params=_params(("parallel", "parallel")), name="proj",
    )(h, w)


def _mm_conv_kernel(h_ref, w_ref, cw_ref, o_ref, u_ref, *, tiles_per_batch):
    tm = h_ref.shape[0]
    width = cw_ref.shape[0]

    @pl.when(pl.program_id(1) % tiles_per_batch == 0)
    def _():
        u_ref[0:SUBLANES, :] = jnp.zeros((SUBLANES, u_ref.shape[1]), F32)

    cw = cw_ref[...]
    n_sub = 4
    sub = tm // n_sub

    def matmul(s):
        u_ref[SUBLANES + s * sub:SUBLANES + (s + 1) * sub, :] = jnp.dot(
            h_ref[s * sub:(s + 1) * sub, :], w_ref[...], preferred_element_type=F32)

    def conv(s):
        base = SUBLANES + s * sub
        y = cw[width - 1:width, :] * u_ref[base:base + sub, :]
        for i in range(width - 1):
            off = base - (width - 1) + i
            y = y + cw[i:i + 1, :] * u_ref[off:off + sub, :]
        o_ref[s * sub:(s + 1) * sub, :] = y * jax.nn.sigmoid(y)

    matmul(0)
    for s in range(1, n_sub):
        matmul(s)
        conv(s - 1)
    conv(n_sub - 1)
    u_ref[0:SUBLANES, :] = u_ref[tm:tm + SUBLANES, :]


def _matmul_conv_silu(h, w, conv_w, tn, seq):
    t, d = h.shape
    n = w.shape[1]
    return pl.pallas_call(
        functools.partial(_mm_conv_kernel, tiles_per_batch=seq // PROJ_TILE),
        grid=(n // tn, t // PROJ_TILE),
        in_specs=[pl.BlockSpec((PROJ_TILE, d), lambda j, i: (i, 0)),
                  pl.BlockSpec((d, tn), lambda j, i: (0, j)),
                  pl.BlockSpec((conv_w.shape[0], tn), lambda j, i: (0, j))],
        out_specs=pl.BlockSpec((PROJ_TILE, tn), lambda j, i: (i, j)),
        out_shape=jax.ShapeDtypeStruct((t, n), F32),
        scratch_shapes=[pltpu.VMEM((PROJ_TILE + SUBLANES, tn), F32)],
        compiler_params=_params(("parallel", "arbitrary")), name="proj_conv_silu",
    )(h, w, conv_w)


def _compress_kernel(t_ref, wc_ref, pos_ref, w1_ref, b1_ref, w2_ref, b2_ref, eye_ref,
                     kc_ref, vcT_ref):
    n_half = t_ref.shape[1] // CMP_STRIDE
    acc = None
    for l in range(CMP_STRIDE):
        tok = t_ref[0, pl.ds(l, n_half, stride=CMP_STRIDE), :].astype(BF16)
        term = jnp.dot(tok, wc_ref[l], preferred_element_type=F32)
        acc = term if acc is None else acc + term
    outs = []
    for kind in range(2):
        top = acc[:, 2 * kind * CMP_HIDDEN:(2 * kind + 1) * CMP_HIDDEN]
        bot = acc[:, (2 * kind + 1) * CMP_HIDDEN:(2 * kind + 2) * CMP_HIDDEN]
        bot_next = pltpu.roll(bot, n_half - 1, 0)
        pos_term = jnp.dot(pos_ref[kind], w1_ref[kind], preferred_element_type=F32)[0:1]
        hid = jax.nn.gelu(top + bot_next + pos_term + b1_ref[kind])
        outs.append(jnp.dot(hid.astype(BF16), w2_ref[kind], preferred_element_type=F32)
                    + b2_ref[kind])
    kc_ref[0, 0] = outs[0].astype(BF16)
    dk = NSA_HEAD_DIM
    vcT_ref[0, 0] = lax.dot_general(eye_ref[:dk, :dk], outs[1].astype(BF16), NT_DIMS,
                                    preferred_element_type=F32).astype(BF16)


def _compress(pc3, wc, pos, w1, b1, w2, b2, eye, first_col_block):
    bsz, seq, _ = pc3.shape
    n_grp, dk = NSA_KV_HEADS, NSA_HEAD_DIM
    n_half = seq // CMP_STRIDE
    const3 = lambda b, g: (0, 0, 0)
    return pl.pallas_call(
        _compress_kernel,
        grid=(bsz, n_grp),
        in_specs=[pl.BlockSpec((1, seq, LANES), lambda b, g: (b, 0, first_col_block + g)),
                  pl.BlockSpec(wc.shape, const3),
                  pl.BlockSpec(pos.shape, const3),
                  pl.BlockSpec(w1.shape, const3),
                  pl.BlockSpec(b1.shape, const3),
                  pl.BlockSpec(w2.shape, const3),
                  pl.BlockSpec(b2.shape, const3),
                  pl.BlockSpec(eye.shape, lambda b, g: (0, 0))],
        out_specs=[pl.BlockSpec((1, 1, n_half, dk), lambda b, g: (b, g, 0, 0)),
                   pl.BlockSpec((1, 1, dk, n_half), lambda b, g: (b, g, 0, 0))],
        out_shape=[jax.ShapeDtypeStruct((bsz, n_grp, n_half, dk), BF16),
                   jax.ShapeDtypeStruct((bsz, n_grp, dk, n_half), BF16)],
        compiler_params=_params(("parallel", "parallel")), name="nsa_compress",
    )(pc3, wc, pos, w1, b1, w2, b2, eye)


def _nsa_attn_kernel(q_ref, ks_ref, kw_ref, v_ref, kc_ref, vcT_ref, ov_ref, gl_ref, eye_ref,
                     aux_ref, o_ref, *scratch):
    tq = NSA_TQ
    first_tile = pl.program_id(2) * NSA_TILES_PER_STEP

    def one_tile(i, carry):
        rows_i = pl.ds(pl.multiple_of(i * tq, tq), tq)
        _nsa_query_tile(first_tile + i, q_ref.at[:, rows_i, :], ks_ref, kw_ref, v_ref, kc_ref,
                        vcT_ref, ov_ref, gl_ref.at[:, rows_i, :], eye_ref, aux_ref,
                        o_ref.at[:, rows_i, :], *scratch)
        return carry

    lax.fori_loop(0, NSA_TILES_PER_STEP, one_tile, 0)


def _nsa_query_tile(qt, q_ref, ks_ref, kw_ref, v_ref, kc_ref, vcT_ref, ov_ref, gl_ref, eye_ref,
                    aux_ref, o_ref,
                    ksa_ref, kwa_ref, vsT_ref, vwT_ref, sel_ref, s_ref, m_ref, acc_ref,
                    sw_ref, mw_ref, accw_ref, part_ref, gate_ref):
    tq, tk = NSA_TQ, NSA_TK
    rows = NSA_REP * tq
    dk = NSA_HEAD_DIM
    grp = pl.program_id(1)
    qs = qt * tq
    seq = ks_ref.shape[1]
    eye = eye_ref[...]
    tq_per_tk = tk // tq

    @pl.when(qt == 0)
    def _():
        lane = lax.broadcasted_iota(jnp.int32, (tk, LANES), 1)
        lane_flag = jnp.where(lane == dk, 1.0, 0.0).astype(BF16)
        ones_rows = jnp.ones((NSA_AUX, tk), BF16)

        def build(j, carry):
            r0 = pl.multiple_of(j * tk, tk)
            ksa_ref[pl.ds(r0, tk), :] = ks_ref[0, pl.ds(r0, tk), :] + aux_ref[...]
            kwa_ref[pl.ds(r0, tk), :] = kw_ref[0, pl.ds(r0, tk), :] + lane_flag
            vT = lax.dot_general(eye[:LANES, :LANES], v_ref[0, pl.ds(r0, tk), :], NT_DIMS,
                                 preferred_element_type=F32).astype(BF16)
            vsT_ref[j, 0:dk, :] = vT[:dk]
            vsT_ref[j, dk:, :] = ones_rows
            for i in range(tq_per_tk):
                vwT_ref[j * tq_per_tk + i, 0:dk, :] = vT[dk:, i * tq:(i + 1) * tq]
                vwT_ref[j * tq_per_tk + i, dk:, :] = ones_rows[:, :tq]
            return carry

        lax.fori_loop(0, seq // tk, build, 0)

    q4 = lax.dot_general(eye[:NSA_REP * dk, :NSA_REP * dk], q_ref[0], NT_DIMS,
                         preferred_element_type=F32).astype(BF16)
    qT = jnp.concatenate([q4[r * dk:(r + 1) * dk, :] for r in range(NSA_REP)], axis=1)
    t_q = qs + lax.broadcasted_iota(jnp.int32, (1, tq), 1)
    t_row = jnp.concatenate([t_q] * NSA_REP, axis=1)
    zpad = jnp.zeros((LANES - dk - NSA_AUX, rows), BF16)

    def per_head(a):
        return jnp.concatenate([a] * NSA_REP, axis=1)

    gate_ref[...] = _dot_exact_lhs(eye[:LANES, :LANES], jax.nn.sigmoid(gl_ref[0]), NT_DIMS)
    gates = gate_ref[pl.ds(pl.multiple_of(grp * NSA_AUX, NSA_AUX), NSA_AUX), :]

    n_cmp = kc_ref.shape[2]
    s_c = jnp.dot(kc_ref[0, 0], qT, preferred_element_type=F32)
    c_end = lax.broadcasted_iota(jnp.int32, (n_cmp, 1), 0) * CMP_STRIDE + (CMP_BLK - 1)
    s_c = s_c + per_head(jnp.where(c_end <= t_q, 0.0, MASK_VALUE))
    mx_c = jnp.max(s_c, axis=0, keepdims=True)
    p_c = jnp.exp2(s_c - mx_c)
    l_c = jnp.sum(p_c, axis=0, keepdims=True)
    p_c = p_c * jnp.where(t_row >= CMP_BLK - 1, 1.0 / l_c, 0.0)
    o_c = jnp.dot(vcT_ref[0, 0], p_c.astype(BF16), preferred_element_type=F32)
    for r in range(NSA_REP):
        part_ref[r * dk:(r + 1) * dk, :] = gates[3 * r:3 * r + 1] * o_c[:, r * tq:(r + 1) * tq]

    p_sum = p_c[:, 0:tq]
    for r in range(1, NSA_REP):
        p_sum = p_sum + p_c[:, r * tq:(r + 1) * tq]
    ov = ov_ref[...]
    imp = _dot_exact_lhs(ov, p_sum)
    n_blk = ov.shape[0]
    j_blk = lax.broadcasted_iota(jnp.int32, (n_blk, 1), 0)
    cur = t_q // SEL_BLK
    forced = (j_blk == 0) | (j_blk == cur) | (j_blk == cur - 1)
    valid = j_blk * SEL_BLK <= t_q
    n_rank = N_SEL - 3
    cand = jnp.where(forced, -jnp.inf, jnp.where(valid, imp, -1.0))

    n_wt = WINDOW // tq + 1
    row0 = lax.broadcasted_iota(jnp.int32, (NSA_AUX, rows), 0) == 0
    a_k = lax.broadcasted_iota(jnp.int32, (tq, 1), 0)
    b_q = lax.broadcasted_iota(jnp.int32, (1, tq), 1)
    w_tiles = [jnp.maximum(qt - (n_wt - 1) + i, 0) for i in range(n_wt)]

    def win_scores(i):
        before_start = qt - (n_wt - 1) + i < 0
        flag = jnp.where(row0 & before_start, MASK_VALUE, 0.0).astype(BF16)
        rhs_w = jnp.concatenate([qT, flag, zpad], axis=0)
        k_tile = kwa_ref[pl.ds(pl.multiple_of(w_tiles[i] * tq, tq), tq), :]
        s = jnp.dot(k_tile, rhs_w, preferred_element_type=F32)
        if i == 0:
            s = s + per_head(jnp.where(a_k > b_q, 0.0, MASK_VALUE))
        if i == n_wt - 1:
            s = s + per_head(jnp.where(a_k <= b_q, 0.0, MASK_VALUE))
        sw_ref[i * tq:(i + 1) * tq, :] = s

    def win_max():
        mw_ref[...] = jnp.max(sw_ref[...], axis=0, keepdims=True)

    def win_pv(i):
        p = jnp.exp2(sw_ref[i * tq:(i + 1) * tq, :] - mw_ref[...]).astype(BF16)
        pv = jnp.dot(vwT_ref[w_tiles[i]], p, preferred_element_type=F32)
        if i == 0:
            accw_ref[...] = pv
        else:
            accw_ref[...] = accw_ref[...] + pv

    win_steps = ([functools.partial(win_scores, i) for i in range(n_wt)] + [win_max]
                 + [functools.partial(win_pv, i) for i in range(n_wt)])

    work = cand
    maxima = []
    for r in range(n_rank):
        mx = jnp.max(work, axis=0, keepdims=True)
        maxima.append(mx)
        work = jnp.where(work == mx, -jnp.inf, work)
        if r < len(win_steps):
            win_steps[r]()
    for step in win_steps[n_rank:]:
        step()

    ones_lhs = jnp.ones((SUBLANES, n_blk), BF16)

    def count(mask):
        return jnp.dot(ones_lhs, jnp.where(mask, 1.0, 0.0).astype(BF16),
                       preferred_element_type=F32)[0:1]

    thr = maxima[-1]
    for mx in reversed(maxima[:-1]):
        thr = jnp.where(count(cand >= mx) >= n_rank, mx, thr)
    ties = cand == thr
    below = (lax.broadcasted_iota(jnp.int32, (n_blk, n_blk), 1)
             < lax.broadcasted_iota(jnp.int32, (n_blk, n_blk), 0))
    tie_rank = jnp.dot(jnp.where(below, 1.0, 0.0).astype(BF16),
                       jnp.where(ties, 1.0, 0.0).astype(BF16), preferred_element_type=F32)
    picked = (cand > thr) | (ties & (tie_rank < n_rank - count(cand > thr)))
    sel_ref[0:n_blk, :] = jnp.where(forced | (picked & (cand >= 0.0)), 0.0, MASK_VALUE)
    sel_ref[n_blk:, :] = jnp.zeros((NSA_AUX, tq), F32)

    acc_w = accw_ref[...]
    o_w = acc_w[:dk] * (1.0 / acc_w[dk:dk + 1])
    for r in range(NSA_REP):
        part_ref[r * dk:(r + 1) * dk, :] = (part_ref[r * dk:(r + 1) * dk, :]
                                            + gates[3 * r + 2:3 * r + 3] * o_w[:, r * tq:(r + 1) * tq])

    blk_per_tile = tk // SEL_BLK
    n_full = qs // tk

    def scores(kt, slot, diagonal=False):
        sb = sel_ref[pl.ds(pl.multiple_of(kt * blk_per_tile, blk_per_tile), NSA_AUX), :]
        rhs = jnp.concatenate([qT, per_head(sb).astype(BF16), zpad], axis=0)
        k_tile = ksa_ref[pl.ds(pl.multiple_of(kt * tk, tk), tk), :]
        s = jnp.dot(k_tile, rhs, preferred_element_type=F32)
        if diagonal:
            kpos = kt * tk + lax.broadcasted_iota(jnp.int32, (tk, 1), 0)
            s = s + per_head(jnp.where(kpos <= t_q, 0.0, MASK_VALUE))
        s_ref[slot] = s

    def softmax_pv(kt, slot):
        m_i = m_ref[...]
        m_new = jnp.maximum(m_i, jnp.max(s_ref[slot], axis=0, keepdims=True))
        alpha = jnp.exp2(m_i - m_new)
        pv = None
        for c in range(tk // PV_CHUNK):
            keys = slice(c * PV_CHUNK, (c + 1) * PV_CHUNK)
            p = jnp.exp2(s_ref[slot, keys, :] - m_new).astype(BF16)
            term = jnp.dot(vsT_ref[kt, :, keys], p, preferred_element_type=F32)
            pv = term if pv is None else pv + term
        acc_ref[...] = alpha * acc_ref[...] + pv
        m_ref[...] = m_new

    m_ref[...] = jnp.full((1, rows), MASK_VALUE, F32)
    acc_ref[...] = jnp.zeros((dk + NSA_AUX, rows), F32)
    scores(n_full, 0, diagonal=True)

    n_pairs = n_full // 2

    def slot0_tile(i):
        return jnp.where(i == 0, n_full, 2 * i - 1)

    def pair_step(i, carry):
        scores(2 * i, 1)
        softmax_pv(slot0_tile(i), 0)
        scores(2 * i + 1, 0)
        softmax_pv(2 * i, 1)
        return carry

    def quad_step(i, carry):
        pair_step(2 * i, carry)
        return pair_step(2 * i + 1, carry)

    n_quads = n_pairs // 2
    lax.fori_loop(0, n_quads, quad_step, 0)
    lax.fori_loop(2 * n_quads, n_pairs, pair_step, 0)

    @pl.when(n_full % 2 == 1)
    def _():
        scores(2 * n_pairs, 1)
        softmax_pv(slot0_tile(n_pairs), 0)
        softmax_pv(2 * n_pairs, 1)

    @pl.when(n_full % 2 == 0)
    def _():
        softmax_pv(slot0_tile(n_pairs), 0)

    acc_s = acc_ref[...]
    o_s = acc_s[:dk] * (1.0 / acc_s[dk:dk + 1])
    gates = gate_ref[pl.ds(pl.multiple_of(grp * NSA_AUX, NSA_AUX), NSA_AUX), :]
    out_t = jnp.concatenate(
        [part_ref[r * dk:(r + 1) * dk, :] + gates[3 * r + 1:3 * r + 2] * o_s[:, r * tq:(r + 1) * tq]
         for r in range(NSA_REP)], axis=0).astype(BF16)
    o_ref[0] = lax.dot_general(eye[:tq, :tq], out_t, NT_DIMS,
                               preferred_element_type=F32).astype(o_ref.dtype)


def _nsa_attn(p_att3, aux3, small_block, kc, vcT, overlap_t, eye, aux_s):
    bsz, seq, _ = p_att3.shape
    n_grp, dk = NSA_KV_HEADS, NSA_HEAD_DIM
    tq, tk = NSA_TQ, NSA_TK
    rows = NSA_REP * tq
    q_w = NSA_REP * dk
    n_cmp = kc.shape[2]
    n_blk = seq // SEL_BLK
    v_rows = dk + NSA_AUX
    n_wt = WINDOW // tq + 1
    first = NSA_HEADS * dk // LANES
    kv_spec = lambda off: pl.BlockSpec((1, seq, LANES), lambda b, g, t: (b, 0, first + off + g))
    step_rows = NSA_TILES_PER_STEP * tq
    return pl.pallas_call(
        _nsa_attn_kernel,
        grid=(bsz, n_grp, seq // step_rows),
        in_specs=[
            pl.BlockSpec((1, step_rows, q_w), lambda b, g, t: (b, t, g)),
            kv_spec(0), kv_spec(n_grp), kv_spec(2 * n_grp),
            pl.BlockSpec((1, 1, n_cmp, dk), lambda b, g, t: (b, g, 0, 0)),
            pl.BlockSpec((1, 1, dk, n_cmp), lambda b, g, t: (b, g, 0, 0)),
            pl.BlockSpec((n_blk, n_cmp), lambda b, g, t: (0, 0)),
            pl.BlockSpec((1, step_rows, LANES), lambda b, g, t: (b, t, small_block)),
            pl.BlockSpec(eye.shape, lambda b, g, t: (0, 0)),
            pl.BlockSpec((tk, LANES), lambda b, g, t: (0, 0)),
        ],
        out_specs=pl.BlockSpec((1, step_rows, q_w), lambda b, g, t: (b, t, g)),
        out_shape=jax.ShapeDtypeStruct((bsz, seq, NSA_HEADS * dk), BF16),
        scratch_shapes=[pltpu.VMEM((seq, LANES), BF16),
                        pltpu.VMEM((seq, LANES), BF16),
                        pltpu.VMEM((seq // tk, v_rows, tk), BF16),
                        pltpu.VMEM((seq // tq, v_rows, tq), BF16),
                        pltpu.VMEM((n_blk + NSA_AUX, tq), F32),
                        pltpu.VMEM((2, tk, rows), F32),
                        pltpu.VMEM((1, rows), F32),
                        pltpu.VMEM((v_rows, rows), F32),
                        pltpu.VMEM((n_wt * tq, rows), F32),
                        pltpu.VMEM((1, rows), F32),
                        pltpu.VMEM((v_rows, rows), F32),
                        pltpu.VMEM((q_w, tq), F32),
                        pltpu.VMEM((LANES, tq), F32)],
        compiler_params=_params(("parallel", "parallel", "arbitrary")), name="nsa_attention",
    )(p_att3, p_att3, p_att3, p_att3, kc, vcT, overlap_t, aux3, eye, aux_s)


def _bmm(a, b, dims):
    a = a.astype(BF16)
    b = b.astype(BF16)
    return jnp.stack([lax.dot_general(a[n], b[n], dims, preferred_element_type=F32)
                      for n in range(a.shape[0])])


def _gdn_kernel(y_ref, z_ref, small_ref, alog_ref, dtb_ref, nw_ref, o_ref, state_ref):
    bsz = y_ref.shape[0]
    cs = GDN_CHUNK
    hd = GDN_HEAD_DIM
    nh = GDN_HEADS
    c = pl.program_id(0)

    @pl.when(c == 0)
    def _():
        state_ref[...] = jnp.zeros_like(state_ref)

    y = y_ref[...]

    small = small_ref[...]
    sp_in = small + dtb_ref[...]
    softplus = jnp.maximum(sp_in, 0.0) + jnp.log1p(jnp.exp(-jnp.abs(sp_in)))
    g_all = -jnp.exp(alog_ref[...]) * softplus
    beta_all = jax.nn.sigmoid(small)
    ii = lax.broadcasted_iota(jnp.int32, (cs, cs), 0)
    jj = lax.broadcasted_iota(jnp.int32, (cs, cs), 1)
    incl = ii >= jj
    strict = ii > jj
    tril = jnp.where(incl, 1.0, 0.0).astype(BF16)
    e128 = jnp.where(lax.broadcasted_iota(jnp.int32, (LANES, LANES), 0)
                     == lax.broadcasted_iota(jnp.int32, (LANES, LANES), 1), 1.0, 0.0).astype(BF16)
    gc_all = [_dot_exact_lhs(tril, g_all[b]) for b in range(bsz)]
    gcT_all = [_dot_exact_lhs(e128, gc_all[b], NT_DIMS) for b in range(bsz)]

    chains = [(b, h) for b in range(bsz) for h in range(nh)]

    def heads(a, base):
        return jnp.stack([a[b][:, base + h * hd:base + (h + 1) * hd] for b, h in chains])

    def lane_bcast(cols, base):
        return jnp.stack([jnp.broadcast_to(cols[b][:, base + h:base + h + 1], (cs, hd))
                          for b, h in chains])

    n_qk = nh * hd
    q = heads(y, 0)
    k = heads(y, n_qk)
    v = heads(y, 2 * n_qk)
    q = q * lax.rsqrt(jnp.sum(q * q, axis=-1, keepdims=True) + NORM_EPS) * (hd ** -0.5)
    k = k * lax.rsqrt(jnp.sum(k * k, axis=-1, keepdims=True) + NORM_EPS)
    gc = lane_bcast(gc_all, SMALL_A)
    beta = lane_bcast([beta_all[b] for b in range(bsz)], SMALL_B)
    gc_row = jnp.stack([gcT_all[b][SMALL_A + h:SMALL_A + h + 1, :] for b, h in chains])

    decay = jnp.exp(jnp.where(incl, gc[:, :, :cs] - gc_row, MASK_VALUE))
    kk = _bmm(k, k, NT_DIMS)
    m_full = jnp.where(strict, -(beta[:, :, :cs] * kk * decay), 0.0)
    blk_diff = ii ^ jj
    inv_m1 = jnp.where(blk_diff < 2, m_full, 0.0)
    size = 2
    while size < cs:
        m_off = jnp.where((blk_diff >= size) & (blk_diff < 2 * size), m_full, 0.0)
        t_mat = m_off + _bmm(inv_m1, m_off, NN_DIMS)
        inv_m1 = inv_m1 + t_mat + _bmm(t_mat, inv_m1, NN_DIMS)
        size *= 2
    e_gc = jnp.exp(gc)
    rhs = jnp.concatenate([v * beta, k * (beta * e_gc)], axis=2)
    sol = rhs + _bmm(inv_m1, rhs, NN_DIMS)
    u, w = sol[:, :, :hd], sol[:, :, hd:]
    qk = _bmm(q, k, NT_DIMS) * decay
    q_dec = q * e_gc
    gc_last = gc[:, cs - 1:cs, :]
    k_dec = k * jnp.exp(gc_last - gc)
    state = state_ref[...]
    v_new = u - _bmm(w, state, NN_DIMS)
    o = _bmm(q_dec, state, NN_DIMS) + _bmm(qk, v_new, NN_DIMS)
    state_ref[...] = state * jnp.exp(gc_last) + _bmm(k_dec, v_new, TN_DIMS)

    o = o * lax.rsqrt(jnp.mean(o * o, axis=-1, keepdims=True) + NORM_EPS) * nw_ref[...]
    z = heads(z_ref[...].astype(F32), 0)
    o = o * (z * jax.nn.sigmoid(z))
    for b in range(bsz):
        o_ref[b] = jnp.concatenate([o[b * nh + h] for h in range(nh)], axis=1).astype(o_ref.dtype)


def _gated_deltanet(qkv3, zm3, aux3, small_block, alog_pad, dtb_pad, norm_w):
    bsz, seq, qkv_cols = qkv3.shape
    cs = GDN_CHUNK
    z_cols = GDN_HEADS * GDN_HEAD_DIM
    out = pl.pallas_call(
        _gdn_kernel,
        grid=(seq // cs,),
        in_specs=[
            pl.BlockSpec((bsz, cs, qkv_cols), lambda c: (0, c, 0)),
            pl.BlockSpec((bsz, cs, z_cols), lambda c: (0, c, 0)),
            pl.BlockSpec((bsz, cs, LANES), lambda c: (0, c, small_block)),
            pl.BlockSpec((1, LANES), lambda c: (0, 0)),
            pl.BlockSpec((1, LANES), lambda c: (0, 0)),
            pl.BlockSpec((1, GDN_HEAD_DIM), lambda c: (0, 0)),
        ],
        out_specs=pl.BlockSpec((bsz, cs, z_cols), lambda c: (0, c, 0)),
        out_shape=jax.ShapeDtypeStruct((bsz, seq, z_cols), BF16),
        scratch_shapes=[pltpu.VMEM((bsz * GDN_HEADS, GDN_HEAD_DIM, GDN_HEAD_DIM), F32)],
        compiler_params=_params(("arbitrary",)), name="gated_deltanet",
    )(qkv3, zm3, aux3, alog_pad, dtb_pad, norm_w.reshape(1, GDN_HEAD_DIM))
    return out.reshape(bsz * seq, z_cols)


def _merge_kernel(ya_ref, yb_ref, ma_ref, mb_ref, x_ref, mod_ref, wa_ref, wb_ref, wo_ref,
                  nf_ref, x1_ref, h2_ref):
    pa = jnp.dot(ya_ref[...], wa_ref[...], preferred_element_type=F32)
    pb = jnp.dot(yb_ref[...], wb_ref[...], preferred_element_type=F32)
    mixed = (jax.nn.sigmoid(ma_ref[...].astype(F32)) * pa
             + jax.nn.sigmoid(mb_ref[...].astype(F32)) * pb)
    mod = mod_ref[0]
    x1 = x_ref[...] + mod[GT_M:GT_M + 1, :] * jnp.dot(
        mixed.astype(BF16), wo_ref[...], preferred_element_type=F32)
    x1_ref[...] = x1
    h2_ref[...] = _modulated_norm(x1, nf_ref[...], mod, SC_F, SH_F).astype(BF16)


def _merge_project(ya, yb, pc, merge_blk, x2, mod, w_pa, w_pb, w_o, norm_ffn, seq):
    t, d = x2.shape
    tiles_per_batch = seq // ROW_TILE
    row = lambda i: (i, 0)
    const = lambda i: (0, 0)
    return pl.pallas_call(
        _merge_kernel,
        grid=(t // ROW_TILE,),
        in_specs=[pl.BlockSpec((ROW_TILE, d), row),
                  pl.BlockSpec((ROW_TILE, d), row),
                  pl.BlockSpec((ROW_TILE, d), lambda i: (i, merge_blk)),
                  pl.BlockSpec((ROW_TILE, d), lambda i: (i, merge_blk + 1)),
                  pl.BlockSpec((ROW_TILE, d), row),
                  pl.BlockSpec((1, 6, d), lambda i: (i // tiles_per_batch, 0, 0)),
                  pl.BlockSpec((d, d), const),
                  pl.BlockSpec((d, d), const),
                  pl.BlockSpec((d, d), const),
                  pl.BlockSpec((1, d), const)],
        out_specs=[pl.BlockSpec((ROW_TILE, d), row), pl.BlockSpec((ROW_TILE, d), row)],
        out_shape=[jax.ShapeDtypeStruct((t, d), F32), jax.ShapeDtypeStruct((t, d), BF16)],
        compiler_params=_params(("parallel",)), name="merge_project",
    )(ya, yb, pc, pc, x2, mod, w_pa, w_pb, w_o, norm_ffn.reshape(1, d))


def _ffn_kernel(h_ref, halo_ref, res_ref, mod_ref, wg_ref, wv_ref, cwg_ref, cwv_ref,
                cbg_ref, cbv_ref, wd_ref, nfin_ref, o_ref, ug_ref, uv_ref,
                *, tiles_per_batch, last):
    i = pl.program_id(0)
    tm = h_ref.shape[0]
    first_of_batch = (i % tiles_per_batch) == 0
    lhs = jnp.concatenate([halo_ref[...], h_ref[...]], axis=0)

    def conv_half(w_ref, cw_ref, cb_ref, u_ref):
        u_all = jnp.dot(lhs, w_ref[...], preferred_element_type=F32)
        u_ref[SUBLANES:, :] = u_all[2 * SUBLANES:]
        u_ref[0:SUBLANES, :] = jnp.where(first_of_batch, 0.0, u_all[SUBLANES:2 * SUBLANES])
        cw = cw_ref[...]
        out = cb_ref[...] + cw[FFN_CONV - 1:FFN_CONV, :] * u_ref[SUBLANES:, :]
        for j in range(FFN_CONV - 1):
            off = SUBLANES - (FFN_CONV - 1) + j
            out = out + cw[j:j + 1, :] * u_ref[off:off + tm, :]
        return out

    gate = conv_half(wg_ref, cwg_ref, cbg_ref, ug_ref)
    val = conv_half(wv_ref, cwv_ref, cbv_ref, uv_ref)
    act = gate * jax.nn.sigmoid(gate) * val
    down = jnp.dot(act.astype(BF16), wd_ref[...], preferred_element_type=F32)
    y = res_ref[...] + mod_ref[0][GT_F:GT_F + 1, :] * down
    if last:
        ms = jnp.mean(y * y, axis=-1, keepdims=True)
        y = y * lax.rsqrt(ms + NORM_EPS) * nfin_ref[...]
    o_ref[...] = y


def _conv_ffn(h2, x1, mod, w_up, conv_w, conv_b, w_down, norm_final, seq):
    t, d = x1.shape
    tm, tf = ROW_TILE, FFN_TF
    n_f = FFN_DIM // tf
    tiles_per_batch = seq // tm
    halo_per_tile = tm // SUBLANES
    row = lambda i: (i, 0)
    res = x1
    for f in range(n_f):
        col = lambda i, c=f: (0, c)
        col_val = lambda i, c=n_f + f: (0, c)
        res = pl.pallas_call(
            functools.partial(_ffn_kernel, tiles_per_batch=tiles_per_batch, last=f == n_f - 1),
            grid=(t // tm,),
            in_specs=[pl.BlockSpec((tm, d), row),
                      pl.BlockSpec((SUBLANES * 2, d),
                                   lambda i: (jnp.maximum(i * (halo_per_tile // 2) - 1, 0), 0)),
                      pl.BlockSpec((tm, d), row),
                      pl.BlockSpec((1, 6, d), lambda i: (i // tiles_per_batch, 0, 0)),
                      pl.BlockSpec((d, tf), col),
                      pl.BlockSpec((d, tf), col_val),
                      pl.BlockSpec((FFN_CONV, tf), col),
                      pl.BlockSpec((FFN_CONV, tf), col_val),
                      pl.BlockSpec((1, tf), col),
                      pl.BlockSpec((1, tf), col_val),
                      pl.BlockSpec((tf, d), lambda i, c=f: (c, 0)),
                      pl.BlockSpec((1, d), lambda i: (0, 0))],
            out_specs=pl.BlockSpec((tm, d), row),
            out_shape=jax.ShapeDtypeStruct((t, d), F32),
            scratch_shapes=[pltpu.VMEM((tm + SUBLANES, tf), F32),
                            pltpu.VMEM((tm + SUBLANES, tf), F32)],
            compiler_params=_params(("parallel",)), name="conv_ffn",
        )(h2, h2, res, mod, w_up, w_up, conv_w, conv_w, conv_b, conv_b, w_down,
          norm_final.reshape(1, d))
    return res


def _overlap_t(seq):
    n_cmp = (seq - CMP_BLK) // CMP_STRIDE + 1
    n_blk = seq // SEL_BLK
    c_start = np.arange(n_cmp) * CMP_STRIDE
    s_start = np.arange(n_blk) * SEL_BLK
    ov = np.clip(np.minimum(c_start[:, None] + CMP_BLK, s_start[None, :] + SEL_BLK)
                 - np.maximum(c_start[:, None], s_start[None, :]), 0, None) / CMP_BLK
    n_half = seq // CMP_STRIDE
    out = np.zeros((n_blk, n_half), np.float32)
    out[:, :n_cmp] = ov.T
    return jnp.asarray(out, BF16)


def kernel(x, c, w_ada, b_ada, norm_mix, w_in, nsa_pos_k, nsa_pos_v, nsa_ck_w1, nsa_ck_b1, nsa_ck_w2, nsa_ck_b2, nsa_cv_w1, nsa_cv_b1, nsa_cv_w2, nsa_cv_b2, gdn_conv, gdn_a_log, gdn_dt_bias, gdn_norm, w_proj_nsa, w_proj_gdn, w_out, norm_ffn, ffn_up, ffn_conv, ffn_conv_b, ffn_down, norm_final):
    bsz, seq, d = x.shape
    depth = w_ada.shape[0]
    n_grp, dk = NSA_KV_HEADS, NSA_HEAD_DIM
    q_cols = NSA_HEADS * dk
    gqkv_cols = 3 * GDN_HEADS * GDN_HEAD_DIM
    z_cols = GDN_HEADS * GDN_HEAD_DIM
    o_kv = q_cols
    o_gl = o_kv + 6 * n_grp * dk
    o_gq = o_gl + 3 * NSA_HEADS
    o_a = o_gq + gqkv_cols
    o_z = o_a + 2 * GDN_HEADS
    overlap_t = _overlap_t(seq)
    x2 = x.reshape(bsz * seq, d)
    assert depth == 1, "kernel supports the problem's DEPTH == 1"
    for l in range(depth):
        mod = _adaln_mod(c, w_ada[l], b_ada[l])
        w = w_in[l]
        kv_w = w[:, o_kv:o_gl].reshape(d, 6, n_grp, dk)
        zero = jnp.zeros((d, n_grp, dk), F32)
        pair = lambda a, b: jnp.concatenate([a, b], axis=2).reshape(d, n_grp * 2 * dk)
        w_att = jnp.concatenate([w[:, :o_kv] * (dk ** -0.5 * LOG2_E), pair(kv_w[:, 2], zero),
                                 pair(kv_w[:, 4], zero), pair(kv_w[:, 3], kv_w[:, 5])],
                                axis=1).astype(BF16)
        gate_w = jnp.pad(w[:, o_gl:o_gq].reshape(d, n_grp, 3 * NSA_REP),
                         ((0, 0), (0, 0), (0, NSA_AUX - 3 * NSA_REP))).reshape(d, SMALL_A)
        w_small = jnp.concatenate([gate_w, w[:, o_a:o_z],
                                   jnp.zeros((d, LANES - SMALL_B - GDN_HEADS), F32)],
                                  axis=1).astype(BF16)
        w_aux = jnp.concatenate([pair(kv_w[:, 0], kv_w[:, 1]).astype(BF16), w_small], axis=1)
        small_block = n_grp
        w_zm = w[:, o_z:].astype(BF16)

        p_att, h = _norm_mod_matmul(x2, norm_mix[l], mod, w_att, 1280, BF16, seq)
        qkv = _matmul_conv_silu(h, w[:, o_gq:o_a].astype(BF16), gdn_conv[l], 1536, seq)
        zm = _matmul(h, w_zm, 1536, BF16)
        aux = _matmul(h, w_aux, w_aux.shape[1], F32)
        aux3 = aux.reshape(bsz, seq, aux.shape[1])

        w1k = nsa_ck_w1[l].reshape(2, CMP_STRIDE, dk, CMP_HIDDEN)
        w1v = nsa_cv_w1[l].reshape(2, CMP_STRIDE, dk, CMP_HIDDEN)
        zw = jnp.zeros((CMP_STRIDE, dk, 2 * CMP_HIDDEN), F32)
        wc = jnp.concatenate(
            [jnp.concatenate([w1k[0], w1k[1], zw], axis=2),
             jnp.concatenate([zw, w1v[0], w1v[1]], axis=2)], axis=1).astype(BF16)
        pos = jnp.stack([nsa_pos_k[l], nsa_pos_v[l]]).reshape(2, 1, CMP_BLK * dk)
        pos = jnp.broadcast_to(pos, (2, 2 * SUBLANES, CMP_BLK * dk)).astype(BF16)
        cw1 = jnp.stack([nsa_ck_w1[l], nsa_cv_w1[l]]).astype(BF16)
        cb1 = jnp.stack([nsa_ck_b1[l], nsa_cv_b1[l]]).reshape(2, 1, CMP_HIDDEN)
        cw2 = jnp.stack([nsa_ck_w2[l], nsa_cv_w2[l]]).astype(BF16)
        cb2 = jnp.stack([nsa_ck_b2[l], nsa_cv_b2[l]]).reshape(2, 1, dk)
        eye = jnp.eye(max(NSA_REP * dk, NSA_TQ), dtype=BF16)
        kc, vcT = _compress(aux3, wc, pos, cw1, cb1, cw2, cb2, eye, 0)

        blk_in_tile = np.arange(NSA_TK) // SEL_BLK
        aux_s = np.zeros((NSA_TK, LANES), np.float32)
        aux_s[np.arange(NSA_TK), dk + blk_in_tile] = 1.0
        y_a = _nsa_attn(p_att.reshape(bsz, seq, p_att.shape[1]), aux3, small_block, kc, vcT,
                        overlap_t, eye, jnp.asarray(aux_s, BF16)).reshape(bsz * seq, q_cols)

        alog_pad = jnp.zeros((1, LANES), F32).at[0, SMALL_A:SMALL_A + GDN_HEADS].set(gdn_a_log[l])
        dtb_pad = jnp.zeros((1, LANES), F32).at[0, SMALL_A:SMALL_A + GDN_HEADS].set(gdn_dt_bias[l])
        y_b = _gated_deltanet(qkv.reshape(bsz, seq, gqkv_cols), zm.reshape(bsz, seq, zm.shape[1]),
                              aux3, small_block, alog_pad, dtb_pad, gdn_norm[l])

        x1, h2 = _merge_project(y_a, y_b, zm, z_cols // d, x2, mod, w_proj_nsa[l].astype(BF16),
                                w_proj_gdn[l].astype(BF16), w_out[l].astype(BF16),
                                norm_ffn[l], seq)
        out = _conv_ffn(h2, x1, mod, ffn_up[l].astype(BF16), ffn_conv[l],
                        ffn_conv_b[l].reshape(1, 2 * FFN_DIM), ffn_down[l].astype(BF16),
                        norm_final, seq)
    return out.reshape(bsz, seq, d)
```

```python
import functools

import numpy as np
import jax
import jax.numpy as jnp
from jax import lax
from jax.experimental import pallas as pl
from jax.experimental.pallas import tpu as pltpu

F32 = jnp.float32
BF16 = jnp.bfloat16
HIGHEST = lax.Precision.HIGHEST

NSA_HEAD_DIM = 64
NSA_HEADS = 16
NSA_KV_HEADS = 4
NSA_REP = NSA_HEADS // NSA_KV_HEADS
CMP_BLK = 32
CMP_STRIDE = 16
CMP_HIDDEN = 256
SEL_BLK = 64
N_SEL = 16
WINDOW = 512
GDN_HEAD_DIM = 128
GDN_HEADS = 8
GDN_CONV = 4
GDN_CHUNK = 64
FFN_DIM = 2816
FFN_CONV = 3
NORM_EPS = 1e-6
MASK_VALUE = -1e30

LANES = 128
SUBLANES = 8
VMEM_LIMIT = 56 * 1024 * 1024

NSA_TQ = 256
NSA_TILES_PER_STEP = 4
NSA_TK = 512
PV_CHUNK = 256
NSA_AUX = 16
LOG2_E = 1.4426950408889634
PROJ_TILE = 1024
ROW_TILE = 512
FFN_TF = 1408

SH_M, SC_M, GT_M, SH_F, SC_F, GT_F = range(6)
SMALL_A = NSA_KV_HEADS * NSA_AUX
SMALL_B = SMALL_A + GDN_HEADS


def _params(sem):
    return pltpu.CompilerParams(dimension_semantics=sem, vmem_limit_bytes=VMEM_LIMIT)


def _split3(a):
    hi = a.astype(BF16)
    r1 = a - hi.astype(F32)
    mid = r1.astype(BF16)
    lo = (r1 - mid.astype(F32)).astype(BF16)
    return hi, mid, lo


def _dot_exact_lhs(a_bf16, b, dims=None):
    if dims is None:
        dims = (((a_bf16.ndim - 1,), (0,)), ((), ()))
    out = None
    for part in _split3(b):
        term = lax.dot_general(a_bf16, part, dimension_numbers=dims, preferred_element_type=F32)
        out = term if out is None else out + term
    return out


NT_DIMS = (((1,), (1,)), ((), ()))
TN_DIMS = (((0,), (0,)), ((), ()))
NN_DIMS = (((1,), (0,)), ((), ()))


def _mod_kernel(c_ref, w_ref, b_ref, o_ref):
    o_ref[...] = jnp.dot(c_ref[...], w_ref[...], preferred_element_type=F32,
                         precision=HIGHEST) + b_ref[...]


def _adaln_mod(c, w_ada, b_ada):
    bsz, d = c.shape
    n = w_ada.shape[1]
    c8 = jnp.zeros((SUBLANES, d), F32).at[:bsz].set(c)
    out = pl.pallas_call(
        _mod_kernel,
        grid=(n // d,),
        in_specs=[pl.BlockSpec((SUBLANES, d), lambda j: (0, 0)),
                  pl.BlockSpec((d, d), lambda j: (0, j)),
                  pl.BlockSpec((1, d), lambda j: (0, j))],
        out_specs=pl.BlockSpec((SUBLANES, d), lambda j: (0, j)),
        out_shape=jax.ShapeDtypeStruct((SUBLANES, n), F32),
        compiler_params=_params(("arbitrary",)), name="adaln_mod",
    )(c8, w_ada, b_ada.reshape(1, n))
    return out[:bsz].reshape(bsz, n // d, d)


def _modulated_norm(x, norm_w, mod, sc_row, sh_row):
    ms = jnp.mean(x * x, axis=-1, keepdims=True)
    y = x * lax.rsqrt(ms + NORM_EPS) * norm_w
    return y * (1.0 + mod[sc_row:sc_row + 1, :]) + mod[sh_row:sh_row + 1, :]


def _nmm_kernel(x_ref, nw_ref, mod_ref, w_ref, o_ref, h_ref):
    @pl.when(pl.program_id(1) == 0)
    def _():
        h = _modulated_norm(x_ref[...], nw_ref[...], mod_ref[0], SC_M, SH_M)
        h_ref[...] = h.astype(BF16)

    o_ref[...] = jnp.dot(h_ref[...], w_ref[...],
                         preferred_element_type=F32).astype(o_ref.dtype)


def _norm_mod_matmul(x2, norm_w, mod, w, tn, out_dtype, seq):
    t, d = x2.shape
    n = w.shape[1]
    tiles_per_batch = seq // PROJ_TILE
    return pl.pallas_call(
        _nmm_kernel,
        grid=(t // PROJ_TILE, n // tn),
        in_specs=[pl.BlockSpec((PROJ_TILE, d), lambda i, j: (i, 0)),
                  pl.BlockSpec((1, d), lambda i, j: (0, 0)),
                  pl.BlockSpec((1, 6, d), lambda i, j: (i // tiles_per_batch, 0, 0)),
                  pl.BlockSpec((d, tn), lambda i, j: (0, j))],
        out_specs=[pl.BlockSpec((PROJ_TILE, tn), lambda i, j: (i, j)),
                   pl.BlockSpec((PROJ_TILE, d), lambda i, j: (i, 0))],
        out_shape=[jax.ShapeDtypeStruct((t, n), out_dtype), jax.ShapeDtypeStruct((t, d), BF16)],
        compiler_params=_params(("parallel", "arbitrary")), name="norm_mod_proj",
    )(x2, norm_w.reshape(1, d), mod, w)


def _mm_kernel(h_ref, w_ref, o_ref):
    o_ref[...] = jnp.dot(h_ref[...], w_ref[...], preferred_element_type=F32).astype(o_ref.dtype)


def _matmul(h, w, tn, out_dtype):
    t, d = h.shape
    n = w.shape[1]
    return pl.pallas_call(
        _mm_kernel,
        grid=(n // tn, t // PROJ_TILE),
        in_specs=[pl.BlockSpec((PROJ_TILE, d), lambda j, i: (i, 0)),
                  pl.BlockSpec((d, tn), lambda j, i: (0, j))],
        out_specs=pl.BlockSpec((PROJ_TILE, tn), lambda j, i: (i, j)),
        out_shape=jax.ShapeDtypeStruct((t, n), out_dtype),
        compiler_params=_params(("parallel", "parallel")), name="proj",
    )(h, w)


def _mm_conv_kernel(h_ref, w_ref, cw_ref, o_ref, u_ref, *, tiles_per_batch):
    tm = h_ref.shape[0]
    width = cw_ref.shape[0]

    @pl.when(pl.program_id(1) % tiles_per_batch == 0)
    def _():
        u_ref[0:SUBLANES, :] = jnp.zeros((SUBLANES, u_ref.shape[1]), F32)

    cw = cw_ref[...]
    n_sub = 4
    sub = tm // n_sub

    def matmul(s):
        u_ref[SUBLANES + s * sub:SUBLANES + (s + 1) * sub, :] = jnp.dot(
            h_ref[s * sub:(s + 1) * sub, :], w_ref[...], preferred_element_type=F32)

    def conv(s):
        base = SUBLANES + s * sub
        y = cw[width - 1:width, :] * u_ref[base:base + sub, :]
        for i in range(width - 1):
            off = base - (width - 1) + i
            y = y + cw[i:i + 1, :] * u_ref[off:off + sub, :]
        o_ref[s * sub:(s + 1) * sub, :] = y * jax.nn.sigmoid(y)

    matmul(0)
    for s in range(1, n_sub):
        matmul(s)
        conv(s - 1)
    conv(n_sub - 1)
    u_ref[0:SUBLANES, :] = u_ref[tm:tm + SUBLANES, :]


def _matmul_conv_silu(h, w, conv_w, tn, seq):
    t, d = h.shape
    n = w.shape[1]
    return pl.pallas_call(
        functools.partial(_mm_conv_kernel, tiles_per_batch=seq // PROJ_TILE),
        grid=(n // tn, t // PROJ_TILE),
        in_specs=[pl.BlockSpec((PROJ_TILE, d), lambda j, i: (i, 0)),
                  pl.BlockSpec((d, tn), lambda j, i: (0, j)),
                  pl.BlockSpec((conv_w.shape[0], tn), lambda j, i: (0, j))],
        out_specs=pl.BlockSpec((PROJ_TILE, tn), lambda j, i: (i, j)),
        out_shape=jax.ShapeDtypeStruct((t, n), F32),
        scratch_shapes=[pltpu.VMEM((PROJ_TILE + SUBLANES, tn), F32)],
        compiler_params=_params(("parallel", "arbitrary")), name="proj_conv_silu",
    )(h, w, conv_w)


def _compress_kernel(t_ref, wc_ref, pos_ref, w1_ref, b1_ref, w2_ref, b2_ref, eye_ref,
                     kc_ref, vcT_ref):
    n_half = t_ref.shape[1] // CMP_STRIDE
    acc = None
    for l in range(CMP_STRIDE):
        tok = t_ref[0, pl.ds(l, n_half, stride=CMP_STRIDE), :].astype(BF16)
        term = jnp.dot(tok, wc_ref[l], preferred_element_type=F32)
        acc = term if acc is None else acc + term
    outs = []
    for kind in range(2):
        top = acc[:, 2 * kind * CMP_HIDDEN:(2 * kind + 1) * CMP_HIDDEN]
        bot = acc[:, (2 * kind + 1) * CMP_HIDDEN:(2 * kind + 2) * CMP_HIDDEN]
        bot_next = pltpu.roll(bot, n_half - 1, 0)
        pos_term = jnp.dot(pos_ref[kind], w1_ref[kind], preferred_element_type=F32)[0:1]
        hid = jax.nn.gelu(top + bot_next + pos_term + b1_ref[kind])
        outs.append(jnp.dot(hid.astype(BF16), w2_ref[kind], preferred_element_type=F32)
                    + b2_ref[kind])
    kc_ref[0, 0] = outs[0].astype(BF16)
    dk = NSA_HEAD_DIM
    vcT_ref[0, 0] = lax.dot_general(eye_ref[:dk, :dk], outs[1].astype(BF16), NT_DIMS,
                                    preferred_element_type=F32).astype(BF16)


def _compress(pc3, wc, pos, w1, b1, w2, b2, eye, first_col_block):
    bsz, seq, _ = pc3.shape
    n_grp, dk = NSA_KV_HEADS, NSA_HEAD_DIM
    n_half = seq // CMP_STRIDE
    const3 = lambda b, g: (0, 0, 0)
    return pl.pallas_call(
        _compress_kernel,
        grid=(bsz, n_grp),
        in_specs=[pl.BlockSpec((1, seq, LANES), lambda b, g: (b, 0, first_col_block + g)),
                  pl.BlockSpec(wc.shape, const3),
                  pl.BlockSpec(pos.shape, const3),
                  pl.BlockSpec(w1.shape, const3),
                  pl.BlockSpec(b1.shape, const3),
                  pl.BlockSpec(w2.shape, const3),
                  pl.BlockSpec(b2.shape, const3),
                  pl.BlockSpec(eye.shape, lambda b, g: (0, 0))],
        out_specs=[pl.BlockSpec((1, 1, n_half, dk), lambda b, g: (b, g, 0, 0)),
                   pl.BlockSpec((1, 1, dk, n_half), lambda b, g: (b, g, 0, 0))],
        out_shape=[jax.ShapeDtypeStruct((bsz, n_grp, n_half, dk), BF16),
                   jax.ShapeDtypeStruct((bsz, n_grp, dk, n_half), BF16)],
        compiler_params=_params(("parallel", "parallel")), name="nsa_compress",
    )(pc3, wc, pos, w1, b1, w2, b2, eye)


def _nsa_attn_kernel(q_ref, ks_ref, kw_ref, v_ref, kc_ref, vcT_ref, ov_ref, gl_ref, eye_ref,
                     aux_ref, o_ref, *scratch):
    tq = NSA_TQ
    first_tile = pl.program_id(2) * NSA_TILES_PER_STEP

    def one_tile(i, carry):
        rows_i = pl.ds(pl.multiple_of(i * tq, tq), tq)
        _nsa_query_tile(first_tile + i, q_ref.at[:, rows_i, :], ks_ref, kw_ref, v_ref, kc_ref,
                        vcT_ref, ov_ref, gl_ref.at[:, rows_i, :], eye_ref, aux_ref,
                        o_ref.at[:, rows_i, :], *scratch)
        return carry

    lax.fori_loop(0, NSA_TILES_PER_STEP, one_tile, 0)


def _nsa_query_tile(qt, q_ref, ks_ref, kw_ref, v_ref, kc_ref, vcT_ref, ov_ref, gl_ref, eye_ref,
                    aux_ref, o_ref,
                    ksa_ref, kwa_ref, vsT_ref, vwT_ref, sel_ref, s_ref, m_ref, acc_ref,
                    sw_ref, mw_ref, accw_ref, part_ref, gate_ref, tmax_ref):
    tq, tk = NSA_TQ, NSA_TK
    rows = NSA_REP * tq
    dk = NSA_HEAD_DIM
    grp = pl.program_id(1)
    qs = qt * tq
    seq = ks_ref.shape[1]
    eye = eye_ref[...]
    tq_per_tk = tk // tq

    @pl.when(qt == 0)
    def _():
        lane = lax.broadcasted_iota(jnp.int32, (tk, LANES), 1)
        lane_flag = jnp.where(lane == dk, 1.0, 0.0).astype(BF16)
        ones_rows = jnp.ones((NSA_AUX, tk), BF16)

        def build(j, carry):
            r0 = pl.multiple_of(j * tk, tk)
            ksa_ref[pl.ds(r0, tk), :] = ks_ref[0, pl.ds(r0, tk), :] + aux_ref[...]
            kwa_ref[pl.ds(r0, tk), :] = kw_ref[0, pl.ds(r0, tk), :] + lane_flag
            vT = lax.dot_general(eye[:LANES, :LANES], v_ref[0, pl.ds(r0, tk), :], NT_DIMS,
                                 preferred_element_type=F32).astype(BF16)
            vsT_ref[j, 0:dk, :] = vT[:dk]
            vsT_ref[j, dk:, :] = ones_rows
            for i in range(tq_per_tk):
                vwT_ref[j * tq_per_tk + i, 0:dk, :] = vT[dk:, i * tq:(i + 1) * tq]
                vwT_ref[j * tq_per_tk + i, dk:, :] = ones_rows[:, :tq]
            return carry

        lax.fori_loop(0, seq // tk, build, 0)

    q4 = lax.dot_general(eye[:NSA_REP * dk, :NSA_REP * dk], q_ref[0], NT_DIMS,
                         preferred_element_type=F32).astype(BF16)
    qT = jnp.concatenate([q4[r * dk:(r + 1) * dk, :] for r in range(NSA_REP)], axis=1)
    t_q = qs + lax.broadcasted_iota(jnp.int32, (1, tq), 1)
    t_row = jnp.concatenate([t_q] * NSA_REP, axis=1)
    zpad = jnp.zeros((LANES - dk - NSA_AUX, rows), BF16)

    def per_head(a):
        return jnp.concatenate([a] * NSA_REP, axis=1)

    gate_ref[...] = _dot_exact_lhs(eye[:LANES, :LANES], jax.nn.sigmoid(gl_ref[0]), NT_DIMS)
    gates = gate_ref[pl.ds(pl.multiple_of(grp * NSA_AUX, NSA_AUX), NSA_AUX), :]

    n_cmp = kc_ref.shape[2]
    s_c = jnp.dot(kc_ref[0, 0], qT, preferred_element_type=F32)
    c_end = lax.broadcasted_iota(jnp.int32, (n_cmp, 1), 0) * CMP_STRIDE + (CMP_BLK - 1)
    s_c = s_c + per_head(jnp.where(c_end <= t_q, 0.0, MASK_VALUE))
    mx_c = jnp.max(s_c, axis=0, keepdims=True)
    p_c = jnp.exp2(s_c - mx_c)
    l_c = jnp.sum(p_c, axis=0, keepdims=True)
    p_c = p_c * jnp.where(t_row >= CMP_BLK - 1, 1.0 / l_c, 0.0)
    o_c = jnp.dot(vcT_ref[0, 0], p_c.astype(BF16), preferred_element_type=F32)
    for r in range(NSA_REP):
        part_ref[r * dk:(r + 1) * dk, :] = gates[3 * r:3 * r + 1] * o_c[:, r * tq:(r + 1) * tq]

    p_sum = p_c[:, 0:tq]
    for r in range(1, NSA_REP):
        p_sum = p_sum + p_c[:, r * tq:(r + 1) * tq]
    ov = ov_ref[...]
    imp = _dot_exact_lhs(ov, p_sum)
    n_blk = ov.shape[0]
    j_blk = lax.broadcasted_iota(jnp.int32, (n_blk, 1), 0)
    cur = t_q // SEL_BLK
    forced = (j_blk == 0) | (j_blk == cur) | (j_blk == cur - 1)
    valid = j_blk * SEL_BLK <= t_q
    n_rank = N_SEL - 3
    cand = jnp.where(forced, -jnp.inf, jnp.where(valid, imp, -1.0))

    n_wt = WINDOW // tq + 1
    row0 = lax.broadcasted_iota(jnp.int32, (NSA_AUX, rows), 0) == 0
    a_k = lax.broadcasted_iota(jnp.int32, (tq, 1), 0)
    b_q = lax.broadcasted_iota(jnp.int32, (1, tq), 1)
    w_tiles = [jnp.maximum(qt - (n_wt - 1) + i, 0) for i in range(n_wt)]

    def win_scores(i):
        before_start = qt - (n_wt - 1) + i < 0
        flag = jnp.where(row0 & before_start, MASK_VALUE, 0.0).astype(BF16)
        rhs_w = jnp.concatenate([qT, flag, zpad], axis=0)
        k_tile = kwa_ref[pl.ds(pl.multiple_of(w_tiles[i] * tq, tq), tq), :]
        s = jnp.dot(k_tile, rhs_w, preferred_element_type=F32)
        if i == 0:
            s = s + per_head(jnp.where(a_k > b_q, 0.0, MASK_VALUE))
        if i == n_wt - 1:
            s = s + per_head(jnp.where(a_k <= b_q, 0.0, MASK_VALUE))
        sw_ref[i * tq:(i + 1) * tq, :] = s

    def win_max():
        mw_ref[...] = jnp.max(sw_ref[...], axis=0, keepdims=True)

    def win_pv(i):
        p = jnp.exp2(sw_ref[i * tq:(i + 1) * tq, :] - mw_ref[...]).astype(BF16)
        pv = jnp.dot(vwT_ref[w_tiles[i]], p, preferred_element_type=F32)
        if i == 0:
            accw_ref[...] = pv
        else:
            accw_ref[...] = accw_ref[...] + pv

    win_steps = ([functools.partial(win_scores, i) for i in range(n_wt)] + [win_max]
                 + [functools.partial(win_pv, i) for i in range(n_wt)])

    work = cand
    maxima = []
    for r in range(n_rank):
        mx = jnp.max(work, axis=0, keepdims=True)
        maxima.append(mx)
        work = jnp.where(work == mx, -jnp.inf, work)
        if r < len(win_steps):
            win_steps[r]()
    for step in win_steps[n_rank:]:
        step()

    ones_lhs = jnp.ones((SUBLANES, n_blk), BF16)

    def count(mask):
        return jnp.dot(ones_lhs, jnp.where(mask, 1.0, 0.0).astype(BF16),
                       preferred_element_type=F32)[0:1]

    thr = maxima[-1]
    for mx in reversed(maxima[:-1]):
        thr = jnp.where(count(cand >= mx) >= n_rank, mx, thr)
    ties = cand == thr
    below = (lax.broadcasted_iota(jnp.int32, (n_blk, n_blk), 1)
             < lax.broadcasted_iota(jnp.int32, (n_blk, n_blk), 0))
    tie_rank = jnp.dot(jnp.where(below, 1.0, 0.0).astype(BF16),
                       jnp.where(ties, 1.0, 0.0).astype(BF16), preferred_element_type=F32)
    picked = (cand > thr) | (ties & (tie_rank < n_rank - count(cand > thr)))
    sel_ref[0:n_blk, :] = jnp.where(forced | (picked & (cand >= 0.0)), 0.0, MASK_VALUE)
    sel_ref[n_blk:, :] = jnp.zeros((NSA_AUX, tq), F32)

    acc_w = accw_ref[...]
    o_w = acc_w[:dk] * (1.0 / acc_w[dk:dk + 1])
    for r in range(NSA_REP):
        part_ref[r * dk:(r + 1) * dk, :] = (part_ref[r * dk:(r + 1) * dk, :]
                                            + gates[3 * r + 2:3 * r + 3] * o_w[:, r * tq:(r + 1) * tq])

    blk_per_tile = tk // SEL_BLK
    n_full = qs // tk

    def scores(kt, slot, diagonal=False):
        sb = sel_ref[pl.ds(pl.multiple_of(kt * blk_per_tile, blk_per_tile), NSA_AUX), :]
        rhs = jnp.concatenate([qT, per_head(sb).astype(BF16), zpad], axis=0)
        k_tile = ksa_ref[pl.ds(pl.multiple_of(kt * tk, tk), tk), :]
        s = jnp.dot(k_tile, rhs, preferred_element_type=F32)
        if diagonal:
            kpos = kt * tk + lax.broadcasted_iota(jnp.int32, (tk, 1), 0)
            s = s + per_head(jnp.where(kpos <= t_q, 0.0, MASK_VALUE))
        s_ref[slot] = s
        tmax_ref[slot] = jnp.max(s, axis=0, keepdims=True)

    def softmax_pv(kt, slot):
        m_i = m_ref[...]
        m_new = jnp.maximum(m_i, tmax_ref[slot])
        alpha = jnp.exp2(m_i - m_new)
        pv = None
        for c in range(tk // PV_CHUNK):
            keys = slice(c * PV_CHUNK, (c + 1) * PV_CHUNK)
            p = jnp.exp2(s_ref[slot, keys, :] - m_new).astype(BF16)
            term = jnp.dot(vsT_ref[kt, :, keys], p, preferred_element_type=F32)
            pv = term if pv is None else pv + term
        acc_ref[...] = alpha * acc_ref[...] + pv
        m_ref[...] = m_new

    m_ref[...] = jnp.full((1, rows), MASK_VALUE, F32)
    acc_ref[...] = jnp.zeros((dk + NSA_AUX, rows), F32)
    scores(n_full, 0, diagonal=True)

    n_pairs = n_full // 2

    def slot0_tile(i):
        return jnp.where(i == 0, n_full, 2 * i - 1)

    def pair_step(i, carry):
        scores(2 * i, 1)
        softmax_pv(slot0_tile(i), 0)
        scores(2 * i + 1, 0)
        softmax_pv(2 * i, 1)
        return carry

    def quad_step(i, carry):
        pair_step(2 * i, carry)
        return pair_step(2 * i + 1, carry)

    n_quads = n_pairs // 2
    lax.fori_loop(0, n_quads, quad_step, 0)
    lax.fori_loop(2 * n_quads, n_pairs, pair_step, 0)

    @pl.when(n_full % 2 == 1)
    def _():
        scores(2 * n_pairs, 1)
        softmax_pv(slot0_tile(n_pairs), 0)
        softmax_pv(2 * n_pairs, 1)

    @pl.when(n_full % 2 == 0)
    def _():
        softmax_pv(slot0_tile(n_pairs), 0)

    acc_s = acc_ref[...]
    o_s = acc_s[:dk] * (1.0 / acc_s[dk:dk + 1])
    gates = gate_ref[pl.ds(pl.multiple_of(grp * NSA_AUX, NSA_AUX), NSA_AUX), :]
    out_t = jnp.concatenate(
        [part_ref[r * dk:(r + 1) * dk, :] + gates[3 * r + 1:3 * r + 2] * o_s[:, r * tq:(r + 1) * tq]
         for r in range(NSA_REP)], axis=0).astype(BF16)
    o_ref[0] = lax.dot_general(eye[:tq, :tq], out_t, NT_DIMS,
                               preferred_element_type=F32).astype(o_ref.dtype)


def _nsa_attn(p_att3, aux3, small_block, kc, vcT, overlap_t, eye, aux_s):
    bsz, seq, _ = p_att3.shape
    n_grp, dk = NSA_KV_HEADS, NSA_HEAD_DIM
    tq, tk = NSA_TQ, NSA_TK
    rows = NSA_REP * tq
    q_w = NSA_REP * dk
    n_cmp = kc.shape[2]
    n_blk = seq // SEL_BLK
    v_rows = dk + NSA_AUX
    n_wt = WINDOW // tq + 1
    first = NSA_HEADS * dk // LANES
    kv_spec = lambda off: pl.BlockSpec((1, seq, LANES), lambda b, g, t: (b, 0, first + off + g))
    step_rows = NSA_TILES_PER_STEP * tq
    return pl.pallas_call(
        _nsa_attn_kernel,
        grid=(bsz, n_grp, seq // step_rows),
        in_specs=[
            pl.BlockSpec((1, step_rows, q_w), lambda b, g, t: (b, t, g)),
            kv_spec(0), kv_spec(n_grp), kv_spec(2 * n_grp),
            pl.BlockSpec((1, 1, n_cmp, dk), lambda b, g, t: (b, g, 0, 0)),
            pl.BlockSpec((1, 1, dk, n_cmp), lambda b, g, t: (b, g, 0, 0)),
            pl.BlockSpec((n_blk, n_cmp), lambda b, g, t: (0, 0)),
            pl.BlockSpec((1, step_rows, LANES), lambda b, g, t: (b, t, small_block)),
            pl.BlockSpec(eye.shape, lambda b, g, t: (0, 0)),
            pl.BlockSpec((tk, LANES), lambda b, g, t: (0, 0)),
        ],
        out_specs=pl.BlockSpec((1, step_rows, q_w), lambda b, g, t: (b, t, g)),
        out_shape=jax.ShapeDtypeStruct((bsz, seq, NSA_HEADS * dk), BF16),
        scratch_shapes=[pltpu.VMEM((seq, LANES), BF16),
                        pltpu.VMEM((seq, LANES), BF16),
                        pltpu.VMEM((seq // tk, v_rows, tk), BF16),
                        pltpu.VMEM((seq // tq, v_rows, tq), BF16),
                        pltpu.VMEM((n_blk + NSA_AUX, tq), F32),
                        pltpu.VMEM((2, tk, rows), F32),
                        pltpu.VMEM((1, rows), F32),
                        pltpu.VMEM((v_rows, rows), F32),
                        pltpu.VMEM((n_wt * tq, rows), F32),
                        pltpu.VMEM((1, rows), F32),
                        pltpu.VMEM((v_rows, rows), F32),
                        pltpu.VMEM((q_w, tq), F32),
                        pltpu.VMEM((LANES, tq), F32),
                        pltpu.VMEM((2, 1, rows), F32)],
        compiler_params=_params(("parallel", "parallel", "arbitrary")), name="nsa_attention",
    )(p_att3, p_att3, p_att3, p_att3, kc, vcT, overlap_t, aux3, eye, aux_s)


def _bmm(a, b, dims):
    a = a.astype(BF16)
    b = b.astype(BF16)
    return jnp.stack([lax.dot_general(a[n], b[n], dims, preferred_element_type=F32)
                      for n in range(a.shape[0])])


def _gdn_kernel(y_ref, z_ref, small_ref, alog_ref, dtb_ref, nw_ref, o_ref, state_ref):
    bsz = y_ref.shape[0]
    cs = GDN_CHUNK
    hd = GDN_HEAD_DIM
    nh = GDN_HEADS
    c = pl.program_id(0)

    @pl.when(c == 0)
    def _():
        state_ref[...] = jnp.zeros_like(state_ref)

    y = y_ref[...]

    small = small_ref[...]
    sp_in = small + dtb_ref[...]
    softplus = jnp.maximum(sp_in, 0.0) + jnp.log1p(jnp.exp(-jnp.abs(sp_in)))
    g_all = -jnp.exp(alog_ref[...]) * softplus
    beta_all = jax.nn.sigmoid(small)
    ii = lax.broadcasted_iota(jnp.int32, (cs, cs), 0)
    jj = lax.broadcasted_iota(jnp.int32, (cs, cs), 1)
    incl = ii >= jj
    strict = ii > jj
    tril = jnp.where(incl, 1.0, 0.0).astype(BF16)
    e128 = jnp.where(lax.broadcasted_iota(jnp.int32, (LANES, LANES), 0)
                     == lax.broadcasted_iota(jnp.int32, (LANES, LANES), 1), 1.0, 0.0).astype(BF16)
    gc_all = [_dot_exact_lhs(tril, g_all[b]) for b in range(bsz)]
    gcT_all = [_dot_exact_lhs(e128, gc_all[b], NT_DIMS) for b in range(bsz)]

    chains = [(b, h) for b in range(bsz) for h in range(nh)]

    def heads(a, base):
        return jnp.stack([a[b][:, base + h * hd:base + (h + 1) * hd] for b, h in chains])

    def lane_bcast(cols, base):
        return jnp.stack([jnp.broadcast_to(cols[b][:, base + h:base + h + 1], (cs, hd))
                          for b, h in chains])

    n_qk = nh * hd
    q = heads(y, 0)
    k = heads(y, n_qk)
    v = heads(y, 2 * n_qk)
    q = q * lax.rsqrt(jnp.sum(q * q, axis=-1, keepdims=True) + NORM_EPS) * (hd ** -0.5)
    k = k * lax.rsqrt(jnp.sum(k * k, axis=-1, keepdims=True) + NORM_EPS)
    gc = lane_bcast(gc_all, SMALL_A)
    beta = lane_bcast([beta_all[b] for b in range(bsz)], SMALL_B)
    gc_row = jnp.stack([gcT_all[b][SMALL_A + h:SMALL_A + h + 1, :] for b, h in chains])

    decay = jnp.exp(jnp.where(incl, gc[:, :, :cs] - gc_row, MASK_VALUE))
    kk = _bmm(k, k, NT_DIMS)
    m_full = jnp.where(strict, -(beta[:, :, :cs] * kk * decay), 0.0)
    blk_diff = ii ^ jj
    inv_m1 = jnp.where(blk_diff < 2, m_full, 0.0)
    size = 2
    while size < cs:
        m_off = jnp.where((blk_diff >= size) & (blk_diff < 2 * size), m_full, 0.0)
        t_mat = m_off + _bmm(inv_m1, m_off, NN_DIMS)
        inv_m1 = inv_m1 + t_mat + _bmm(t_mat, inv_m1, NN_DIMS)
        size *= 2
    e_gc = jnp.exp(gc)
    rhs = jnp.concatenate([v * beta, k * (beta * e_gc)], axis=2)
    sol = rhs + _bmm(inv_m1, rhs, NN_DIMS)
    u, w = sol[:, :, :hd], sol[:, :, hd:]
    qk = _bmm(q, k, NT_DIMS) * decay
    q_dec = q * e_gc
    gc_last = gc[:, cs - 1:cs, :]
    k_dec = k * jnp.exp(gc_last - gc)
    state = state_ref[...]
    v_new = u - _bmm(w, state, NN_DIMS)
    o = _bmm(q_dec, state, NN_DIMS) + _bmm(qk, v_new, NN_DIMS)
    state_ref[...] = state * jnp.exp(gc_last) + _bmm(k_dec, v_new, TN_DIMS)

    o = o * lax.rsqrt(jnp.mean(o * o, axis=-1, keepdims=True) + NORM_EPS) * nw_ref[...]
    z = heads(z_ref[...].astype(F32), 0)
    o = o * (z * jax.nn.sigmoid(z))
    for b in range(bsz):
        o_ref[b] = jnp.concatenate([o[b * nh + h] for h in range(nh)], axis=1).astype(o_ref.dtype)


def _gated_deltanet(qkv3, zm3, aux3, small_block, alog_pad, dtb_pad, norm_w):
    bsz, seq, qkv_cols = qkv3.shape
    cs = GDN_CHUNK
    z_cols = GDN_HEADS * GDN_HEAD_DIM
    out = pl.pallas_call(
        _gdn_kernel,
        grid=(seq // cs,),
        in_specs=[
            pl.BlockSpec((bsz, cs, qkv_cols), lambda c: (0, c, 0)),
            pl.BlockSpec((bsz, cs, z_cols), lambda c: (0, c, 0)),
            pl.BlockSpec((bsz, cs, LANES), lambda c: (0, c, small_block)),
            pl.BlockSpec((1, LANES), lambda c: (0, 0)),
            pl.BlockSpec((1, LANES), lambda c: (0, 0)),
            pl.BlockSpec((1, GDN_HEAD_DIM), lambda c: (0, 0)),
        ],
        out_specs=pl.BlockSpec((bsz, cs, z_cols), lambda c: (0, c, 0)),
        out_shape=jax.ShapeDtypeStruct((bsz, seq, z_cols), BF16),
        scratch_shapes=[pltpu.VMEM((bsz * GDN_HEADS, GDN_HEAD_DIM, GDN_HEAD_DIM), F32)],
        compiler_params=_params(("arbitrary",)), name="gated_deltanet",
    )(qkv3, zm3, aux3, alog_pad, dtb_pad, norm_w.reshape(1, GDN_HEAD_DIM))
    return out.reshape(bsz * seq, z_cols)


def _merge_kernel(ya_ref, yb_ref, ma_ref, mb_ref, x_ref, mod_ref, wa_ref, wb_ref, wo_ref,
                  nf_ref, x1_ref, h2_ref):
    pa = jnp.dot(ya_ref[...], wa_ref[...], preferred_element_type=F32)
    pb = jnp.dot(yb_ref[...], wb_ref[...], preferred_element_type=F32)
    mixed = (jax.nn.sigmoid(ma_ref[...].astype(F32)) * pa
             + jax.nn.sigmoid(mb_ref[...].astype(F32)) * pb)
    mod = mod_ref[0]
    x1 = x_ref[...] + mod[GT_M:GT_M + 1, :] * jnp.dot(
        mixed.astype(BF16), wo_ref[...], preferred_element_type=F32)
    x1_ref[...] = x1
    h2_ref[...] = _modulated_norm(x1, nf_ref[...], mod, SC_F, SH_F).astype(BF16)


def _merge_project(ya, yb, pc, merge_blk, x2, mod, w_pa, w_pb, w_o, norm_ffn, seq):
    t, d = x2.shape
    tiles_per_batch = seq // ROW_TILE
    row = lambda i: (i, 0)
    const = lambda i: (0, 0)
    return pl.pallas_call(
        _merge_kernel,
        grid=(t // ROW_TILE,),
        in_specs=[pl.BlockSpec((ROW_TILE, d), row),
                  pl.BlockSpec((ROW_TILE, d), row),
                  pl.BlockSpec((ROW_TILE, d), lambda i: (i, merge_blk)),
                  pl.BlockSpec((ROW_TILE, d), lambda i: (i, merge_blk + 1)),
                  pl.BlockSpec((ROW_TILE, d), row),
                  pl.BlockSpec((1, 6, d), lambda i: (i // tiles_per_batch, 0, 0)),
                  pl.BlockSpec((d, d), const),
                  pl.BlockSpec((d, d), const),
                  pl.BlockSpec((d, d), const),
                  pl.BlockSpec((1, d), const)],
        out_specs=[pl.BlockSpec((ROW_TILE, d), row), pl.BlockSpec((ROW_TILE, d), row)],
        out_shape=[jax.ShapeDtypeStruct((t, d), F32), jax.ShapeDtypeStruct((t, d), BF16)],
        compiler_params=_params(("parallel",)), name="merge_project",
    )(ya, yb, pc, pc, x2, mod, w_pa, w_pb, w_o, norm_ffn.reshape(1, d))


def _ffn_kernel(h_ref, halo_ref, res_ref, mod_ref, wg_ref, wv_ref, cwg_ref, cwv_ref,
                cbg_ref, cbv_ref, wd_ref, nfin_ref, o_ref, ug_ref, uv_ref,
                *, tiles_per_batch, last):
    i = pl.program_id(0)
    tm = h_ref.shape[0]
    first_of_batch = (i % tiles_per_batch) == 0
    lhs = jnp.concatenate([halo_ref[...], h_ref[...]], axis=0)

    def conv_half(w_ref, cw_ref, cb_ref, u_ref):
        u_all = jnp.dot(lhs, w_ref[...], preferred_element_type=F32)
        u_ref[SUBLANES:, :] = u_all[2 * SUBLANES:]
        u_ref[0:SUBLANES, :] = jnp.where(first_of_batch, 0.0, u_all[SUBLANES:2 * SUBLANES])
        cw = cw_ref[...]
        out = cb_ref[...] + cw[FFN_CONV - 1:FFN_CONV, :] * u_ref[SUBLANES:, :]
        for j in range(FFN_CONV - 1):
            off = SUBLANES - (FFN_CONV - 1) + j
            out = out + cw[j:j + 1, :] * u_ref[off:off + tm, :]
        return out

    gate = conv_half(wg_ref, cwg_ref, cbg_ref, ug_ref)
    val = conv_half(wv_ref, cwv_ref, cbv_ref, uv_ref)
    act = gate * jax.nn.sigmoid(gate) * val
    down = jnp.dot(act.astype(BF16), wd_ref[...], preferred_element_type=F32)
    y = res_ref[...] + mod_ref[0][GT_F:GT_F + 1, :] * down
    if last:
        ms = jnp.mean(y * y, axis=-1, keepdims=True)
        y = y * lax.rsqrt(ms + NORM_EPS) * nfin_ref[...]
    o_ref[...] = y


def _conv_ffn(h2, x1, mod, w_up, conv_w, conv_b, w_down, norm_final, seq):
    t, d = x1.shape
    tm, tf = ROW_TILE, FFN_TF
    n_f = FFN_DIM // tf
    tiles_per_batch = seq // tm
    halo_per_tile = tm // SUBLANES
    row = lambda i: (i, 0)
    res = x1
    for f in range(n_f):
        col = lambda i, c=f: (0, c)
        col_val = lambda i, c=n_f + f: (0, c)
        res = pl.pallas_call(
            functools.partial(_ffn_kernel, tiles_per_batch=tiles_per_batch, last=f == n_f - 1),
            grid=(t // tm,),
            in_specs=[pl.BlockSpec((tm, d), row),
                      pl.BlockSpec((SUBLANES * 2, d),
                                   lambda i: (jnp.maximum(i * (halo_per_tile // 2) - 1, 0), 0)),
                      pl.BlockSpec((tm, d), row),
                      pl.BlockSpec((1, 6, d), lambda i: (i // tiles_per_batch, 0, 0)),
                      pl.BlockSpec((d, tf), col),
                      pl.BlockSpec((d, tf), col_val),
                      pl.BlockSpec((FFN_CONV, tf), col),
                      pl.BlockSpec((FFN_CONV, tf), col_val),
                      pl.BlockSpec((1, tf), col),
                      pl.BlockSpec((1, tf), col_val),
                      pl.BlockSpec((tf, d), lambda i, c=f: (c, 0)),
                      pl.BlockSpec((1, d), lambda i: (0, 0))],
            out_specs=pl.BlockSpec((tm, d), row),
            out_shape=jax.ShapeDtypeStruct((t, d), F32),
            scratch_shapes=[pltpu.VMEM((tm + SUBLANES, tf), F32),
                            pltpu.VMEM((tm + SUBLANES, tf), F32)],
            compiler_params=_params(("parallel",)), name="conv_ffn",
        )(h2, h2, res, mod, w_up, w_up, conv_w, conv_w, conv_b, conv_b, w_down,
          norm_final.reshape(1, d))
    return res


def _overlap_t(seq):
    n_cmp = (seq - CMP_BLK) // CMP_STRIDE + 1
    n_blk = seq // SEL_BLK
    c_start = np.arange(n_cmp) * CMP_STRIDE
    s_start = np.arange(n_blk) * SEL_BLK
    ov = np.clip(np.minimum(c_start[:, None] + CMP_BLK, s_start[None, :] + SEL_BLK)
                 - np.maximum(c_start[:, None], s_start[None, :]), 0, None) / CMP_BLK
    n_half = seq // CMP_STRIDE
    out = np.zeros((n_blk, n_half), np.float32)
    out[:, :n_cmp] = ov.T
    return jnp.asarray(out, BF16)


def kernel(x, c, w_ada, b_ada, norm_mix, w_in, nsa_pos_k, nsa_pos_v, nsa_ck_w1, nsa_ck_b1, nsa_ck_w2, nsa_ck_b2, nsa_cv_w1, nsa_cv_b1, nsa_cv_w2, nsa_cv_b2, gdn_conv, gdn_a_log, gdn_dt_bias, gdn_norm, w_proj_nsa, w_proj_gdn, w_out, norm_ffn, ffn_up, ffn_conv, ffn_conv_b, ffn_down, norm_final):
    bsz, seq, d = x.shape
    depth = w_ada.shape[0]
    n_grp, dk = NSA_KV_HEADS, NSA_HEAD_DIM
    q_cols = NSA_HEADS * dk
    gqkv_cols = 3 * GDN_HEADS * GDN_HEAD_DIM
    z_cols = GDN_HEADS * GDN_HEAD_DIM
    o_kv = q_cols
    o_gl = o_kv + 6 * n_grp * dk
    o_gq = o_gl + 3 * NSA_HEADS
    o_a = o_gq + gqkv_cols
    o_z = o_a + 2 * GDN_HEADS
    overlap_t = _overlap_t(seq)
    x2 = x.reshape(bsz * seq, d)
    assert depth == 1, "kernel supports the problem's DEPTH == 1"
    for l in range(depth):
        mod = _adaln_mod(c, w_ada[l], b_ada[l])
        w = w_in[l]
        kv_w = w[:, o_kv:o_gl].reshape(d, 6, n_grp, dk)
        zero = jnp.zeros((d, n_grp, dk), F32)
        pair = lambda a, b: jnp.concatenate([a, b], axis=2).reshape(d, n_grp * 2 * dk)
        w_att = jnp.concatenate([w[:, :o_kv] * (dk ** -0.5 * LOG2_E), pair(kv_w[:, 2], zero),
                                 pair(kv_w[:, 4], zero), pair(kv_w[:, 3], kv_w[:, 5])],
                                axis=1).astype(BF16)
        gate_w = jnp.pad(w[:, o_gl:o_gq].reshape(d, n_grp, 3 * NSA_REP),
                         ((0, 0), (0, 0), (0, NSA_AUX - 3 * NSA_REP))).reshape(d, SMALL_A)
        w_small = jnp.concatenate([gate_w, w[:, o_a:o_z],
                                   jnp.zeros((d, LANES - SMALL_B - GDN_HEADS), F32)],
                                  axis=1).astype(BF16)
        w_aux = jnp.concatenate([pair(kv_w[:, 0], kv_w[:, 1]).astype(BF16), w_small], axis=1)
        small_block = n_grp
        w_zm = w[:, o_z:].astype(BF16)

        p_att, h = _norm_mod_matmul(x2, norm_mix[l], mod, w_att, 1280, BF16, seq)
        qkv = _matmul_conv_silu(h, w[:, o_gq:o_a].astype(BF16), gdn_conv[l], 1536, seq)
        zm = _matmul(h, w_zm, 1536, BF16)
        aux = _matmul(h, w_aux, w_aux.shape[1], F32)
        aux3 = aux.reshape(bsz, seq, aux.shape[1])

        w1k = nsa_ck_w1[l].reshape(2, CMP_STRIDE, dk, CMP_HIDDEN)
        w1v = nsa_cv_w1[l].reshape(2, CMP_STRIDE, dk, CMP_HIDDEN)
        zw = jnp.zeros((CMP_STRIDE, dk, 2 * CMP_HIDDEN), F32)
        wc = jnp.concatenate(
            [jnp.concatenate([w1k[0], w1k[1], zw], axis=2),
             jnp.concatenate([zw, w1v[0], w1v[1]], axis=2)], axis=1).astype(BF16)
        pos = jnp.stack([nsa_pos_k[l], nsa_pos_v[l]]).reshape(2, 1, CMP_BLK * dk)
        pos = jnp.broadcast_to(pos, (2, 2 * SUBLANES, CMP_BLK * dk)).astype(BF16)
        cw1 = jnp.stack([nsa_ck_w1[l], nsa_cv_w1[l]]).astype(BF16)
        cb1 = jnp.stack([nsa_ck_b1[l], nsa_cv_b1[l]]).reshape(2, 1, CMP_HIDDEN)
        cw2 = jnp.stack([nsa_ck_w2[l], nsa_cv_w2[l]]).astype(BF16)
        cb2 = jnp.stack([nsa_ck_b2[l], nsa_cv_b2[l]]).reshape(2, 1, dk)
        eye = jnp.eye(max(NSA_REP * dk, NSA_TQ), dtype=BF16)
        kc, vcT = _compress(aux3, wc, pos, cw1, cb1, cw2, cb2, eye, 0)

        blk_in_tile = np.arange(NSA_TK) // SEL_BLK
        aux_s = np.zeros((NSA_TK, LANES), np.float32)
        aux_s[np.arange(NSA_TK), dk + blk_in_tile] = 1.0
        y_a = _nsa_attn(p_att.reshape(bsz, seq, p_att.shape[1]), aux3, small_block, kc, vcT,
                        overlap_t, eye, jnp.asarray(aux_s, BF16)).reshape(bsz * seq, q_cols)

        alog_pad = jnp.zeros((1, LANES), F32).at[0, SMALL_A:SMALL_A + GDN_HEADS].set(gdn_a_log[l])
        dtb_pad = jnp.zeros((1, LANES), F32).at[0, SMALL_A:SMALL_A + GDN_HEADS].set(gdn_dt_bias[l])
        y_b = _gated_deltanet(qkv.reshape(bsz, seq, gqkv_cols), zm.reshape(bsz, seq, zm.shape[1]),
                              aux3, small_block, alog_pad, dtb_pad, gdn_norm[l])

        x1, h2 = _merge_project(y_a, y_b, zm, z_cols // d, x2, mod, w_proj_nsa[l].astype(BF16),
                                w_proj_gdn[l].astype(BF16), w_out[l].astype(BF16),
                                norm_ffn[l], seq)
        out = _conv_ffn(h2, x1, mod, ffn_up[l].astype(BF16), ffn_conv[l],
                        ffn_conv_b[l].reshape(1, 2 * FFN_DIM), ffn_down[l].astype(BF16),
                        norm_final, seq)
    return out.reshape(bsz, seq, d)
```
